```python
import jax, jax.numpy as jnp
from jax import lax
import numpy as np

D_MODEL = 2048
BATCH = 2
SEQ = 8192
DEPTH = 1
DEC_BATCH = 2
DEC_SEQ = 16384
PAST_LEN = 128

MLA_HEADS = 8
Q_LORA = 512
KV_LORA = 512
QK_NOPE = 128
QK_ROPE = 64
V_DIM = 128
DIL_PATTERNS = ((128, 1), (512, 4), (2048, 16))
N_DIL_GROUPS = 3
HEADS_PER_GROUP = 4
DIL_HEAD_DIM = 128
DIL_WIDTH = N_DIL_GROUPS * HEADS_PER_GROUP * DIL_HEAD_DIM
D_FF = 5504
ROPE_THETA = 10000.0
Q_BLOCK = 128
LN_EPS = 1e-5
RMS_EPS = 1e-6
NEG_INF = -1e30
DEEPNORM_ALPHA = (2 * DEPTH) ** 0.25
DEEPNORM_BETA = (8 * DEPTH) ** -0.25
IN_COLS = Q_LORA + KV_LORA + QK_ROPE + 3 * DIL_WIDTH + 2 * D_MODEL

kernel_name = 'hybrid_mla_dilated_macaron_encoder'


def layer_norm(x, g, b):
    xf = x.astype(jnp.float32)
    mu = xf.mean(-1, keepdims=True)
    var = jnp.square(xf - mu).mean(-1, keepdims=True)
    return ((xf - mu) * lax.rsqrt(var + LN_EPS) * g.astype(jnp.float32) + b.astype(jnp.float32)).astype(x.dtype)


def rms_norm(x, g):
    xf = x.astype(jnp.float32)
    return (xf * lax.rsqrt(jnp.square(xf).mean(-1, keepdims=True) + RMS_EPS) * g.astype(jnp.float32)).astype(x.dtype)


def rope(x):
    S, d = x.shape[1], x.shape[-1]
    inv_freq = ROPE_THETA ** (-jnp.arange(0, d, 2, dtype=jnp.float32) / d)
    ang = jnp.arange(S, dtype=jnp.float32)[:, None] * inv_freq[None, :]
    cos = jnp.cos(ang)[None, :, None, :]
    sin = jnp.sin(ang)[None, :, None, :]
    xf = x.astype(jnp.float32)
    x1, x2 = jnp.split(xf, 2, axis=-1)
    return jnp.concatenate([x1 * cos - x2 * sin, x2 * cos + x1 * sin], axis=-1).astype(x.dtype)


def swiglu(x, w_in, w_out):
    g, u = jnp.split(x @ w_in, 2, axis=-1)
    return (jax.nn.silu(g) * u) @ w_out


def mla_attention(q_nope, q_pe, k_nope, k_pe, v):
    B, S, H, _ = q_nope.shape
    nq = S // Q_BLOCK
    scale = (QK_NOPE + QK_ROPE) ** -0.5

    def blocks(t):
        return t.reshape((B, nq, Q_BLOCK) + t.shape[2:]).swapaxes(0, 1)

    def attend(qs):
        qn, qp = qs
        s = (jnp.einsum('bqhd,bkhd->bhqk', qn, k_nope).astype(jnp.float32)
             + jnp.einsum('bqhd,bkd->bhqk', qp, k_pe).astype(jnp.float32)) * scale
        p = jax.nn.softmax(s, axis=-1)
        return jnp.einsum('bhqk,bkhd->bqhd', p.astype(v.dtype), v)

    o = lax.map(attend, (blocks(q_nope), blocks(q_pe)))
    return o.swapaxes(0, 1).reshape(B, S, H * V_DIM)


def banded_attention(q, k, v, half):
    N, L, H, dh = q.shape
    nb = -(-L // half)
    pad = nb * half - L
    qb = jnp.pad(q, ((0, 0), (0, pad), (0, 0), (0, 0))).reshape(N, nb, half, H, dh)

    def windows(t):
        tp = jnp.pad(t, ((0, 0), (half, half + pad), (0, 0), (0, 0))).reshape(N, nb + 2, half, H, dh)
        return jnp.concatenate([tp[:, :-2], tp[:, 1:-1], tp[:, 2:]], axis=2)

    kw, vw = windows(k), windows(v)
    blk = jnp.arange(nb)[:, None] * half
    qpos = blk + jnp.arange(half)[None, :]
    kpos = blk - half + jnp.arange(3 * half)[None, :]
    mask = ((jnp.abs(qpos[:, :, None] - kpos[:, None, :]) <= half)
            & (kpos[:, None, :] >= 0) & (kpos[:, None, :] < L))
    s = jnp.einsum('nbqhd,nbkhd->nbhqk', qb, kw).astype(jnp.float32) * (dh ** -0.5)
    s = jnp.where(mask[None, :, None], s, NEG_INF)
    m = s.max(-1, keepdims=True)
    p = jnp.exp(s - m)
    denom = p.sum(-1, keepdims=True)
    o = jnp.einsum('nbhqk,nbkhd->nbqhd', (p / denom).astype(v.dtype), vw)
    lse = (m + jnp.log(denom))[..., 0]
    o = o.reshape(N, nb * half, H, dh)[:, :L]
    lse = lse.transpose(0, 1, 3, 2).reshape(N, nb * half, H)[:, :L]
    return o, lse


def dilated_group(q, k, v, dilation, half):
    B, S, H, dh = q.shape
    L = S // dilation

    def to_streams(t):
        return t.reshape(B, L, dilation, H, dh).transpose(0, 2, 1, 3, 4).reshape(B * dilation, L, H, dh)

    o, lse = banded_attention(to_streams(q), to_streams(k), to_streams(v), half)
    o = o.reshape(B, dilation, L, H, dh).transpose(0, 2, 1, 3, 4).reshape(B, S, H, dh)
    lse = lse.reshape(B, dilation, L, H).transpose(0, 2, 1, 3).reshape(B, S, H)
    return o, lse


def token_mixer(h, w_in_mix, b_gate, q_norm_g, w_uq, kv_norm_g, w_ukv, w_branch_a, w_branch_b, w_out_mix):
    B, S, _ = h.shape
    z = h @ w_in_mix
    o1 = Q_LORA
    o2 = o1 + KV_LORA
    o3 = o2 + QK_ROPE
    o4 = o3 + 3 * DIL_WIDTH
    c_q, c_kv, k_rope, qkv_d, gate_logits = z[..., :o1], z[..., o1:o2], z[..., o2:o3], z[..., o3:o4], z[..., o4:]

    q = (rms_norm(c_q, q_norm_g) @ w_uq).reshape(B, S, MLA_HEADS, QK_NOPE + QK_ROPE)
    q_nope, q_pe = q[..., :QK_NOPE], rope(q[..., QK_NOPE:])
    kv = (rms_norm(c_kv, kv_norm_g) @ w_ukv).reshape(B, S, MLA_HEADS, QK_NOPE + V_DIM)
    k_nope, v_a = kv[..., :QK_NOPE], kv[..., QK_NOPE:]
    k_pe = rope(k_rope[:, :, None, :])[:, :, 0, :]
    out_a = mla_attention(q_nope, q_pe, k_nope, k_pe, v_a)

    n_h = N_DIL_GROUPS * HEADS_PER_GROUP
    qd, kd, vd = jnp.split(qkv_d, 3, axis=-1)
    qd = rope(qd.reshape(B, S, n_h, DIL_HEAD_DIM))
    kd = rope(kd.reshape(B, S, n_h, DIL_HEAD_DIM))
    vd = vd.reshape(B, S, n_h, DIL_HEAD_DIM)
    outs, lses = [], []
    for g, (window, dilation) in enumerate(DIL_PATTERNS):
        sl = slice(g * HEADS_PER_GROUP, (g + 1) * HEADS_PER_GROUP)
        o_g, l_g = dilated_group(qd[:, :, sl], kd[:, :, sl], vd[:, :, sl], dilation, window // (2 * dilation))
        outs.append(o_g)
        lses.append(l_g)
    wts = jax.nn.softmax(jnp.stack(lses, axis=0), axis=0)
    out_b = jnp.sum(wts[..., None] * jnp.stack(outs, axis=0).astype(jnp.float32), axis=0)
    out_b = out_b.astype(h.dtype).reshape(B, S, HEADS_PER_GROUP * DIL_HEAD_DIM)

    g_a, g_b = jnp.split(jax.nn.sigmoid(gate_logits + b_gate), 2, axis=-1)
    merged = g_a * (out_a @ w_branch_a) + g_b * (out_b @ w_branch_b)
    return merged @ w_out_mix


def encoder_layer(x, ffn1_w_in, ffn1_w_out, ln1_g, ln1_b, w_in_mix, b_gate, q_norm_g, w_uq, kv_norm_g, w_ukv,
                  w_branch_a, w_branch_b, w_out_mix, ln2_g, ln2_b, ffn2_w_in, ffn2_w_out, ln3_g, ln3_b):
    x = layer_norm(DEEPNORM_ALPHA * x + 0.5 * swiglu(x, ffn1_w_in, ffn1_w_out), ln1_g, ln1_b)
    x = layer_norm(DEEPNORM_ALPHA * x + token_mixer(x, w_in_mix, b_gate, q_norm_g, w_uq, kv_norm_g, w_ukv,
                                                    w_branch_a, w_branch_b, w_out_mix), ln2_g, ln2_b)
    x = layer_norm(DEEPNORM_ALPHA * x + 0.5 * swiglu(x, ffn2_w_in, ffn2_w_out), ln3_g, ln3_b)
    return x


def setup_inputs(seed: int = 0) -> dict:
    key = jax.random.key(seed)
    ks = jax.random.split(key, 21)

    def w(k, shape, fan_in, scale=1.0):
        return jax.random.normal(k, (DEPTH,) + shape, jnp.float32) * (scale * fan_in ** -0.5)

    def gain(k, n):
        return 1.0 + 0.01 * jax.random.normal(k, (DEPTH, n), jnp.float32)

    def bias(k, n):
        return 0.01 * jax.random.normal(k, (DEPTH, n), jnp.float32)

    return {
        'x_prompt': jax.random.normal(ks[0], (BATCH, SEQ, D_MODEL), jnp.float32),
        'x_sample': jax.random.normal(ks[1], (DEC_BATCH, DEC_SEQ, D_MODEL), jnp.float32),
        'ffn1_w_in': w(ks[2], (D_MODEL, 2 * D_FF), D_MODEL),
        'ffn1_w_out': w(ks[3], (D_FF, D_MODEL), D_FF, DEEPNORM_BETA),
        'ln1_g': gain(ks[4], D_MODEL),
        'ln1_b': bias(ks[5], D_MODEL),
        'w_in_mix': w(ks[6], (D_MODEL, IN_COLS), D_MODEL),
        'b_gate': bias(ks[7], 2 * D_MODEL),
        'q_norm_g': gain(ks[8], Q_LORA),
        'w_uq': w(ks[9], (Q_LORA, MLA_HEADS * (QK_NOPE + QK_ROPE)), Q_LORA),
        'kv_norm_g': gain(ks[10], KV_LORA),
        'w_ukv': w(ks[11], (KV_LORA, MLA_HEADS * (QK_NOPE + V_DIM)), KV_LORA),
        'w_branch_a': w(ks[12], (MLA_HEADS * V_DIM, D_MODEL), MLA_HEADS * V_DIM),
        'w_branch_b': w(ks[13], (HEADS_PER_GROUP * DIL_HEAD_DIM, D_MODEL), HEADS_PER_GROUP * DIL_HEAD_DIM),
        'w_out_mix': w(ks[14], (D_MODEL, D_MODEL), D_MODEL, DEEPNORM_BETA),
        'ln2_g': gain(ks[15], D_MODEL),
        'ln2_b': bias(ks[16], D_MODEL),
        'ffn2_w_in': w(ks[17], (D_MODEL, 2 * D_FF), D_MODEL),
        'ffn2_w_out': w(ks[18], (D_FF, D_MODEL), D_FF, DEEPNORM_BETA),
        'ln3_g': gain(ks[19], D_MODEL),
        'ln3_b': bias(ks[20], D_MODEL),
    }


def reference(x_prompt, x_sample, ffn1_w_in, ffn1_w_out, ln1_g, ln1_b, w_in_mix, b_gate, q_norm_g, w_uq,
              kv_norm_g, w_ukv, w_branch_a, w_branch_b, w_out_mix, ln2_g, ln2_b, ffn2_w_in, ffn2_w_out,
              ln3_g, ln3_b):
    y_prompt, y_sample = x_prompt, x_sample
    for i in range(DEPTH):
        params = (ffn1_w_in[i], ffn1_w_out[i], ln1_g[i], ln1_b[i], w_in_mix[i], b_gate[i], q_norm_g[i], w_uq[i],
                  kv_norm_g[i], w_ukv[i], w_branch_a[i], w_branch_b[i], w_out_mix[i], ln2_g[i], ln2_b[i],
                  ffn2_w_in[i], ffn2_w_out[i], ln3_g[i], ln3_b[i])
        y_prompt = encoder_layer(y_prompt, *params)
        y_sample = encoder_layer(y_sample, *params)
    return (y_prompt, y_sample)
```

```python
import functools

import jax
import jax.numpy as jnp
from jax import lax
from jax.experimental import pallas as pl
from jax.experimental.pallas import tpu as pltpu

D_MODEL = 2048
DEPTH = 1
MLA_HEADS = 8
Q_LORA = 512
KV_LORA = 512
QK_NOPE = 128
QK_ROPE = 64
V_DIM = 128
DIL_PATTERNS = ((128, 1), (512, 4), (2048, 16))
HEADS_PER_GROUP = 4
DIL_HEAD_DIM = 128
DIL_WIDTH = len(DIL_PATTERNS) * HEADS_PER_GROUP * DIL_HEAD_DIM
D_FF = 5504
ROPE_THETA = 10000.0
LN_EPS = 1e-5
RMS_EPS = 1e-6
NEG_INF = -1e30
ALPHA = (2 * DEPTH) ** 0.25
MLA_SCALE = (QK_NOPE + QK_ROPE) ** -0.5
DIL_SCALE = DIL_HEAD_DIM ** -0.5

LANES = 128
D_FF_PAD = 5632
FF_CHUNK = 512
LAT_WIDTH = Q_LORA + KV_LORA + LANES
MLA_SLAB = 2 * LANES
GROUP_WIDTH = HEADS_PER_GROUP * DIL_HEAD_DIM
VMEM_LIMIT = 56 * 1024 * 1024

BF16 = jnp.bfloat16
F32 = jnp.float32
_NT = (((1,), (1,)), ((), ()))


def _params(semantics):
    return pltpu.CompilerParams(dimension_semantics=semantics, vmem_limit_bytes=VMEM_LIMIT)


def _layer_norm(v, g, b):
    mu = jnp.mean(v, axis=-1, keepdims=True)
    c = v - mu
    var = jnp.mean(c * c, axis=-1, keepdims=True)
    return c * lax.rsqrt(var + LN_EPS) * g + b


def _ffn_ln_kernel(n_chunks, emit_bf16, x_ref, wg_ref, wu_ref, wo_ref, g_ref, b_ref, *rest):
    if emit_bf16:
        y_ref, ybf_ref, xbf_ref, acc_ref = rest
    else:
        y_ref, xbf_ref, acc_ref = rest
    k = pl.program_id(1)

    @pl.when(k == 0)
    def _():
        xbf_ref[...] = x_ref[...].astype(BF16)
        acc_ref[...] = jnp.zeros_like(acc_ref)

    xb = xbf_ref[...]
    gate = jnp.dot(xb, wg_ref[...], preferred_element_type=F32)
    up = jnp.dot(xb, wu_ref[...], preferred_element_type=F32)
    act = gate * jax.nn.sigmoid(gate) * up
    acc_ref[...] += jnp.dot(act.astype(BF16), wo_ref[...], preferred_element_type=F32)

    @pl.when(k == n_chunks - 1)
    def _():
        y = _layer_norm(ALPHA * x_ref[...] + 0.5 * acc_ref[...], g_ref[...], b_ref[...])
        y_ref[...] = y
        if emit_bf16:
            ybf_ref[...] = y.astype(BF16)


def _ffn_ln(x, wg, wu, wo, g, b, *, emit_bf16, tm=512):
    rows = x.shape[0]
    n_chunks = D_FF_PAD // FF_CHUNK
    out_shape = [jax.ShapeDtypeStruct((rows, D_MODEL), F32)]
    out_specs = [pl.BlockSpec((tm, D_MODEL), lambda i, k: (i, 0))]
    if emit_bf16:
        out_shape.append(jax.ShapeDtypeStruct((rows, D_MODEL), BF16))
        out_specs.append(pl.BlockSpec((tm, D_MODEL), lambda i, k: (i, 0)))
    return pl.pallas_call(
        functools.partial(_ffn_ln_kernel, n_chunks, emit_bf16),
        grid=(rows // tm, n_chunks),
        in_specs=[
            pl.BlockSpec((tm, D_MODEL), lambda i, k: (i, 0)),
            pl.BlockSpec((D_MODEL, FF_CHUNK), lambda i, k: (0, k)),
            pl.BlockSpec((D_MODEL, FF_CHUNK), lambda i, k: (0, k)),
            pl.BlockSpec((FF_CHUNK, D_MODEL), lambda i, k: (k, 0)),
            pl.BlockSpec((1, D_MODEL), lambda i, k: (0, 0)),
            pl.BlockSpec((1, D_MODEL), lambda i, k: (0, 0)),
        ],
        out_specs=out_specs,
        out_shape=out_shape,
        scratch_shapes=[pltpu.VMEM((tm, D_MODEL), BF16), pltpu.VMEM((tm, D_MODEL), F32)],
        compiler_params=_params(("parallel", "arbitrary")),
        name="ffn_ln",
    )(x, wg, wu, wo, g, b)


def _proj_kernel(mode, x_ref, w_ref, *rest):
    z = jnp.dot(x_ref[...], w_ref[...], preferred_element_type=F32)
    if mode == "rope":
        cos_ref, sin_ref, o_ref = rest
        cos = cos_ref[...]
        sin = sin_ref[...]
        for h in range(z.shape[1] // LANES):
            zh = z[:, h * LANES:(h + 1) * LANES]
            o_ref[:, h * LANES:(h + 1) * LANES] = (
                zh * cos + pltpu.roll(zh, LANES // 2, 1) * sin).astype(o_ref.dtype)
    elif mode == "gate":
        b_ref, o_ref = rest
        o_ref[...] = jax.nn.sigmoid(z + b_ref[...]).astype(o_ref.dtype)
    else:
        (o_ref,) = rest
        o_ref[...] = z.astype(o_ref.dtype)


def _proj(x, w, out_dtype, *, mode="plain", extras=(), seq=None, tm=1024, tn=512):
    rows = x.shape[0]
    n = w.shape[1]
    tn = min(tn, n)
    in_specs = [
        pl.BlockSpec((tm, D_MODEL), lambda i, j: (i, 0)),
        pl.BlockSpec((D_MODEL, tn), lambda i, j: (0, j)),
    ]
    if mode == "rope":
        tiles_per_seq = seq // tm
        in_specs += [pl.BlockSpec((tm, LANES), lambda i, j: (i % tiles_per_seq, 0))] * 2
    elif mode == "gate":
        in_specs += [pl.BlockSpec((1, tn), lambda i, j: (0, j))]
    return pl.pallas_call(
        functools.partial(_proj_kernel, mode),
        grid=(rows // tm, n // tn),
        in_specs=in_specs,
        out_specs=pl.BlockSpec((tm, tn), lambda i, j: (i, j)),
        out_shape=jax.ShapeDtypeStruct((rows, n), out_dtype),
        compiler_params=_params(("parallel", "arbitrary")),
        name="proj_" + mode,
    )(x, w, *extras)


def _latent_kernel(lat_ref, qg_ref, kvg_ref, wqn_ref, wqp_ref, wkn_ref, wv_ref,
                   cos_ref, sin_lo_ref, sin_hi_ref, q_ref, k_ref, v_ref):
    lat = lat_ref[...]
    c_q = lat[:, :Q_LORA]
    c_kv = lat[:, Q_LORA:Q_LORA + KV_LORA]
    k_rope = lat[:, Q_LORA + KV_LORA:]

    def rms(x, g):
        return (x * lax.rsqrt(jnp.mean(x * x, axis=-1, keepdims=True) + RMS_EPS) * g).astype(BF16)

    cos = cos_ref[...]
    sin_lo = sin_lo_ref[...]
    sin_hi = sin_hi_ref[...]

    def rope64(x):
        quarter = QK_ROPE // 2
        return (x * cos + pltpu.roll(x, LANES - quarter, 1) * sin_lo
                + pltpu.roll(x, quarter, 1) * sin_hi)

    nq = rms(c_q, qg_ref[...])
    nkv = rms(c_kv, kvg_ref[...])
    q_nope = jnp.dot(nq, wqn_ref[...], preferred_element_type=F32)
    q_pe = jnp.dot(nq, wqp_ref[...], preferred_element_type=F32)
    k_nope = jnp.dot(nkv, wkn_ref[...], preferred_element_type=F32)
    v = jnp.dot(nkv, wv_ref[...], preferred_element_type=F32)
    k_pe = rope64(k_rope).astype(BF16)
    for h in range(MLA_HEADS):
        src = slice(h * LANES, (h + 1) * LANES)
        lo = slice(h * MLA_SLAB, h * MLA_SLAB + LANES)
        hi = slice(h * MLA_SLAB + LANES, (h + 1) * MLA_SLAB)
        q_ref[:, lo] = (q_nope[:, src] * MLA_SCALE).astype(BF16)
        q_ref[:, hi] = (rope64(q_pe[:, src]) * MLA_SCALE).astype(BF16)
        k_ref[:, lo] = k_nope[:, src].astype(BF16)
        k_ref[:, hi] = k_pe
    v_ref[...] = v.astype(BF16)


def _latent(lat, qg, kvg, wqn, wqp, wkn, wv, tables, *, seq, tm=512):
    rows = lat.shape[0]
    tiles_per_seq = seq // tm
    width = MLA_HEADS * LANES
    row_spec = lambda w: pl.BlockSpec((tm, w), lambda i: (i, 0))
    full = lambda a: pl.BlockSpec(a.shape, lambda i: (0, 0), pipeline_mode=pl.Buffered(1))
    table_spec = pl.BlockSpec((tm, LANES), lambda i: (i % tiles_per_seq, 0))
    return pl.pallas_call(
        _latent_kernel,
        grid=(rows // tm,),
        in_specs=[row_spec(LAT_WIDTH), full(qg), full(kvg), full(wqn), full(wqp), full(wkn), full(wv),
                  table_spec, table_spec, table_spec],
        out_specs=[row_spec(MLA_HEADS * MLA_SLAB), row_spec(MLA_HEADS * MLA_SLAB), row_spec(width)],
        out_shape=[jax.ShapeDtypeStruct((rows, MLA_HEADS * MLA_SLAB), BF16),
                   jax.ShapeDtypeStruct((rows, MLA_HEADS * MLA_SLAB), BF16),
                   jax.ShapeDtypeStruct((rows, width), BF16)],
        compiler_params=_params(("parallel",)),
        name="latent",
    )(lat, qg, kvg, wqn, wqp, wkn, wv, *tables)


def _mla_kernel(n_kv, q_ref, k_ref, v_ref, o_ref, m_ref, l_ref, acc_ref):
    j = pl.program_id(3)

    @pl.when(j == 0)
    def _():
        m_ref[...] = jnp.full_like(m_ref, -jnp.inf)
        l_ref[...] = jnp.zeros_like(l_ref)
        acc_ref[...] = jnp.zeros_like(acc_ref)

    s = lax.dot_general(q_ref[...], k_ref[...], _NT, preferred_element_type=F32)
    m_prev = m_ref[...]
    m_new = jnp.maximum(m_prev, jnp.max(s, axis=1, keepdims=True))
    alpha = jnp.exp(m_prev - m_new)
    p = jnp.exp(s - m_new)
    l_ref[...] = alpha * l_ref[...] + jnp.sum(p, axis=1, keepdims=True)
    acc_ref[...] = alpha * acc_ref[...] + jnp.dot(p.astype(BF16), v_ref[...], preferred_element_type=F32)
    m_ref[...] = m_new

    @pl.when(j == n_kv - 1)
    def _():
        o_ref[...] = (acc_ref[...] / l_ref[...]).astype(o_ref.dtype)


def _mla(q, k, v, *, batch, seq, tq=512, tk=1024):
    n_q = seq // tq
    n_kv = seq // tk
    return pl.pallas_call(
        functools.partial(_mla_kernel, n_kv),
        grid=(batch, MLA_HEADS, n_q, n_kv),
        in_specs=[
            pl.BlockSpec((tq, MLA_SLAB), lambda b, h, i, j: (b * n_q + i, h)),
            pl.BlockSpec((tk, MLA_SLAB), lambda b, h, i, j: (b * n_kv + j, h)),
            pl.BlockSpec((tk, V_DIM), lambda b, h, i, j: (b * n_kv + j, h)),
        ],
        out_specs=pl.BlockSpec((tq, V_DIM), lambda b, h, i, j: (b * n_q + i, h)),
        out_shape=jax.ShapeDtypeStruct((batch * seq, MLA_HEADS * V_DIM), BF16),
        scratch_shapes=[pltpu.VMEM((tq, 1), F32), pltpu.VMEM((tq, 1), F32), pltpu.VMEM((tq, V_DIM), F32)],
        compiler_params=_params(("parallel", "parallel", "parallel", "arbitrary")),
        name="mla",
    )(q, k, v)


def _dilated_kernel(stream_len, half, q_ref, kp_ref, kc_ref, kn_ref, vp_ref, vc_ref, vn_ref, o_ref, lse_ref):
    i = pl.program_id(2)
    tl = q_ref.shape[0]
    row = lax.broadcasted_iota(jnp.int32, (tl, 3 * tl), 0)
    col = lax.broadcasted_iota(jnp.int32, (tl, 3 * tl), 1)
    q_pos = i * tl + row
    k_pos = (i - 1) * tl + col
    mask = (jnp.abs(q_pos - k_pos) <= half) & (k_pos >= 0) & (k_pos < stream_len)
    for h in range(HEADS_PER_GROUP):
        sl = slice(h * DIL_HEAD_DIM, (h + 1) * DIL_HEAD_DIM)
        q = q_ref[:, sl]
        s = jnp.concatenate(
            [lax.dot_general(q, ref[:, sl], _NT, preferred_element_type=F32) for ref in (kp_ref, kc_ref, kn_ref)],
            axis=1) * DIL_SCALE
        s = jnp.where(mask, s, NEG_INF)
        m = jnp.max(s, axis=1, keepdims=True)
        p = jnp.exp(s - m)
        denom = jnp.sum(p, axis=1, keepdims=True)
        pn = (p / denom).astype(BF16)
        o = (jnp.dot(pn[:, :tl], vp_ref[:, sl], preferred_element_type=F32)
             + jnp.dot(pn[:, tl:2 * tl], vc_ref[:, sl], preferred_element_type=F32)
             + jnp.dot(pn[:, 2 * tl:], vn_ref[:, sl], preferred_element_type=F32))
        o_ref[:, sl] = o
        lse_ref[:, sl] = jnp.broadcast_to(m + jnp.log(denom), (tl, DIL_HEAD_DIM))


def _dilated_group(qk, v, *, group, window, dilation, batch, seq, tl=128):
    stream_len = seq // dilation
    half = window // (2 * dilation)
    n_l = stream_len // tl
    n_groups = len(DIL_PATTERNS)
    qk_blocks = 2 * n_groups
    qk3 = qk.reshape(batch, stream_len, dilation * 2 * DIL_WIDTH)
    v3 = v.reshape(batch, stream_len, dilation * DIL_WIDTH)
    prev = lambda i: jnp.maximum(i - 1, 0)
    nxt = lambda i: jnp.minimum(i + 1, n_l - 1)
    blk = (None, tl, GROUP_WIDTH)
    k_col = lambda r: r * qk_blocks + n_groups + group
    v_col = lambda r: r * n_groups + group
    o, lse = pl.pallas_call(
        functools.partial(_dilated_kernel, stream_len, half),
        grid=(batch, dilation, n_l),
        in_specs=[
            pl.BlockSpec(blk, lambda b, r, i: (b, i, r * qk_blocks + group)),
            pl.BlockSpec(blk, lambda b, r, i: (b, prev(i), k_col(r))),
            pl.BlockSpec(blk, lambda b, r, i: (b, i, k_col(r))),
            pl.BlockSpec(blk, lambda b, r, i: (b, nxt(i), k_col(r))),
            pl.BlockSpec(blk, lambda b, r, i: (b, prev(i), v_col(r))),
            pl.BlockSpec(blk, lambda b, r, i: (b, i, v_col(r))),
            pl.BlockSpec(blk, lambda b, r, i: (b, nxt(i), v_col(r))),
        ],
        out_specs=[pl.BlockSpec(blk, lambda b, r, i: (b, i, r))] * 2,
        out_shape=[jax.ShapeDtypeStruct((batch, stream_len, dilation * GROUP_WIDTH), F32)] * 2,
        compiler_params=_params(("parallel", "parallel", "parallel")),
        name="dilated_g%d" % group,
    )(qk3, qk3, qk3, qk3, v3, v3, v3)
    return o.reshape(batch * seq, GROUP_WIDTH), lse.reshape(batch * seq, GROUP_WIDTH)


def _merge_kernel(oa_ref, o0_ref, o1_ref, o2_ref, l0_ref, l1_ref, l2_ref, ga_ref, gb_ref, y1_ref,
                  wa_ref, wb_ref, wo_ref, g_ref, b_ref, y_ref):
    l0, l1, l2 = l0_ref[...], l1_ref[...], l2_ref[...]
    m = jnp.maximum(jnp.maximum(l0, l1), l2)
    e0, e1, e2 = jnp.exp(l0 - m), jnp.exp(l1 - m), jnp.exp(l2 - m)
    denom = e0 + e1 + e2
    out_b = (e0 / denom) * o0_ref[...] + (e1 / denom) * o1_ref[...] + (e2 / denom) * o2_ref[...]
    branch_a = jnp.dot(oa_ref[...], wa_ref[...], preferred_element_type=F32)
    branch_b = jnp.dot(out_b.astype(BF16), wb_ref[...], preferred_element_type=F32)
    merged = ga_ref[...] * branch_a + gb_ref[...] * branch_b
    mix = jnp.dot(merged.astype(BF16), wo_ref[...], preferred_element_type=F32)
    y_ref[...] = _layer_norm(ALPHA * y1_ref[...] + mix, g_ref[...], b_ref[...])


def _merge(out_a, outs, lses, gates, y1, wa, wb, wo, g, b, *, tm=256):
    rows = y1.shape[0]
    row_spec = lambda w, c=0: pl.BlockSpec((tm, w), lambda i: (i, c))
    full = lambda a: pl.BlockSpec(a.shape, lambda i: (0, 0), pipeline_mode=pl.Buffered(1))
    return pl.pallas_call(
        _merge_kernel,
        grid=(rows // tm,),
        in_specs=[row_spec(MLA_HEADS * V_DIM)] + [row_spec(GROUP_WIDTH)] * 6
                 + [row_spec(D_MODEL, 0), row_spec(D_MODEL, 1), row_spec(D_MODEL)]
                 + [full(wa), full(wb), full(wo), full(g), full(b)],
        out_specs=row_spec(D_MODEL),
        out_shape=jax.ShapeDtypeStruct((rows, D_MODEL), F32),
        compiler_params=_params(("parallel",)),
        name="merge",
    )(out_a, *outs, *lses, gates, gates, y1, wa, wb, wo, g, b)


def _rope_tables(seq):
    pos = jnp.arange(seq, dtype=F32)[:, None]

    def angles(d):
        inv_freq = ROPE_THETA ** (-jnp.arange(0, d, 2, dtype=F32) / d)
        ang = pos * inv_freq[None, :]
        return jnp.cos(ang), jnp.sin(ang)

    cos, sin = angles(DIL_HEAD_DIM)
    dil = (jnp.concatenate([cos, cos], axis=1), jnp.concatenate([-sin, sin], axis=1))
    cos, sin = angles(QK_ROPE)
    zeros = jnp.zeros_like(cos)
    pad = jnp.zeros((seq, LANES - QK_ROPE), F32)
    mla = (jnp.concatenate([cos, cos, pad], axis=1),
           jnp.concatenate([-sin, zeros, pad], axis=1),
           jnp.concatenate([zeros, sin, pad], axis=1))
    return dil, mla


def _prepare(ffn1_w_in, ffn1_w_out, ln1_g, ln1_b, w_in_mix, b_gate, q_norm_g, w_uq, kv_norm_g, w_ukv,
             w_branch_a, w_branch_b, w_out_mix, ln2_g, ln2_b, ffn2_w_in, ffn2_w_out, ln3_g, ln3_b):
    def ffn(w_in, w_out):
        pad = D_FF_PAD - D_FF
        wg = jnp.pad(w_in[:, :D_FF], ((0, 0), (0, pad))).astype(BF16)
        wu = jnp.pad(w_in[:, D_FF:], ((0, 0), (0, pad))).astype(BF16)
        wo = jnp.pad(w_out, ((0, pad), (0, 0))).astype(BF16)
        return wg, wu, wo

    row = lambda a: a.reshape(1, -1)
    o_lat = Q_LORA + KV_LORA + QK_ROPE
    o_qk = o_lat + 2 * DIL_WIDTH
    o_v = o_qk + DIL_WIDTH
    w_lat = jnp.pad(w_in_mix[:, :o_lat], ((0, 0), (0, LAT_WIDTH - o_lat))).astype(BF16)
    uq = w_uq.reshape(Q_LORA, MLA_HEADS, QK_NOPE + QK_ROPE)
    ukv = w_ukv.reshape(KV_LORA, MLA_HEADS, QK_NOPE + V_DIM)
    flat = lambda a: a.reshape(a.shape[0], -1).astype(BF16)
    return dict(
        ffn1=ffn(ffn1_w_in, ffn1_w_out), ln1=(row(ln1_g), row(ln1_b)),
        ffn2=ffn(ffn2_w_in, ffn2_w_out), ln3=(row(ln3_g), row(ln3_b)),
        w_lat=w_lat,
        w_qk=w_in_mix[:, o_lat:o_qk].astype(BF16),
        w_v=w_in_mix[:, o_qk:o_v].astype(BF16),
        w_gate=w_in_mix[:, o_v:].astype(BF16),
        b_gate=row(b_gate),
        q_norm_g=row(q_norm_g), kv_norm_g=row(kv_norm_g),
        wqn=flat(uq[:, :, :QK_NOPE]),
        wqp=flat(jnp.pad(uq[:, :, QK_NOPE:], ((0, 0), (0, 0), (0, LANES - QK_ROPE)))),
        wkn=flat(ukv[:, :, :QK_NOPE]),
        wv=flat(ukv[:, :, QK_NOPE:]),
        wa=w_branch_a.astype(BF16), wb=w_branch_b.astype(BF16), wo=w_out_mix.astype(BF16),
        ln2=(row(ln2_g), row(ln2_b)),
    )


def _encoder_layer(x, p):
    batch, seq, _ = x.shape
    x2 = x.reshape(batch * seq, D_MODEL)
    dil_tables, mla_tables = _rope_tables(seq)

    y1, y1_bf = _ffn_ln(x2, *p["ffn1"], *p["ln1"], emit_bf16=True)

    lat = _proj(y1_bf, p["w_lat"], F32, tn=LAT_WIDTH // 3)
    qk_d = _proj(y1_bf, p["w_qk"], BF16, mode="rope", extras=dil_tables, seq=seq)
    v_d = _proj(y1_bf, p["w_v"], BF16)
    gates = _proj(y1_bf, p["w_gate"], F32, mode="gate", extras=(p["b_gate"],))

    q, k, v = _latent(lat, p["q_norm_g"], p["kv_norm_g"], p["wqn"], p["wqp"], p["wkn"], p["wv"],
                      mla_tables, seq=seq)
    out_a = _mla(q, k, v, batch=batch, seq=seq)

    outs, lses = [], []
    for group, (window, dilation) in enumerate(DIL_PATTERNS):
        o, lse = _dilated_group(qk_d, v_d, group=group, window=window, dilation=dilation, batch=batch, seq=seq)
        outs.append(o)
        lses.append(lse)

    y2 = _merge(out_a, outs, lses, gates, y1, p["wa"], p["wb"], p["wo"], *p["ln2"])
    (y3,) = _ffn_ln(y2, *p["ffn2"], *p["ln3"], emit_bf16=False)
    return y3.reshape(batch, seq, D_MODEL)


def kernel(x_prompt, x_sample, ffn1_w_in, ffn1_w_out, ln1_g, ln1_b, w_in_mix, b_gate, q_norm_g, w_uq, kv_norm_g, w_ukv, w_branch_a, w_branch_b, w_out_mix, ln2_g, ln2_b, ffn2_w_in, ffn2_w_out, ln3_g, ln3_b):
    weights = (ffn1_w_in, ffn1_w_out, ln1_g, ln1_b, w_in_mix, b_gate, q_norm_g, w_uq, kv_norm_g, w_ukv,
               w_branch_a, w_branch_b, w_out_mix, ln2_g, ln2_b, ffn2_w_in, ffn2_w_out, ln3_g, ln3_b)
    y_prompt, y_sample = x_prompt, x_sample
    for layer in range(DEPTH):
        p = _prepare(*(w[layer] for w in weights))
        y_prompt = _encoder_layer(y_prompt, p)
        y_sample = _encoder_layer(y_sample, p)
    return (y_prompt, y_sample)
```

```python
import functools

import jax
import jax.numpy as jnp
from jax import lax
from jax.experimental import pallas as pl
from jax.experimental.pallas import tpu as pltpu

D_MODEL = 2048
DEPTH = 1
MLA_HEADS = 8
Q_LORA = 512
KV_LORA = 512
QK_NOPE = 128
QK_ROPE = 64
V_DIM = 128
DIL_PATTERNS = ((128, 1), (512, 4), (2048, 16))
HEADS_PER_GROUP = 4
DIL_HEAD_DIM = 128
DIL_WIDTH = len(DIL_PATTERNS) * HEADS_PER_GROUP * DIL_HEAD_DIM
D_FF = 5504
ROPE_THETA = 10000.0
LN_EPS = 1e-5
RMS_EPS = 1e-6
NEG_INF = -1e30
ALPHA = (2 * DEPTH) ** 0.25
MLA_SCALE = (QK_NOPE + QK_ROPE) ** -0.5
MLA_Q_SCALE = MLA_SCALE * 1.4426950408889634
DIL_SCALE = DIL_HEAD_DIM ** -0.5

LANES = 128
D_FF_PAD = 5632
FF_CHUNK = 512
LAT_WIDTH = Q_LORA + KV_LORA + LANES
MLA_SLAB = 2 * LANES
GROUP_WIDTH = HEADS_PER_GROUP * DIL_HEAD_DIM
VMEM_LIMIT = 56 * 1024 * 1024

BF16 = jnp.bfloat16
F32 = jnp.float32
_NT = (((1,), (1,)), ((), ()))


def _params(semantics):
    return pltpu.CompilerParams(dimension_semantics=semantics, vmem_limit_bytes=VMEM_LIMIT)


def _layer_norm(v, g, b):
    mu = jnp.mean(v, axis=-1, keepdims=True)
    c = v - mu
    var = jnp.mean(c * c, axis=-1, keepdims=True)
    return c * lax.rsqrt(var + LN_EPS) * g + b


def _ffn_ln_kernel(n_chunks, emit_bf16, x_ref, wg_ref, wu_ref, wo_ref, g_ref, b_ref, *rest):
    if emit_bf16:
        y_ref, ybf_ref, xbf_ref, acc_ref = rest
    else:
        y_ref, xbf_ref, acc_ref = rest
    k = pl.program_id(1)

    @pl.when(k == 0)
    def _():
        xbf_ref[...] = x_ref[...].astype(BF16)
        acc_ref[...] = jnp.zeros_like(acc_ref)

    xb = xbf_ref[...]
    gate = jnp.dot(xb, wg_ref[...], preferred_element_type=F32)
    up = jnp.dot(xb, wu_ref[...], preferred_element_type=F32)
    act = gate * jax.nn.sigmoid(gate) * up
    acc_ref[...] += jnp.dot(act.astype(BF16), wo_ref[...], preferred_element_type=F32)

    @pl.when(k == n_chunks - 1)
    def _():
        y = _layer_norm(ALPHA * x_ref[...] + 0.5 * acc_ref[...], g_ref[...], b_ref[...])
        y_ref[...] = y
        if emit_bf16:
            ybf_ref[...] = y.astype(BF16)


def _ffn_ln(x, wg, wu, wo, g, b, *, emit_bf16, tm=512):
    rows = x.shape[0]
    n_chunks = D_FF_PAD // FF_CHUNK
    out_shape = [jax.ShapeDtypeStruct((rows, D_MODEL), F32)]
    out_specs = [pl.BlockSpec((tm, D_MODEL), lambda i, k: (i, 0))]
    if emit_bf16:
        out_shape.append(jax.ShapeDtypeStruct((rows, D_MODEL), BF16))
        out_specs.append(pl.BlockSpec((tm, D_MODEL), lambda i, k: (i, 0)))
    return pl.pallas_call(
        functools.partial(_ffn_ln_kernel, n_chunks, emit_bf16),
        grid=(rows // tm, n_chunks),
        in_specs=[
            pl.BlockSpec((tm, D_MODEL), lambda i, k: (i, 0)),
            pl.BlockSpec((D_MODEL, FF_CHUNK), lambda i, k: (0, k)),
            pl.BlockSpec((D_MODEL, FF_CHUNK), lambda i, k: (0, k)),
            pl.BlockSpec((FF_CHUNK, D_MODEL), lambda i, k: (k, 0)),
            pl.BlockSpec((1, D_MODEL), lambda i, k: (0, 0)),
            pl.BlockSpec((1, D_MODEL), lambda i, k: (0, 0)),
        ],
        out_specs=out_specs,
        out_shape=out_shape,
        scratch_shapes=[pltpu.VMEM((tm, D_MODEL), BF16), pltpu.VMEM((tm, D_MODEL), F32)],
        compiler_params=_params(("parallel", "arbitrary")),
        name="ffn_ln",
    )(x, wg, wu, wo, g, b)


def _proj_kernel(mode, x_ref, w_ref, *rest):
    z = jnp.dot(x_ref[...], w_ref[...], preferred_element_type=F32)
    if mode == "rope":
        cos_ref, sin_ref, o_ref = rest
        cos = cos_ref[...]
        sin = sin_ref[...]
        for h in range(z.shape[1] // LANES):
            zh = z[:, h * LANES:(h + 1) * LANES]
            o_ref[:, h * LANES:(h + 1) * LANES] = (
                zh * cos + pltpu.roll(zh, LANES // 2, 1) * sin).astype(o_ref.dtype)
    elif mode == "gate":
        b_ref, o_ref = rest
        o_ref[...] = jax.nn.sigmoid(z + b_ref[...]).astype(o_ref.dtype)
    else:
        (o_ref,) = rest
        o_ref[...] = z.astype(o_ref.dtype)


def _proj(x, w, out_dtype, *, mode="plain", extras=(), seq=None, tm=1024, tn=512):
    rows = x.shape[0]
    n = w.shape[1]
    tn = min(tn, n)
    in_specs = [
        pl.BlockSpec((tm, D_MODEL), lambda i, j: (i, 0)),
        pl.BlockSpec((D_MODEL, tn), lambda i, j: (0, j)),
    ]
    if mode == "rope":
        tiles_per_seq = seq // tm
        in_specs += [pl.BlockSpec((tm, LANES), lambda i, j: (i % tiles_per_seq, 0))] * 2
    elif mode == "gate":
        in_specs += [pl.BlockSpec((1, tn), lambda i, j: (0, j))]
    return pl.pallas_call(
        functools.partial(_proj_kernel, mode),
        grid=(rows // tm, n // tn),
        in_specs=in_specs,
        out_specs=pl.BlockSpec((tm, tn), lambda i, j: (i, j)),
        out_shape=jax.ShapeDtypeStruct((rows, n), out_dtype),
        compiler_params=_params(("parallel", "arbitrary")),
        name="proj_" + mode,
    )(x, w, *extras)


def _latent_kernel(lat_ref, qg_ref, kvg_ref, wqn_ref, wqp_ref, wkn_ref, wvt_ref,
                   cos_ref, sin_lo_ref, sin_hi_ref, q_ref, k_ref, vt_ref):
    lat = lat_ref[...]
    c_q = lat[:, :Q_LORA]
    c_kv = lat[:, Q_LORA:Q_LORA + KV_LORA]
    k_rope = lat[:, Q_LORA + KV_LORA:]

    def rms(x, g):
        return (x * lax.rsqrt(jnp.mean(x * x, axis=-1, keepdims=True) + RMS_EPS) * g).astype(BF16)

    cos = cos_ref[...]
    sin_lo = sin_lo_ref[...]
    sin_hi = sin_hi_ref[...]

    def rope64(x):
        quarter = QK_ROPE // 2
        return (x * cos + pltpu.roll(x, LANES - quarter, 1) * sin_lo
                + pltpu.roll(x, quarter, 1) * sin_hi)

    nq = rms(c_q, qg_ref[...])
    nkv = rms(c_kv, kvg_ref[...])
    q_nope = jnp.dot(nq, wqn_ref[...], preferred_element_type=F32)
    q_pe = jnp.dot(nq, wqp_ref[...], preferred_element_type=F32)
    k_nope = jnp.dot(nkv, wkn_ref[...], preferred_element_type=F32)
    vt = lax.dot_general(wvt_ref[...], nkv, _NT, preferred_element_type=F32)
    k_pe = rope64(k_rope).astype(BF16)
    for h in range(MLA_HEADS):
        src = slice(h * LANES, (h + 1) * LANES)
        lo = slice(h * MLA_SLAB, h * MLA_SLAB + LANES)
        hi = slice(h * MLA_SLAB + LANES, (h + 1) * MLA_SLAB)
        q_ref[:, lo] = (q_nope[:, src] * MLA_Q_SCALE).astype(BF16)
        q_ref[:, hi] = (rope64(q_pe[:, src]) * MLA_Q_SCALE).astype(BF16)
        k_ref[:, lo] = k_nope[:, src].astype(BF16)
        k_ref[:, hi] = k_pe
    vt_ref[...] = vt.astype(BF16)


def _latent(lat, qg, kvg, wqn, wqp, wkn, wvt, tables, *, seq, tm=512):
    rows = lat.shape[0]
    tiles_per_seq = seq // tm
    width = MLA_HEADS * LANES
    row_spec = lambda w: pl.BlockSpec((tm, w), lambda i: (i, 0))
    full = lambda a: pl.BlockSpec(a.shape, lambda i: (0, 0), pipeline_mode=pl.Buffered(1))
    table_spec = pl.BlockSpec((tm, LANES), lambda i: (i % tiles_per_seq, 0))
    return pl.pallas_call(
        _latent_kernel,
        grid=(rows // tm,),
        in_specs=[row_spec(LAT_WIDTH), full(qg), full(kvg), full(wqn), full(wqp), full(wkn), full(wvt),
                  table_spec, table_spec, table_spec],
        out_specs=[row_spec(MLA_HEADS * MLA_SLAB), row_spec(MLA_HEADS * MLA_SLAB),
                   pl.BlockSpec((width, tm), lambda i: (0, i))],
        out_shape=[jax.ShapeDtypeStruct((rows, MLA_HEADS * MLA_SLAB), BF16),
                   jax.ShapeDtypeStruct((rows, MLA_HEADS * MLA_SLAB), BF16),
                   jax.ShapeDtypeStruct((width, rows), BF16)],
        compiler_params=_params(("parallel",)),
        name="latent",
    )(lat, qg, kvg, wqn, wqp, wkn, wvt, *tables)


def _mla_kernel(n_kv, tk, q_ref, k_ref, vt_ref, o_ref, acc_ref, s_ref):
    n_sub, _, ts = acc_ref.shape
    acc_ref[...] = jnp.zeros_like(acc_ref)

    def scores_into(slot, j):
        k = k_ref[pl.ds(pl.multiple_of(j * tk, tk), tk), :]
        for t in range(n_sub):
            s_ref[slot, t] = lax.dot_general(k, q_ref[pl.ds(t * ts, ts), :], _NT,
                                             preferred_element_type=F32)

    def consume(slot, j, carry):
        vt = vt_ref[:, pl.ds(pl.multiple_of(j * tk, tk), tk)]
        out = []
        for t in range(n_sub):
            m_prev, l_prev = carry[t]
            s = s_ref[slot, t]
            m_new = jnp.maximum(m_prev, jnp.max(s, axis=0, keepdims=True))
            alpha = jnp.exp2(m_prev - m_new)
            p = jnp.exp2(s - m_new)
            l_new = alpha * l_prev + jnp.sum(p, axis=0, keepdims=True)
            acc_ref[t] = alpha * acc_ref[t] + jnp.dot(vt, p.astype(BF16), preferred_element_type=F32)
            out.append((m_new, l_new))
        return tuple(out)

    def body(jj, carry):
        j = 2 * jj
        scores_into(1, j + 1)
        carry = consume(0, j, carry)
        scores_into(0, j + 2)
        return consume(1, j + 1, carry)

    scores_into(0, 0)
    init = tuple((jnp.full((1, ts), -jnp.inf, F32), jnp.zeros((1, ts), F32)) for _ in range(n_sub))
    carry = lax.fori_loop(0, n_kv // 2 - 1, body, init)
    scores_into(1, n_kv - 1)
    carry = consume(0, n_kv - 2, carry)
    carry = consume(1, n_kv - 1, carry)
    for t in range(n_sub):
        o_ref[pl.ds(t * ts, ts), :] = (acc_ref[t] / carry[t][1]).T.astype(o_ref.dtype)


def _mla(q, k, vt, *, batch, seq, tq=2048, ts=512, tk=512):
    n_q = seq // tq
    return pl.pallas_call(
        functools.partial(_mla_kernel, seq // tk, tk),
        grid=(batch, MLA_HEADS, n_q),
        in_specs=[
            pl.BlockSpec((tq, MLA_SLAB), lambda b, h, i: (b * n_q + i, h)),
            pl.BlockSpec((seq, MLA_SLAB), lambda b, h, i: (b, h)),
            pl.BlockSpec((V_DIM, seq), lambda b, h, i: (h, b)),
        ],
        out_specs=pl.BlockSpec((tq, V_DIM), lambda b, h, i: (b * n_q + i, h)),
        out_shape=jax.ShapeDtypeStruct((batch * seq, MLA_HEADS * V_DIM), BF16),
        scratch_shapes=[pltpu.VMEM((tq // ts, V_DIM, ts), F32), pltpu.VMEM((2, tq // ts, tk, ts), F32)],
        compiler_params=_params(("parallel", "parallel", "arbitrary")),
        name="mla",
    )(q, k, vt)


def _dilated_kernel(stream_len, half, q_ref, kp_ref, kc_ref, kn_ref, vp_ref, vc_ref, vn_ref, o_ref, lse_ref):
    i = pl.program_id(2)
    tl = q_ref.shape[0]
    row = lax.broadcasted_iota(jnp.int32, (tl, 3 * tl), 0)
    col = lax.broadcasted_iota(jnp.int32, (tl, 3 * tl), 1)
    q_pos = i * tl + row
    k_pos = (i - 1) * tl + col
    mask = (jnp.abs(q_pos - k_pos) <= half) & (k_pos >= 0) & (k_pos < stream_len)
    for h in range(HEADS_PER_GROUP):
        sl = slice(h * DIL_HEAD_DIM, (h + 1) * DIL_HEAD_DIM)
        q = q_ref[:, sl]
        s = jnp.concatenate(
            [lax.dot_general(q, ref[:, sl], _NT, preferred_element_type=F32) for ref in (kp_ref, kc_ref, kn_ref)],
            axis=1) * DIL_SCALE
        s = jnp.where(mask, s, NEG_INF)
        m = jnp.max(s, axis=1, keepdims=True)
        p = jnp.exp(s - m)
        denom = jnp.sum(p, axis=1, keepdims=True)
        pn = (p / denom).astype(BF16)
        o = (jnp.dot(pn[:, :tl], vp_ref[:, sl], preferred_element_type=F32)
             + jnp.dot(pn[:, tl:2 * tl], vc_ref[:, sl], preferred_element_type=F32)
             + jnp.dot(pn[:, 2 * tl:], vn_ref[:, sl], preferred_element_type=F32))
        o_ref[:, sl] = o
        lse_ref[:, sl] = jnp.broadcast_to(m + jnp.log(denom), (tl, DIL_HEAD_DIM))


def _dilated_group(qk, v, *, group, window, dilation, batch, seq, tl=128):
    stream_len = seq // dilation
    half = window // (2 * dilation)
    n_l = stream_len // tl
    n_groups = len(DIL_PATTERNS)
    qk_blocks = 2 * n_groups
    qk3 = qk.reshape(batch, stream_len, dilation * 2 * DIL_WIDTH)
    v3 = v.reshape(batch, stream_len, dilation * DIL_WIDTH)
    prev = lambda i: jnp.maximum(i - 1, 0)
    nxt = lambda i: jnp.minimum(i + 1, n_l - 1)
    blk = (None, tl, GROUP_WIDTH)
    k_col = lambda r: r * qk_blocks + n_groups + group
    v_col = lambda r: r * n_groups + group
    o, lse = pl.pallas_call(
        functools.partial(_dilated_kernel, stream_len, half),
        grid=(batch, dilation, n_l),
        in_specs=[
            pl.BlockSpec(blk, lambda b, r, i: (b, i, r * qk_blocks + group)),
            pl.BlockSpec(blk, lambda b, r, i: (b, prev(i), k_col(r))),
            pl.BlockSpec(blk, lambda b, r, i: (b, i, k_col(r))),
            pl.BlockSpec(blk, lambda b, r, i: (b, nxt(i), k_col(r))),
            pl.BlockSpec(blk, lambda b, r, i: (b, prev(i), v_col(r))),
            pl.BlockSpec(blk, lambda b, r, i: (b, i, v_col(r))),
            pl.BlockSpec(blk, lambda b, r, i: (b, nxt(i), v_col(r))),
        ],
        out_specs=[pl.BlockSpec(blk, lambda b, r, i: (b, i, r))] * 2,
        out_shape=[jax.ShapeDtypeStruct((batch, stream_len, dilation * GROUP_WIDTH), F32)] * 2,
        compiler_params=_params(("parallel", "parallel", "parallel")),
        name="dilated_g%d" % group,
    )(qk3, qk3, qk3, qk3, v3, v3, v3)
    return o.reshape(batch * seq, GROUP_WIDTH), lse.reshape(batch * seq, GROUP_WIDTH)


def _merge_kernel(oa_ref, o0_ref, o1_ref, o2_ref, l0_ref, l1_ref, l2_ref, ga_ref, gb_ref, y1_ref,
                  wa_ref, wb_ref, wo_ref, g_ref, b_ref, y_ref):
    l0, l1, l2 = l0_ref[...], l1_ref[...], l2_ref[...]
    m = jnp.maximum(jnp.maximum(l0, l1), l2)
    e0, e1, e2 = jnp.exp(l0 - m), jnp.exp(l1 - m), jnp.exp(l2 - m)
    denom = e0 + e1 + e2
    out_b = (e0 / denom) * o0_ref[...] + (e1 / denom) * o1_ref[...] + (e2 / denom) * o2_ref[...]
    branch_a = jnp.dot(oa_ref[...], wa_ref[...], preferred_element_type=F32)
    branch_b = jnp.dot(out_b.astype(BF16), wb_ref[...], preferred_element_type=F32)
    merged = ga_ref[...] * branch_a + gb_ref[...] * branch_b
    mix = jnp.dot(merged.astype(BF16), wo_ref[...], preferred_element_type=F32)
    y_ref[...] = _layer_norm(ALPHA * y1_ref[...] + mix, g_ref[...], b_ref[...])


def _merge(out_a, outs, lses, gates, y1, wa, wb, wo, g, b, *, tm=256):
    rows = y1.shape[0]
    row_spec = lambda w, c=0: pl.BlockSpec((tm, w), lambda i: (i, c))
    full = lambda a: pl.BlockSpec(a.shape, lambda i: (0, 0), pipeline_mode=pl.Buffered(1))
    return pl.pallas_call(
        _merge_kernel,
        grid=(rows // tm,),
        in_specs=[row_spec(MLA_HEADS * V_DIM)] + [row_spec(GROUP_WIDTH)] * 6
                 + [row_spec(D_MODEL, 0), row_spec(D_MODEL, 1), row_spec(D_MODEL)]
                 + [full(wa), full(wb), full(wo), full(g), full(b)],
        out_specs=row_spec(D_MODEL),
        out_shape=jax.ShapeDtypeStruct((rows, D_MODEL), F32),
        compiler_params=_params(("parallel",)),
        name="merge",
    )(out_a, *outs, *lses, gates, gates, y1, wa, wb, wo, g, b)


def _rope_tables(seq):
    pos = jnp.arange(seq, dtype=F32)[:, None]

    def angles(d):
        inv_freq = ROPE_THETA ** (-jnp.arange(0, d, 2, dtype=F32) / d)
        ang = pos * inv_freq[None, :]
        return jnp.cos(ang), jnp.sin(ang)

    cos, sin = angles(DIL_HEAD_DIM)
    dil = (jnp.concatenate([cos, cos], axis=1), jnp.concatenate([-sin, sin], axis=1))
    cos, sin = angles(QK_ROPE)
    zeros = jnp.zeros_like(cos)
    pad = jnp.zeros((seq, LANES - QK_ROPE), F32)
    mla = (jnp.concatenate([cos, cos, pad], axis=1),
           jnp.concatenate([-sin, zeros, pad], axis=1),
           jnp.concatenate([zeros, sin, pad], axis=1))
    return dil, mla


def _prepare(ffn1_w_in, ffn1_w_out, ln1_g, ln1_b, w_in_mix, b_gate, q_norm_g, w_uq, kv_norm_g, w_ukv,
             w_branch_a, w_branch_b, w_out_mix, ln2_g, ln2_b, ffn2_w_in, ffn2_w_out, ln3_g, ln3_b):
    def ffn(w_in, w_out):
        pad = D_FF_PAD - D_FF
        wg = jnp.pad(w_in[:, :D_FF], ((0, 0), (0, pad))).astype(BF16)
        wu = jnp.pad(w_in[:, D_FF:], ((0, 0), (0, pad))).astype(BF16)
        wo = jnp.pad(w_out, ((0, pad), (0, 0))).astype(BF16)
        return wg, wu, wo

    row = lambda a: a.reshape(1, -1)
    o_lat = Q_LORA + KV_LORA + QK_ROPE
    o_qk = o_lat + 2 * DIL_WIDTH
    o_v = o_qk + DIL_WIDTH
    w_lat = jnp.pad(w_in_mix[:, :o_lat], ((0, 0), (0, LAT_WIDTH - o_lat))).astype(BF16)
    uq = w_uq.reshape(Q_LORA, MLA_HEADS, QK_NOPE + QK_ROPE)
    ukv = w_ukv.reshape(KV_LORA, MLA_HEADS, QK_NOPE + V_DIM)
    flat = lambda a: a.reshape(a.shape[0], -1).astype(BF16)
    return dict(
        ffn1=ffn(ffn1_w_in, ffn1_w_out), ln1=(row(ln1_g), row(ln1_b)),
        ffn2=ffn(ffn2_w_in, ffn2_w_out), ln3=(row(ln3_g), row(ln3_b)),
        w_lat=w_lat,
        w_qk=w_in_mix[:, o_lat:o_qk].astype(BF16),
        w_v=w_in_mix[:, o_qk:o_v].astype(BF16),
        w_gate=w_in_mix[:, o_v:].astype(BF16),
        b_gate=row(b_gate),
        q_norm_g=row(q_norm_g), kv_norm_g=row(kv_norm_g),
        wqn=flat(uq[:, :, :QK_NOPE]),
        wqp=flat(jnp.pad(uq[:, :, QK_NOPE:], ((0, 0), (0, 0), (0, LANES - QK_ROPE)))),
        wkn=flat(ukv[:, :, :QK_NOPE]),
        wvt=flat(ukv[:, :, QK_NOPE:]).T,
        wa=w_branch_a.astype(BF16), wb=w_branch_b.astype(BF16), wo=w_out_mix.astype(BF16),
        ln2=(row(ln2_g), row(ln2_b)),
    )


def _encoder_layer(x, p):
    batch, seq, _ = x.shape
    x2 = x.reshape(batch * seq, D_MODEL)
    dil_tables, mla_tables = _rope_tables(seq)

    y1, y1_bf = _ffn_ln(x2, *p["ffn1"], *p["ln1"], emit_bf16=True)

    lat = _proj(y1_bf, p["w_lat"], F32, tn=LAT_WIDTH // 3)
    qk_d = _proj(y1_bf, p["w_qk"], BF16, mode="rope", extras=dil_tables, seq=seq)
    v_d = _proj(y1_bf, p["w_v"], BF16)
    gates = _proj(y1_bf, p["w_gate"], F32, mode="gate", extras=(p["b_gate"],))

    q, k, vt = _latent(lat, p["q_norm_g"], p["kv_norm_g"], p["wqn"], p["wqp"], p["wkn"], p["wvt"],
                       mla_tables, seq=seq)
    out_a = _mla(q, k, vt, batch=batch, seq=seq)

    outs, lses = [], []
    for group, (window, dilation) in enumerate(DIL_PATTERNS):
        o, lse = _dilated_group(qk_d, v_d, group=group, window=window, dilation=dilation, batch=batch, seq=seq)
        outs.append(o)
        lses.append(lse)

    y2 = _merge(out_a, outs, lses, gates, y1, p["wa"], p["wb"], p["wo"], *p["ln2"])
    (y3,) = _ffn_ln(y2, *p["ffn2"], *p["ln3"], emit_bf16=False)
    return y3.reshape(batch, seq, D_MODEL)


def kernel(x_prompt, x_sample, ffn1_w_in, ffn1_w_out, ln1_g, ln1_b, w_in_mix, b_gate, q_norm_g, w_uq, kv_norm_g, w_ukv, w_branch_a, w_branch_b, w_out_mix, ln2_g, ln2_b, ffn2_w_in, ffn2_w_out, ln3_g, ln3_b):
    weights = (ffn1_w_in, ffn1_w_out, ln1_g, ln1_b, w_in_mix, b_gate, q_norm_g, w_uq, kv_norm_g, w_ukv,
               w_branch_a, w_branch_b, w_out_mix, ln2_g, ln2_b, ffn2_w_in, ffn2_w_out, ln3_g, ln3_b)
    y_prompt, y_sample = x_prompt, x_sample
    for layer in range(DEPTH):
        p = _prepare(*(w[layer] for w in weights))
        y_prompt = _encoder_layer(y_prompt, p)
        y_sample = _encoder_layer(y_sample, p)
    return (y_prompt, y_sample)
```

```python
import functools

import jax
import jax.numpy as jnp
from jax import lax
from jax.experimental import pallas as pl
from jax.experimental.pallas import tpu as pltpu

D_MODEL = 2048
DEPTH = 1
MLA_HEADS = 8
Q_LORA = 512
KV_LORA = 512
QK_NOPE = 128
QK_ROPE = 64
V_DIM = 128
DIL_PATTERNS = ((128, 1), (512, 4), (2048, 16))
HEADS_PER_GROUP = 4
DIL_HEAD_DIM = 128
DIL_WIDTH = len(DIL_PATTERNS) * HEADS_PER_GROUP * DIL_HEAD_DIM
D_FF = 5504
ROPE_THETA = 10000.0
LN_EPS = 1e-5
RMS_EPS = 1e-6
NEG_INF = -1e30
ALPHA = (2 * DEPTH) ** 0.25
MLA_SCALE = (QK_NOPE + QK_ROPE) ** -0.5
MLA_Q_SCALE = MLA_SCALE * 1.4426950408889634
DIL_SCALE = DIL_HEAD_DIM ** -0.5

LANES = 128
D_FF_PAD = 5632
FF_CHUNK = 512
LAT_WIDTH = Q_LORA + KV_LORA + LANES
MLA_SLAB = 2 * LANES
GROUP_WIDTH = HEADS_PER_GROUP * DIL_HEAD_DIM
DIL_ROWS = 2048
DIL_HEADS_PER_STEP = 2
DIL_UNITS_PER_TRIP = 4
VMEM_LIMIT = 56 * 1024 * 1024

BF16 = jnp.bfloat16
F32 = jnp.float32
_NT = (((1,), (1,)), ((), ()))


def _params(semantics):
    return pltpu.CompilerParams(dimension_semantics=semantics, vmem_limit_bytes=VMEM_LIMIT)


def _layer_norm(v, g, b):
    mu = jnp.mean(v, axis=-1, keepdims=True)
    c = v - mu
    var = jnp.mean(c * c, axis=-1, keepdims=True)
    return c * lax.rsqrt(var + LN_EPS) * g + b


def _ffn_ln_kernel(n_chunks, emit_bf16, x_ref, wg_ref, wu_ref, wo_ref, g_ref, b_ref, *rest):
    if emit_bf16:
        y_ref, ybf_ref, xbf_ref, acc_ref = rest
    else:
        y_ref, xbf_ref, acc_ref = rest
    k = pl.program_id(1)

    @pl.when(k == 0)
    def _():
        xbf_ref[...] = x_ref[...].astype(BF16)
        acc_ref[...] = jnp.zeros_like(acc_ref)

    xb = xbf_ref[...]
    gate = jnp.dot(xb, wg_ref[...], preferred_element_type=F32)
    up = jnp.dot(xb, wu_ref[...], preferred_element_type=F32)
    act = gate * jax.nn.sigmoid(gate) * up
    acc_ref[...] += jnp.dot(act.astype(BF16), wo_ref[...], preferred_element_type=F32)

    @pl.when(k == n_chunks - 1)
    def _():
        y = _layer_norm(ALPHA * x_ref[...] + 0.5 * acc_ref[...], g_ref[...], b_ref[...])
        y_ref[...] = y
        if emit_bf16:
            ybf_ref[...] = y.astype(BF16)


def _ffn_ln(x, wg, wu, wo, g, b, *, emit_bf16, tm=512):
    rows = x.shape[0]
    n_chunks = D_FF_PAD // FF_CHUNK
    out_shape = [jax.ShapeDtypeStruct((rows, D_MODEL), F32)]
    out_specs = [pl.BlockSpec((tm, D_MODEL), lambda i, k: (i, 0))]
    if emit_bf16:
        out_shape.append(jax.ShapeDtypeStruct((rows, D_MODEL), BF16))
        out_specs.append(pl.BlockSpec((tm, D_MODEL), lambda i, k: (i, 0)))
    return pl.pallas_call(
        functools.partial(_ffn_ln_kernel, n_chunks, emit_bf16),
        grid=(rows // tm, n_chunks),
        in_specs=[
            pl.BlockSpec((tm, D_MODEL), lambda i, k: (i, 0)),
            pl.BlockSpec((D_MODEL, FF_CHUNK), lambda i, k: (0, k)),
            pl.BlockSpec((D_MODEL, FF_CHUNK), lambda i, k: (0, k)),
            pl.BlockSpec((FF_CHUNK, D_MODEL), lambda i, k: (k, 0)),
            pl.BlockSpec((1, D_MODEL), lambda i, k: (0, 0)),
            pl.BlockSpec((1, D_MODEL), lambda i, k: (0, 0)),
        ],
        out_specs=out_specs,
        out_shape=out_shape,
        scratch_shapes=[pltpu.VMEM((tm, D_MODEL), BF16), pltpu.VMEM((tm, D_MODEL), F32)],
        compiler_params=_params(("parallel", "arbitrary")),
        name="ffn_ln",
    )(x, wg, wu, wo, g, b)


def _proj_kernel(mode, x_ref, w_ref, *rest):
    z = jnp.dot(x_ref[...], w_ref[...], preferred_element_type=F32)
    if mode == "dilated":
        cos_ref, sin_ref, o_ref = rest
        heads = [z[:, h * LANES:(h + 1) * LANES] for h in range(z.shape[1] // LANES)]
        n_rope_tiles = 2 * DIL_WIDTH // z.shape[1]

        @pl.when(pl.program_id(1) < n_rope_tiles)
        def _():
            cos = cos_ref[...]
            sin = sin_ref[...]
            for h, zh in enumerate(heads):
                o_ref[h] = zh * cos + pltpu.roll(zh, LANES // 2, 1) * sin

        @pl.when(pl.program_id(1) >= n_rope_tiles)
        def _():
            for h, zh in enumerate(heads):
                o_ref[h] = zh
    elif mode == "gate":
        b_ref, o_ref = rest
        o_ref[...] = jax.nn.sigmoid(z + b_ref[...]).astype(o_ref.dtype)
    else:
        (o_ref,) = rest
        o_ref[...] = z.astype(o_ref.dtype)


def _proj(x, w, out_dtype, *, mode="plain", extras=(), seq=None, tm=1024, tn=512):
    rows = x.shape[0]
    n = w.shape[1]
    tn = min(tn, n)
    in_specs = [
        pl.BlockSpec((tm, D_MODEL), lambda i, j: (i, 0)),
        pl.BlockSpec((D_MODEL, tn), lambda i, j: (0, j)),
    ]
    out_specs = pl.BlockSpec((tm, tn), lambda i, j: (i, j))
    out_shape = jax.ShapeDtypeStruct((rows, n), out_dtype)
    if mode == "dilated":
        tiles_per_seq = seq // tm
        in_specs += [pl.BlockSpec((tm, LANES), lambda i, j: (i % tiles_per_seq, 0))] * 2
        out_specs = pl.BlockSpec((tn // LANES, tm, LANES), lambda i, j: (j, i, 0))
        out_shape = jax.ShapeDtypeStruct((n // LANES, rows, LANES), out_dtype)
    elif mode == "gate":
        in_specs += [pl.BlockSpec((1, tn), lambda i, j: (0, j))]
    return pl.pallas_call(
        functools.partial(_proj_kernel, mode),
        grid=(rows // tm, n // tn),
        in_specs=in_specs,
        out_specs=out_specs,
        out_shape=out_shape,
        compiler_params=_params(("parallel", "arbitrary")),
        name="proj_" + mode,
    )(x, w, *extras)


def _latent_kernel(lat_ref, qg_ref, kvg_ref, wqn_ref, wqp_ref, wkn_ref, wvt_ref,
                   cos_ref, sin_lo_ref, sin_hi_ref, q_ref, k_ref, vt_ref):
    lat = lat_ref[...]
    c_q = lat[:, :Q_LORA]
    c_kv = lat[:, Q_LORA:Q_LORA + KV_LORA]
    k_rope = lat[:, Q_LORA + KV_LORA:]

    def rms(x, g):
        return (x * lax.rsqrt(jnp.mean(x * x, axis=-1, keepdims=True) + RMS_EPS) * g).astype(BF16)

    cos = cos_ref[...]
    sin_lo = sin_lo_ref[...]
    sin_hi = sin_hi_ref[...]

    def rope64(x):
        quarter = QK_ROPE // 2
        return (x * cos + pltpu.roll(x, LANES - quarter, 1) * sin_lo
                + pltpu.roll(x, quarter, 1) * sin_hi)

    nq = rms(c_q, qg_ref[...])
    nkv = rms(c_kv, kvg_ref[...])
    q_nope = jnp.dot(nq, wqn_ref[...], preferred_element_type=F32)
    q_pe = jnp.dot(nq, wqp_ref[...], preferred_element_type=F32)
    k_nope = jnp.dot(nkv, wkn_ref[...], preferred_element_type=F32)
    vt = lax.dot_general(wvt_ref[...], nkv, _NT, preferred_element_type=F32)
    k_pe = rope64(k_rope).astype(BF16)
    for h in range(MLA_HEADS):
        src = slice(h * LANES, (h + 1) * LANES)
        lo = slice(h * MLA_SLAB, h * MLA_SLAB + LANES)
        hi = slice(h * MLA_SLAB + LANES, (h + 1) * MLA_SLAB)
        q_ref[:, lo] = (q_nope[:, src] * MLA_Q_SCALE).astype(BF16)
        q_ref[:, hi] = (rope64(q_pe[:, src]) * MLA_Q_SCALE).astype(BF16)
        k_ref[:, lo] = k_nope[:, src].astype(BF16)
        k_ref[:, hi] = k_pe
    vt_ref[...] = vt.astype(BF16)


def _latent(lat, qg, kvg, wqn, wqp, wkn, wvt, tables, *, seq, tm=512):
    rows = lat.shape[0]
    tiles_per_seq = seq // tm
    width = MLA_HEADS * LANES
    row_spec = lambda w: pl.BlockSpec((tm, w), lambda i: (i, 0))
    full = lambda a: pl.BlockSpec(a.shape, lambda i: (0, 0), pipeline_mode=pl.Buffered(1))
    table_spec = pl.BlockSpec((tm, LANES), lambda i: (i % tiles_per_seq, 0))
    return pl.pallas_call(
        _latent_kernel,
        grid=(rows // tm,),
        in_specs=[row_spec(LAT_WIDTH), full(qg), full(kvg), full(wqn), full(wqp), full(wkn), full(wvt),
                  table_spec, table_spec, table_spec],
        out_specs=[row_spec(MLA_HEADS * MLA_SLAB), row_spec(MLA_HEADS * MLA_SLAB),
                   pl.BlockSpec((width, tm), lambda i: (0, i))],
        out_shape=[jax.ShapeDtypeStruct((rows, MLA_HEADS * MLA_SLAB), BF16),
                   jax.ShapeDtypeStruct((rows, MLA_HEADS * MLA_SLAB), BF16),
                   jax.ShapeDtypeStruct((width, rows), BF16)],
        compiler_params=_params(("parallel",)),
        name="latent",
    )(lat, qg, kvg, wqn, wqp, wkn, wvt, *tables)


def _mla_kernel(n_kv, tk, q_ref, k_ref, vt_ref, o_ref, acc_ref, s_ref):
    n_sub, _, ts = acc_ref.shape
    acc_ref[...] = jnp.zeros_like(acc_ref)

    def scores_into(slot, j):
        k = k_ref[pl.ds(pl.multiple_of(j * tk, tk), tk), :]
        for t in range(n_sub):
            s_ref[slot, t] = lax.dot_general(k, q_ref[pl.ds(t * ts, ts), :], _NT,
                                             preferred_element_type=F32)

    def consume(slot, j, carry):
        vt = vt_ref[:, pl.ds(pl.multiple_of(j * tk, tk), tk)]
        out = []
        for t in range(n_sub):
            m_prev, l_prev = carry[t]
            s = s_ref[slot, t]
            m_new = jnp.maximum(m_prev, jnp.max(s, axis=0, keepdims=True))
            alpha = jnp.exp2(m_prev - m_new)
            p = jnp.exp2(s - m_new)
            l_new = alpha * l_prev + jnp.sum(p, axis=0, keepdims=True)
            acc_ref[t] = alpha * acc_ref[t] + jnp.dot(vt, p.astype(BF16), preferred_element_type=F32)
            out.append((m_new, l_new))
        return tuple(out)

    def body(jj, carry):
        j = 2 * jj
        scores_into(1, j + 1)
        carry = consume(0, j, carry)
        scores_into(0, j + 2)
        return consume(1, j + 1, carry)

    scores_into(0, 0)
    init = tuple((jnp.full((1, ts), -jnp.inf, F32), jnp.zeros((1, ts), F32)) for _ in range(n_sub))
    carry = lax.fori_loop(0, n_kv // 2 - 1, body, init)
    scores_into(1, n_kv - 1)
    carry = consume(0, n_kv - 2, carry)
    carry = consume(1, n_kv - 1, carry)
    for t in range(n_sub):
        o_ref[pl.ds(t * ts, ts), :] = (acc_ref[t] / carry[t][1]).T.astype(o_ref.dtype)


def _mla(q, k, vt, *, batch, seq, tq=2048, ts=512, tk=512):
    n_q = seq // tq
    return pl.pallas_call(
        functools.partial(_mla_kernel, seq // tk, tk),
        grid=(batch, MLA_HEADS, n_q),
        in_specs=[
            pl.BlockSpec((tq, MLA_SLAB), lambda b, h, i: (b * n_q + i, h)),
            pl.BlockSpec((seq, MLA_SLAB), lambda b, h, i: (b, h)),
            pl.BlockSpec((V_DIM, seq), lambda b, h, i: (h, b)),
        ],
        out_specs=pl.BlockSpec((tq, V_DIM), lambda b, h, i: (b * n_q + i, h)),
        out_shape=jax.ShapeDtypeStruct((batch * seq, MLA_HEADS * V_DIM), BF16),
        scratch_shapes=[pltpu.VMEM((tq // ts, V_DIM, ts), F32), pltpu.VMEM((2, tq // ts, tk, ts), F32)],
        compiler_params=_params(("parallel", "parallel", "arbitrary")),
        name="mla",
    )(q, k, vt)


def _dilated_kernel(d, half, stream_len, q_ref, kp_ref, kc_ref, kn_ref, vp_ref, vc_ref, vn_ref,
                    o_ref, lse_ref, qs_ref, ks_ref, vs_ref):
    n_heads = q_ref.shape[0]
    tl = DIL_ROWS // d
    i = pl.program_id(2)
    for h in range(n_heads):
        for r in range(d):
            qs_ref[h, r] = q_ref[h, pl.ds(r, tl, stride=d), :].astype(BF16)
            for dst, before, cur, after in ((ks_ref, kp_ref, kc_ref, kn_ref), (vs_ref, vp_ref, vc_ref, vn_ref)):
                dst[h, r, 0:half] = before[h, pl.ds(r, half, stride=d), :].astype(BF16)
                dst[h, r, half:half + tl] = cur[h, pl.ds(r, tl, stride=d), :].astype(BF16)
                dst[h, r, half + tl:] = after[h, pl.ds(r, half, stride=d), :].astype(BF16)

    chunk = 2 * half
    row = lax.broadcasted_iota(jnp.int32, (chunk, 2 * chunk), 0)
    col = lax.broadcasted_iota(jnp.int32, (chunk, 2 * chunk), 1)
    band = (col >= row) & (col <= row + 2 * half)
    shift = d.bit_length() - 1

    def unit_group(ug, carry):
        jobs = []
        for uu in range(DIL_UNITS_PER_TRIP):
            u = ug * DIL_UNITS_PER_TRIP + uu
            r = u & (d - 1)
            c = u >> shift
            row0 = pl.multiple_of(c * chunk, chunk)
            k_idx = i * tl + c * chunk - half + col
            mask = band & (k_idx >= 0) & (k_idx < stream_len)
            for h in range(n_heads):
                q = qs_ref[h, r, pl.ds(row0, chunk), :]
                k = ks_ref[h, r, pl.ds(row0, 2 * chunk), :]
                s = lax.dot_general(q, k, _NT, preferred_element_type=F32) * DIL_SCALE
                jobs.append((h, r, row0, mask, s))
        for h, r, row0, mask, s in jobs:
            s = jnp.where(mask, s, NEG_INF)
            m = jnp.max(s, axis=1, keepdims=True)
            p = jnp.exp(s - m)
            denom = jnp.sum(p, axis=1, keepdims=True)
            pn = (p / denom).astype(BF16)
            o = jnp.dot(pn, vs_ref[h, r, pl.ds(row0, 2 * chunk), :], preferred_element_type=F32)
            start = r + d * row0
            o_ref[h, pl.ds(start, chunk, stride=d), :] = o
            lse_ref[h, pl.ds(start, chunk, stride=d), :] = jnp.broadcast_to(m + jnp.log(denom), (chunk, LANES))
        return carry

    lax.fori_loop(0, DIL_ROWS // chunk // DIL_UNITS_PER_TRIP, unit_group, 0)


def _dilated_group(qkv, *, group, window, dilation, batch, seq):
    d = dilation
    half = window // (2 * d)
    halo = half * d
    n_r = seq // DIL_ROWS
    halo_per_step = DIL_ROWS // halo
    n_halo = batch * seq // halo
    tl = DIL_ROWS // d
    steps_per_group = HEADS_PER_GROUP // DIL_HEADS_PER_STEP
    slabs_per_kind = DIL_WIDTH // DIL_HEAD_DIM // DIL_HEADS_PER_STEP
    blk = (DIL_HEADS_PER_STEP, DIL_ROWS, LANES)
    hblk = (DIL_HEADS_PER_STEP, halo, LANES)

    def specs(kind):
        slab = lambda g: kind * slabs_per_kind + group * steps_per_group + g
        cur = pl.BlockSpec(blk, lambda g, b, i: (slab(g), b * n_r + i, 0))
        before = pl.BlockSpec(hblk, lambda g, b, i: (slab(g), jnp.maximum((b * n_r + i) * halo_per_step - 1, 0), 0))
        after = pl.BlockSpec(hblk, lambda g, b, i: (slab(g), jnp.minimum((b * n_r + i + 1) * halo_per_step,
                                                                          n_halo - 1), 0))
        return before, cur, after

    out_spec = pl.BlockSpec(blk, lambda g, b, i: (g, b * n_r + i, 0))
    out_shape = jax.ShapeDtypeStruct((HEADS_PER_GROUP, batch * seq, LANES), F32)
    return pl.pallas_call(
        functools.partial(_dilated_kernel, d, half, seq // d),
        grid=(steps_per_group, batch, n_r),
        in_specs=[specs(0)[1], *specs(1), *specs(2)],
        out_specs=[out_spec, out_spec],
        out_shape=[out_shape, out_shape],
        scratch_shapes=[pltpu.VMEM((DIL_HEADS_PER_STEP, d, tl, LANES), BF16),
                        pltpu.VMEM((DIL_HEADS_PER_STEP, d, tl + 2 * half, LANES), BF16),
                        pltpu.VMEM((DIL_HEADS_PER_STEP, d, tl + 2 * half, LANES), BF16)],
        compiler_params=_params(("parallel", "parallel", "parallel")),
        name="dilated_g%d" % group,
    )(*([qkv] * 7))


def _merge_kernel(oa_ref, o0_ref, o1_ref, o2_ref, l0_ref, l1_ref, l2_ref, ga_ref, gb_ref, y1_ref,
                  wa_ref, wb_ref, wo_ref, g_ref, b_ref, y_ref):
    heads = []
    for h in range(HEADS_PER_GROUP):
        l0, l1, l2 = l0_ref[h], l1_ref[h], l2_ref[h]
        m = jnp.maximum(jnp.maximum(l0, l1), l2)
        e0, e1, e2 = jnp.exp(l0 - m), jnp.exp(l1 - m), jnp.exp(l2 - m)
        denom = e0 + e1 + e2
        heads.append(((e0 / denom) * o0_ref[h] + (e1 / denom) * o1_ref[h] + (e2 / denom) * o2_ref[h]).astype(BF16))
    out_b = jnp.concatenate(heads, axis=1)
    branch_a = jnp.dot(oa_ref[...], wa_ref[...], preferred_element_type=F32)
    branch_b = jnp.dot(out_b, wb_ref[...], preferred_element_type=F32)
    merged = ga_ref[...] * branch_a + gb_ref[...] * branch_b
    mix = jnp.dot(merged.astype(BF16), wo_ref[...], preferred_element_type=F32)
    y_ref[...] = _layer_norm(ALPHA * y1_ref[...] + mix, g_ref[...], b_ref[...])


def _merge(out_a, outs, lses, gates, y1, wa, wb, wo, g, b, *, tm=256):
    rows = y1.shape[0]
    row_spec = lambda w, c=0: pl.BlockSpec((tm, w), lambda i: (i, c))
    full = lambda a: pl.BlockSpec(a.shape, lambda i: (0, 0), pipeline_mode=pl.Buffered(1))
    return pl.pallas_call(
        _merge_kernel,
        grid=(rows // tm,),
        in_specs=[row_spec(MLA_HEADS * V_DIM)]
                 + [pl.BlockSpec((HEADS_PER_GROUP, tm, LANES), lambda i: (0, i, 0))] * 6
                 + [row_spec(D_MODEL, 0), row_spec(D_MODEL, 1), row_spec(D_MODEL)]
                 + [full(wa), full(wb), full(wo), full(g), full(b)],
        out_specs=row_spec(D_MODEL),
        out_shape=jax.ShapeDtypeStruct((rows, D_MODEL), F32),
        compiler_params=_params(("parallel",)),
        name="merge",
    )(out_a, *outs, *lses, gates, gates, y1, wa, wb, wo, g, b)


def _rope_tables(seq):
    pos = jnp.arange(seq, dtype=F32)[:, None]

    def angles(d):
        inv_freq = ROPE_THETA ** (-jnp.arange(0, d, 2, dtype=F32) / d)
        ang = pos * inv_freq[None, :]
        return jnp.cos(ang), jnp.sin(ang)

    cos, sin = angles(DIL_HEAD_DIM)
    dil = (jnp.concatenate([cos, cos], axis=1), jnp.concatenate([-sin, sin], axis=1))
    cos, sin = angles(QK_ROPE)
    zeros = jnp.zeros_like(cos)
    pad = jnp.zeros((seq, LANES - QK_ROPE), F32)
    mla = (jnp.concatenate([cos, cos, pad], axis=1),
           jnp.concatenate([-sin, zeros, pad], axis=1),
           jnp.concatenate([zeros, sin, pad], axis=1))
    return dil, mla


def _prepare(ffn1_w_in, ffn1_w_out, ln1_g, ln1_b, w_in_mix, b_gate, q_norm_g, w_uq, kv_norm_g, w_ukv,
             w_branch_a, w_branch_b, w_out_mix, ln2_g, ln2_b, ffn2_w_in, ffn2_w_out, ln3_g, ln3_b):
    def ffn(w_in, w_out):
        pad = D_FF_PAD - D_FF
        wg = jnp.pad(w_in[:, :D_FF], ((0, 0), (0, pad))).astype(BF16)
        wu = jnp.pad(w_in[:, D_FF:], ((0, 0), (0, pad))).astype(BF16)
        wo = jnp.pad(w_out, ((0, pad), (0, 0))).astype(BF16)
        return wg, wu, wo

    row = lambda a: a.reshape(1, -1)
    o_lat = Q_LORA + KV_LORA + QK_ROPE
    o_qk = o_lat + 2 * DIL_WIDTH
    o_v = o_qk + DIL_WIDTH
    w_lat = jnp.pad(w_in_mix[:, :o_lat], ((0, 0), (0, LAT_WIDTH - o_lat))).astype(BF16)
    uq = w_uq.reshape(Q_LORA, MLA_HEADS, QK_NOPE + QK_ROPE)
    ukv = w_ukv.reshape(KV_LORA, MLA_HEADS, QK_NOPE + V_DIM)
    flat = lambda a: a.reshape(a.shape[0], -1).astype(BF16)
    return dict(
        ffn1=ffn(ffn1_w_in, ffn1_w_out), ln1=(row(ln1_g), row(ln1_b)),
        ffn2=ffn(ffn2_w_in, ffn2_w_out), ln3=(row(ln3_g), row(ln3_b)),
        w_lat=w_lat,
        w_dil=w_in_mix[:, o_lat:o_v].astype(BF16),
        w_gate=w_in_mix[:, o_v:].astype(BF16),
        b_gate=row(b_gate),
        q_norm_g=row(q_norm_g), kv_norm_g=row(kv_norm_g),
        wqn=flat(uq[:, :, :QK_NOPE]),
        wqp=flat(jnp.pad(uq[:, :, QK_NOPE:], ((0, 0), (0, 0), (0, LANES - QK_ROPE)))),
        wkn=flat(ukv[:, :, :QK_NOPE]),
        wvt=flat(ukv[:, :, QK_NOPE:]).T,
        wa=w_branch_a.astype(BF16), wb=w_branch_b.astype(BF16), wo=w_out_mix.astype(BF16),
        ln2=(row(ln2_g), row(ln2_b)),
    )


def _encoder_layer(x, p):
    batch, seq, _ = x.shape
    x2 = x.reshape(batch * seq, D_MODEL)
    dil_tables, mla_tables = _rope_tables(seq)

    y1, y1_bf = _ffn_ln(x2, *p["ffn1"], *p["ln1"], emit_bf16=True)

    lat = _proj(y1_bf, p["w_lat"], F32, tn=LAT_WIDTH // 3)
    qkv_d = _proj(y1_bf, p["w_dil"], F32, mode="dilated", extras=dil_tables, seq=seq)
    gates = _proj(y1_bf, p["w_gate"], F32, mode="gate", extras=(p["b_gate"],))

    q, k, vt = _latent(lat, p["q_norm_g"], p["kv_norm_g"], p["wqn"], p["wqp"], p["wkn"], p["wvt"],
                       mla_tables, seq=seq)
    out_a = _mla(q, k, vt, batch=batch, seq=seq)

    outs, lses = [], []
    for group, (window, dilation) in enumerate(DIL_PATTERNS):
        o, lse = _dilated_group(qkv_d, group=group, window=window, dilation=dilation, batch=batch, seq=seq)
        outs.append(o)
        lses.append(lse)

    y2 = _merge(out_a, outs, lses, gates, y1, p["wa"], p["wb"], p["wo"], *p["ln2"])
    (y3,) = _ffn_ln(y2, *p["ffn2"], *p["ln3"], emit_bf16=False)
    return y3.reshape(batch, seq, D_MODEL)


def kernel(x_prompt, x_sample, ffn1_w_in, ffn1_w_out, ln1_g, ln1_b, w_in_mix, b_gate, q_norm_g, w_uq, kv_norm_g, w_ukv, w_branch_a, w_branch_b, w_out_mix, ln2_g, ln2_b, ffn2_w_in, ffn2_w_out, ln3_g, ln3_b):
    weights = (ffn1_w_in, ffn1_w_out, ln1_g, ln1_b, w_in_mix, b_gate, q_norm_g, w_uq, kv_norm_g, w_ukv,
               w_branch_a, w_branch_b, w_out_mix, ln2_g, ln2_b, ffn2_w_in, ffn2_w_out, ln3_g, ln3_b)
    y_prompt, y_sample = x_prompt, x_sample
    for layer in range(DEPTH):
        p = _prepare(*(w[layer] for w in weights))
        y_prompt = _encoder_layer(y_prompt, p)
        y_sample = _encoder_layer(y_sample, p)
    return (y_prompt, y_sample)
```

```python
import functools

import jax
import jax.numpy as jnp
from jax import lax
from jax.experimental import pallas as pl
from jax.experimental.pallas import tpu as pltpu

D_MODEL = 2048
DEPTH = 1
MLA_HEADS = 8
Q_LORA = 512
KV_LORA = 512
QK_NOPE = 128
QK_ROPE = 64
V_DIM = 128
DIL_PATTERNS = ((128, 1), (512, 4), (2048, 16))
HEADS_PER_GROUP = 4
DIL_HEAD_DIM = 128
DIL_WIDTH = len(DIL_PATTERNS) * HEADS_PER_GROUP * DIL_HEAD_DIM
D_FF = 5504
ROPE_THETA = 10000.0
LN_EPS = 1e-5
RMS_EPS = 1e-6
NEG_INF = -1e30
ALPHA = (2 * DEPTH) ** 0.25
MLA_SCALE = (QK_NOPE + QK_ROPE) ** -0.5
MLA_Q_SCALE = MLA_SCALE * 1.4426950408889634
DIL_SCALE = DIL_HEAD_DIM ** -0.5

LANES = 128
D_FF_PAD = 5632
FF_CHUNK = 512
LAT_WIDTH = Q_LORA + KV_LORA + LANES
MLA_SLAB = 2 * LANES
GROUP_WIDTH = HEADS_PER_GROUP * DIL_HEAD_DIM
PROJ_SUB = 512
DIL_ROWS = 2048
DIL_HEADS_PER_STEP = 2
DIL_UNITS_PER_TRIP = 4
VMEM_LIMIT = 56 * 1024 * 1024

BF16 = jnp.bfloat16
F32 = jnp.float32
_NT = (((1,), (1,)), ((), ()))


def _params(semantics):
    return pltpu.CompilerParams(dimension_semantics=semantics, vmem_limit_bytes=VMEM_LIMIT)


def _layer_norm(v, g, b):
    mu = jnp.mean(v, axis=-1, keepdims=True)
    c = v - mu
    var = jnp.mean(c * c, axis=-1, keepdims=True)
    return c * lax.rsqrt(var + LN_EPS) * g + b


def _ffn_ln_kernel(n_chunks, emit_bf16, x_ref, wg_ref, wu_ref, wo_ref, g_ref, b_ref, *rest):
    if emit_bf16:
        y_ref, ybf_ref, xbf_ref, acc_ref = rest
    else:
        y_ref, xbf_ref, acc_ref = rest
    k = pl.program_id(1)

    @pl.when(k == 0)
    def _():
        xbf_ref[...] = x_ref[...].astype(BF16)
        acc_ref[...] = jnp.zeros_like(acc_ref)

    xb = xbf_ref[...]
    gate = jnp.dot(xb, wg_ref[...], preferred_element_type=F32)
    up = jnp.dot(xb, wu_ref[...], preferred_element_type=F32)
    act = gate * jax.nn.sigmoid(gate) * up
    acc_ref[...] += jnp.dot(act.astype(BF16), wo_ref[...], preferred_element_type=F32)

    @pl.when(k == n_chunks - 1)
    def _():
        y = _layer_norm(ALPHA * x_ref[...] + 0.5 * acc_ref[...], g_ref[...], b_ref[...])
        y_ref[...] = y
        if emit_bf16:
            ybf_ref[...] = y.astype(BF16)


def _ffn_ln(x, wg, wu, wo, g, b, *, emit_bf16, tm=512):
    rows = x.shape[0]
    n_chunks = D_FF_PAD // FF_CHUNK
    out_shape = [jax.ShapeDtypeStruct((rows, D_MODEL), F32)]
    out_specs = [pl.BlockSpec((tm, D_MODEL), lambda i, k: (i, 0))]
    if emit_bf16:
        out_shape.append(jax.ShapeDtypeStruct((rows, D_MODEL), BF16))
        out_specs.append(pl.BlockSpec((tm, D_MODEL), lambda i, k: (i, 0)))
    return pl.pallas_call(
        functools.partial(_ffn_ln_kernel, n_chunks, emit_bf16),
        grid=(rows // tm, n_chunks),
        in_specs=[
            pl.BlockSpec((tm, D_MODEL), lambda i, k: (i, 0)),
            pl.BlockSpec((None, D_MODEL, FF_CHUNK), lambda i, k: (k, 0, 0)),
            pl.BlockSpec((None, D_MODEL, FF_CHUNK), lambda i, k: (k, 0, 0)),
            pl.BlockSpec((FF_CHUNK, D_MODEL), lambda i, k: (k, 0)),
            pl.BlockSpec((1, D_MODEL), lambda i, k: (0, 0)),
            pl.BlockSpec((1, D_MODEL), lambda i, k: (0, 0)),
        ],
        out_specs=out_specs,
        out_shape=out_shape,
        scratch_shapes=[pltpu.VMEM((tm, D_MODEL), BF16), pltpu.VMEM((tm, D_MODEL), F32)],
        compiler_params=_params(("parallel", "arbitrary")),
        name="ffn_ln",
    )(x, wg, wu, wo, g, b)


def _proj_kernel(mode, x_ref, w_ref, *rest):
    tn = w_ref.shape[1]
    starts = range(0, tn, PROJ_SUB)

    def products():
        x = x_ref[...]
        return [(c, jnp.dot(x, w_ref[:, c:min(c + PROJ_SUB, tn)], preferred_element_type=F32)) for c in starts]

    if mode == "dilated":
        cos_ref, sin_ref, o_ref = rest
        n_rope_steps = 2 * DIL_WIDTH // tn

        @pl.when(pl.program_id(1) < n_rope_steps)
        def _():
            cos = cos_ref[...]
            sin = sin_ref[...]
            for c, z in products():
                for h in range(z.shape[1] // LANES):
                    zh = z[:, h * LANES:(h + 1) * LANES]
                    o_ref[c // LANES + h] = zh * cos + pltpu.roll(zh, LANES // 2, 1) * sin

        @pl.when(pl.program_id(1) >= n_rope_steps)
        def _():
            for c, z in products():
                for h in range(z.shape[1] // LANES):
                    o_ref[c // LANES + h] = z[:, h * LANES:(h + 1) * LANES]
    elif mode == "gate":
        b_ref, o_ref = rest
        for c, z in products():
            sl = slice(c, c + z.shape[1])
            o_ref[:, sl] = jax.nn.sigmoid(z + b_ref[:, sl]).astype(o_ref.dtype)
    else:
        (o_ref,) = rest
        for c, z in products():
            o_ref[:, c:c + z.shape[1]] = z.astype(o_ref.dtype)


def _proj(x, w, out_dtype, *, mode="plain", extras=(), seq=None, tm=1024, tn=512):
    rows = x.shape[0]
    n = w.shape[1]
    tn = min(tn, n)
    in_specs = [
        pl.BlockSpec((tm, D_MODEL), lambda i, j: (i, 0)),
        pl.BlockSpec((D_MODEL, tn), lambda i, j: (0, j)),
    ]
    out_specs = pl.BlockSpec((tm, tn), lambda i, j: (i, j))
    out_shape = jax.ShapeDtypeStruct((rows, n), out_dtype)
    if mode == "dilated":
        tiles_per_seq = seq // tm
        in_specs += [pl.BlockSpec((tm, LANES), lambda i, j: (i % tiles_per_seq, 0))] * 2
        out_specs = pl.BlockSpec((tn // LANES, tm, LANES), lambda i, j: (j, i, 0))
        out_shape = jax.ShapeDtypeStruct((n // LANES, rows, LANES), out_dtype)
    elif mode == "gate":
        in_specs += [pl.BlockSpec((1, tn), lambda i, j: (0, j))]
    return pl.pallas_call(
        functools.partial(_proj_kernel, mode),
        grid=(rows // tm, n // tn),
        in_specs=in_specs,
        out_specs=out_specs,
        out_shape=out_shape,
        compiler_params=_params(("parallel", "arbitrary")),
        name="proj_" + mode,
    )(x, w, *extras)


def _latent_kernel(lat_ref, qg_ref, kvg_ref, wqn_ref, wqp_ref, wkn_ref, wvt_ref,
                   cos_ref, sin_lo_ref, sin_hi_ref, q_ref, k_ref, vt_ref):
    lat = lat_ref[...]
    c_q = lat[:, :Q_LORA]
    c_kv = lat[:, Q_LORA:Q_LORA + KV_LORA]
    k_rope = lat[:, Q_LORA + KV_LORA:]

    def rms(x, g):
        return (x * lax.rsqrt(jnp.mean(x * x, axis=-1, keepdims=True) + RMS_EPS) * g).astype(BF16)

    cos = cos_ref[...]
    sin_lo = sin_lo_ref[...]
    sin_hi = sin_hi_ref[...]

    def rope64(x):
        quarter = QK_ROPE // 2
        return (x * cos + pltpu.roll(x, LANES - quarter, 1) * sin_lo
                + pltpu.roll(x, quarter, 1) * sin_hi)

    nq = rms(c_q, qg_ref[...])
    nkv = rms(c_kv, kvg_ref[...])
    q_nope = jnp.dot(nq, wqn_ref[...], preferred_element_type=F32)
    q_pe = jnp.dot(nq, wqp_ref[...], preferred_element_type=F32)
    k_nope = jnp.dot(nkv, wkn_ref[...], preferred_element_type=F32)
    vt = lax.dot_general(wvt_ref[...], nkv, _NT, preferred_element_type=F32)
    k_pe = rope64(k_rope).astype(BF16)
    for h in range(MLA_HEADS):
        src = slice(h * LANES, (h + 1) * LANES)
        lo = slice(h * MLA_SLAB, h * MLA_SLAB + LANES)
        hi = slice(h * MLA_SLAB + LANES, (h + 1) * MLA_SLAB)
        q_ref[:, lo] = (q_nope[:, src] * MLA_Q_SCALE).astype(BF16)
        q_ref[:, hi] = (rope64(q_pe[:, src]) * MLA_Q_SCALE).astype(BF16)
        k_ref[:, lo] = k_nope[:, src].astype(BF16)
        k_ref[:, hi] = k_pe
    vt_ref[...] = vt.astype(BF16)


def _latent(lat, qg, kvg, wqn, wqp, wkn, wvt, tables, *, seq, tm=512):
    rows = lat.shape[0]
    tiles_per_seq = seq // tm
    width = MLA_HEADS * LANES
    row_spec = lambda w: pl.BlockSpec((tm, w), lambda i: (i, 0))
    full = lambda a: pl.BlockSpec(a.shape, lambda i: (0, 0), pipeline_mode=pl.Buffered(1))
    table_spec = pl.BlockSpec((tm, LANES), lambda i: (i % tiles_per_seq, 0))
    return pl.pallas_call(
        _latent_kernel,
        grid=(rows // tm,),
        in_specs=[row_spec(LAT_WIDTH), full(qg), full(kvg), full(wqn), full(wqp), full(wkn), full(wvt),
                  table_spec, table_spec, table_spec],
        out_specs=[row_spec(MLA_HEADS * MLA_SLAB), row_spec(MLA_HEADS * MLA_SLAB),
                   pl.BlockSpec((width, tm), lambda i: (0, i))],
        out_shape=[jax.ShapeDtypeStruct((rows, MLA_HEADS * MLA_SLAB), BF16),
                   jax.ShapeDtypeStruct((rows, MLA_HEADS * MLA_SLAB), BF16),
                   jax.ShapeDtypeStruct((width, rows), BF16)],
        compiler_params=_params(("parallel",)),
        name="latent",
    )(lat, qg, kvg, wqn, wqp, wkn, wvt, *tables)


def _mla_kernel(n_kv, tk, q_ref, k_ref, vt_ref, o_ref, acc_ref, s_ref, mblk_ref):
    n_sub, _, ts = acc_ref.shape
    acc_ref[...] = jnp.zeros_like(acc_ref)

    def scores_into(slot, j):
        k = k_ref[pl.ds(pl.multiple_of(j * tk, tk), tk), :]
        for t in range(n_sub):
            s = lax.dot_general(k, q_ref[pl.ds(t * ts, ts), :], _NT, preferred_element_type=F32)
            s_ref[slot, t] = s
            mblk_ref[slot, t] = jnp.max(s, axis=0, keepdims=True)

    def consume(slot, j, carry):
        vt = vt_ref[:, pl.ds(pl.multiple_of(j * tk, tk), tk)]
        out = []
        for t in range(n_sub):
            m_prev, l_prev = carry[t]
            m_new = jnp.maximum(m_prev, mblk_ref[slot, t])
            alpha = jnp.exp2(m_prev - m_new)
            p = jnp.exp2(s_ref[slot, t] - m_new)
            l_new = alpha * l_prev + jnp.sum(p, axis=0, keepdims=True)
            acc_ref[t] = alpha * acc_ref[t] + jnp.dot(vt, p.astype(BF16), preferred_element_type=F32)
            out.append((m_new, l_new))
        return tuple(out)

    def body(jj, carry):
        j = 2 * jj
        scores_into(1, j + 1)
        carry = consume(0, j, carry)
        scores_into(0, j + 2)
        return consume(1, j + 1, carry)

    scores_into(0, 0)
    init = tuple((jnp.full((1, ts), -jnp.inf, F32), jnp.zeros((1, ts), F32)) for _ in range(n_sub))
    carry = lax.fori_loop(0, n_kv // 2 - 1, body, init)
    scores_into(1, n_kv - 1)
    carry = consume(0, n_kv - 2, carry)
    carry = consume(1, n_kv - 1, carry)
    for t in range(n_sub):
        o_ref[pl.ds(t * ts, ts), :] = (acc_ref[t] / carry[t][1]).T.astype(o_ref.dtype)


def _mla(q, k, vt, *, batch, seq, tq=2048, ts=512, tk=512):
    n_q = seq // tq
    return pl.pallas_call(
        functools.partial(_mla_kernel, seq // tk, tk),
        grid=(batch, MLA_HEADS, n_q),
        in_specs=[
            pl.BlockSpec((tq, MLA_SLAB), lambda b, h, i: (b * n_q + i, h)),
            pl.BlockSpec((seq, MLA_SLAB), lambda b, h, i: (b, h)),
            pl.BlockSpec((V_DIM, seq), lambda b, h, i: (h, b)),
        ],
        out_specs=pl.BlockSpec((tq, V_DIM), lambda b, h, i: (b * n_q + i, h)),
        out_shape=jax.ShapeDtypeStruct((batch * seq, MLA_HEADS * V_DIM), BF16),
        scratch_shapes=[pltpu.VMEM((tq // ts, V_DIM, ts), F32), pltpu.VMEM((2, tq // ts, tk, ts), F32),
                        pltpu.VMEM((2, tq // ts, 1, ts), F32)],
        compiler_params=_params(("parallel", "parallel", "arbitrary")),
        name="mla",
    )(q, k, vt)


def _dilated_kernel(d, half, stream_len, q_ref, kp_ref, kc_ref, kn_ref, vp_ref, vc_ref, vn_ref,
                    o_ref, lse_ref, qs_ref, ks_ref, vs_ref):
    n_heads = q_ref.shape[0]
    tl = DIL_ROWS // d
    i = pl.program_id(2)
    for h in range(n_heads):
        for r in range(d):
            qs_ref[h, r] = q_ref[h, pl.ds(r, tl, stride=d), :].astype(BF16)
            for dst, before, cur, after in ((ks_ref, kp_ref, kc_ref, kn_ref), (vs_ref, vp_ref, vc_ref, vn_ref)):
                dst[h, r, 0:half] = before[h, pl.ds(r, half, stride=d), :].astype(BF16)
                dst[h, r, half:half + tl] = cur[h, pl.ds(r, tl, stride=d), :].astype(BF16)
                dst[h, r, half + tl:] = after[h, pl.ds(r, half, stride=d), :].astype(BF16)

    chunk = 2 * half
    row = lax.broadcasted_iota(jnp.int32, (chunk, 2 * chunk), 0)
    col = lax.broadcasted_iota(jnp.int32, (chunk, 2 * chunk), 1)
    band = (col >= row) & (col <= row + 2 * half)
    shift = d.bit_length() - 1

    def unit_group(ug, carry):
        jobs = []
        for uu in range(DIL_UNITS_PER_TRIP):
            u = ug * DIL_UNITS_PER_TRIP + uu
            r = u & (d - 1)
            c = u >> shift
            row0 = pl.multiple_of(c * chunk, chunk)
            k_idx = i * tl + c * chunk - half + col
            mask = band & (k_idx >= 0) & (k_idx < stream_len)
            for h in range(n_heads):
                q = qs_ref[h, r, pl.ds(row0, chunk), :]
                k = ks_ref[h, r, pl.ds(row0, 2 * chunk), :]
                s = lax.dot_general(q, k, _NT, preferred_element_type=F32) * DIL_SCALE
                jobs.append((h, r, row0, mask, s))
        for h, r, row0, mask, s in jobs:
            s = jnp.where(mask, s, NEG_INF)
            m = jnp.max(s, axis=1, keepdims=True)
            p = jnp.exp(s - m)
            denom = jnp.sum(p, axis=1, keepdims=True)
            pn = (p / denom).astype(BF16)
            o = jnp.dot(pn, vs_ref[h, r, pl.ds(row0, 2 * chunk), :], preferred_element_type=F32)
            start = r + d * row0
            o_ref[h, pl.ds(start, chunk, stride=d), :] = o
            lse_ref[h, pl.ds(start, chunk, stride=d), :] = jnp.broadcast_to(m + jnp.log(denom), (chunk, LANES))
        return carry

    lax.fori_loop(0, DIL_ROWS // chunk // DIL_UNITS_PER_TRIP, unit_group, 0)


def _dilated_group(qkv, *, group, window, dilation, batch, seq):
    d = dilation
    half = window // (2 * d)
    halo = half * d
    n_r = seq // DIL_ROWS
    halo_per_step = DIL_ROWS // halo
    n_halo = batch * seq // halo
    tl = DIL_ROWS // d
    steps_per_group = HEADS_PER_GROUP // DIL_HEADS_PER_STEP
    slabs_per_kind = DIL_WIDTH // DIL_HEAD_DIM // DIL_HEADS_PER_STEP
    blk = (DIL_HEADS_PER_STEP, DIL_ROWS, LANES)
    hblk = (DIL_HEADS_PER_STEP, halo, LANES)

    def specs(kind):
        slab = lambda g: kind * slabs_per_kind + group * steps_per_group + g
        cur = pl.BlockSpec(blk, lambda g, b, i: (slab(g), b * n_r + i, 0))
        before = pl.BlockSpec(hblk, lambda g, b, i: (slab(g), jnp.maximum((b * n_r + i) * halo_per_step - 1, 0), 0))
        after = pl.BlockSpec(hblk, lambda g, b, i: (slab(g), jnp.minimum((b * n_r + i + 1) * halo_per_step,
                                                                          n_halo - 1), 0))
        return before, cur, after

    out_spec = pl.BlockSpec(blk, lambda g, b, i: (g, b * n_r + i, 0))
    out_shape = jax.ShapeDtypeStruct((HEADS_PER_GROUP, batch * seq, LANES), F32)
    return pl.pallas_call(
        functools.partial(_dilated_kernel, d, half, seq // d),
        grid=(steps_per_group, batch, n_r),
        in_specs=[specs(0)[1], *specs(1), *specs(2)],
        out_specs=[out_spec, out_spec],
        out_shape=[out_shape, out_shape],
        scratch_shapes=[pltpu.VMEM((DIL_HEADS_PER_STEP, d, tl, LANES), BF16),
                        pltpu.VMEM((DIL_HEADS_PER_STEP, d, tl + 2 * half, LANES), BF16),
                        pltpu.VMEM((DIL_HEADS_PER_STEP, d, tl + 2 * half, LANES), BF16)],
        compiler_params=_params(("parallel", "parallel", "parallel")),
        name="dilated_g%d" % group,
    )(*([qkv] * 7))


def _merge_kernel(oa_ref, o0_ref, o1_ref, o2_ref, l0_ref, l1_ref, l2_ref, ga_ref, gb_ref, y1_ref,
                  wa_ref, wb_ref, wo_ref, g_ref, b_ref, y_ref):
    heads = []
    for h in range(HEADS_PER_GROUP):
        l0, l1, l2 = l0_ref[h], l1_ref[h], l2_ref[h]
        m = jnp.maximum(jnp.maximum(l0, l1), l2)
        e0, e1, e2 = jnp.exp(l0 - m), jnp.exp(l1 - m), jnp.exp(l2 - m)
        denom = e0 + e1 + e2
        heads.append(((e0 / denom) * o0_ref[h] + (e1 / denom) * o1_ref[h] + (e2 / denom) * o2_ref[h]).astype(BF16))
    out_b = jnp.concatenate(heads, axis=1)
    branch_a = jnp.dot(oa_ref[...], wa_ref[...], preferred_element_type=F32)
    branch_b = jnp.dot(out_b, wb_ref[...], preferred_element_type=F32)
    merged = ga_ref[...] * branch_a + gb_ref[...] * branch_b
    mix = jnp.dot(merged.astype(BF16), wo_ref[...], preferred_element_type=F32)
    y_ref[...] = _layer_norm(ALPHA * y1_ref[...] + mix, g_ref[...], b_ref[...])


def _merge(out_a, outs, lses, gates, y1, wa, wb, wo, g, b, *, tm=256):
    rows = y1.shape[0]
    row_spec = lambda w, c=0: pl.BlockSpec((tm, w), lambda i: (i, c))
    full = lambda a: pl.BlockSpec(a.shape, lambda i: (0, 0), pipeline_mode=pl.Buffered(1))
    return pl.pallas_call(
        _merge_kernel,
        grid=(rows // tm,),
        in_specs=[row_spec(MLA_HEADS * V_DIM)]
                 + [pl.BlockSpec((HEADS_PER_GROUP, tm, LANES), lambda i: (0, i, 0))] * 6
                 + [row_spec(D_MODEL, 0), row_spec(D_MODEL, 1), row_spec(D_MODEL)]
                 + [full(wa), full(wb), full(wo), full(g), full(b)],
        out_specs=row_spec(D_MODEL),
        out_shape=jax.ShapeDtypeStruct((rows, D_MODEL), F32),
        compiler_params=_params(("parallel",)),
        name="merge",
    )(out_a, *outs, *lses, gates, gates, y1, wa, wb, wo, g, b)


def _rope_tables(seq):
    pos = jnp.arange(seq, dtype=F32)[:, None]

    def angles(d):
        inv_freq = ROPE_THETA ** (-jnp.arange(0, d, 2, dtype=F32) / d)
        ang = pos * inv_freq[None, :]
        return jnp.cos(ang), jnp.sin(ang)

    cos, sin = angles(DIL_HEAD_DIM)
    dil = (jnp.concatenate([cos, cos], axis=1), jnp.concatenate([-sin, sin], axis=1))
    cos, sin = angles(QK_ROPE)
    zeros = jnp.zeros_like(cos)
    pad = jnp.zeros((seq, LANES - QK_ROPE), F32)
    mla = (jnp.concatenate([cos, cos, pad], axis=1),
           jnp.concatenate([-sin, zeros, pad], axis=1),
           jnp.concatenate([zeros, sin, pad], axis=1))
    return dil, mla


def _prepare(ffn1_w_in, ffn1_w_out, ln1_g, ln1_b, w_in_mix, b_gate, q_norm_g, w_uq, kv_norm_g, w_ukv,
             w_branch_a, w_branch_b, w_out_mix, ln2_g, ln2_b, ffn2_w_in, ffn2_w_out, ln3_g, ln3_b):
    def ffn(w_in, w_out):
        pad = D_FF_PAD - D_FF
        chunked = lambda w: w.reshape(D_MODEL, D_FF_PAD // FF_CHUNK, FF_CHUNK).transpose(1, 0, 2)
        wg = chunked(jnp.pad(w_in[:, :D_FF], ((0, 0), (0, pad))).astype(BF16))
        wu = chunked(jnp.pad(w_in[:, D_FF:], ((0, 0), (0, pad))).astype(BF16))
        wo = jnp.pad(w_out, ((0, pad), (0, 0))).astype(BF16)
        return wg, wu, wo

    row = lambda a: a.reshape(1, -1)
    o_lat = Q_LORA + KV_LORA + QK_ROPE
    o_qk = o_lat + 2 * DIL_WIDTH
    o_v = o_qk + DIL_WIDTH
    w_lat = jnp.pad(w_in_mix[:, :o_lat], ((0, 0), (0, LAT_WIDTH - o_lat))).astype(BF16)
    uq = w_uq.reshape(Q_LORA, MLA_HEADS, QK_NOPE + QK_ROPE)
    ukv = w_ukv.reshape(KV_LORA, MLA_HEADS, QK_NOPE + V_DIM)
    flat = lambda a: a.reshape(a.shape[0], -1).astype(BF16)
    return dict(
        ffn1=ffn(ffn1_w_in, ffn1_w_out), ln1=(row(ln1_g), row(ln1_b)),
        ffn2=ffn(ffn2_w_in, ffn2_w_out), ln3=(row(ln3_g), row(ln3_b)),
        w_lat=w_lat,
        w_dil=w_in_mix[:, o_lat:o_v].astype(BF16),
        w_gate=w_in_mix[:, o_v:].astype(BF16),
        b_gate=row(b_gate),
        q_norm_g=row(q_norm_g), kv_norm_g=row(kv_norm_g),
        wqn=flat(uq[:, :, :QK_NOPE]),
        wqp=flat(jnp.pad(uq[:, :, QK_NOPE:], ((0, 0), (0, 0), (0, LANES - QK_ROPE)))),
        wkn=flat(ukv[:, :, :QK_NOPE]),
        wvt=flat(ukv[:, :, QK_NOPE:]).T,
        wa=w_branch_a.astype(BF16), wb=w_branch_b.astype(BF16), wo=w_out_mix.astype(BF16),
        ln2=(row(ln2_g), row(ln2_b)),
    )


def _encoder_layer(x, p):
    batch, seq, _ = x.shape
    x2 = x.reshape(batch * seq, D_MODEL)
    dil_tables, mla_tables = _rope_tables(seq)

    y1, y1_bf = _ffn_ln(x2, *p["ffn1"], *p["ln1"], emit_bf16=True)

    lat = _proj(y1_bf, p["w_lat"], F32, tn=LAT_WIDTH)
    qkv_d = _proj(y1_bf, p["w_dil"], F32, mode="dilated", extras=dil_tables, seq=seq, tn=DIL_WIDTH)
    gates = _proj(y1_bf, p["w_gate"], F32, mode="gate", extras=(p["b_gate"],), tn=D_MODEL)

    q, k, vt = _latent(lat, p["q_norm_g"], p["kv_norm_g"], p["wqn"], p["wqp"], p["wkn"], p["wvt"],
                       mla_tables, seq=seq)
    out_a = _mla(q, k, vt, batch=batch, seq=seq)

    outs, lses = [], []
    for group, (window, dilation) in enumerate(DIL_PATTERNS):
        o, lse = _dilated_group(qkv_d, group=group, window=window, dilation=dilation, batch=batch, seq=seq)
        outs.append(o)
        lses.append(lse)

    y2 = _merge(out_a, outs, lses, gates, y1, p["wa"], p["wb"], p["wo"], *p["ln2"])
    (y3,) = _ffn_ln(y2, *p["ffn2"], *p["ln3"], emit_bf16=False)
    return y3.reshape(batch, seq, D_MODEL)


def kernel(x_prompt, x_sample, ffn1_w_in, ffn1_w_out, ln1_g, ln1_b, w_in_mix, b_gate, q_norm_g, w_uq, kv_norm_g, w_ukv, w_branch_a, w_branch_b, w_out_mix, ln2_g, ln2_b, ffn2_w_in, ffn2_w_out, ln3_g, ln3_b):
    weights = (ffn1_w_in, ffn1_w_out, ln1_g, ln1_b, w_in_mix, b_gate, q_norm_g, w_uq, kv_norm_g, w_ukv,
               w_branch_a, w_branch_b, w_out_mix, ln2_g, ln2_b, ffn2_w_in, ffn2_w_out, ln3_g, ln3_b)
    y_prompt, y_sample = x_prompt, x_sample
    for layer in range(DEPTH):
        p = _prepare(*(w[layer] for w in weights))
        y_prompt = _encoder_layer(y_prompt, p)
        y_sample = _encoder_layer(y_sample, p)
    return (y_prompt, y_sample)
```

```python
import functools

import jax
import jax.numpy as jnp
from jax import lax
from jax.experimental import pallas as pl
from jax.experimental.pallas import tpu as pltpu

D_MODEL = 2048
DEPTH = 1
MLA_HEADS = 8
Q_LORA = 512
KV_LORA = 512
QK_NOPE = 128
QK_ROPE = 64
V_DIM = 128
DIL_PATTERNS = ((128, 1), (512, 4), (2048, 16))
HEADS_PER_GROUP = 4
DIL_HEAD_DIM = 128
DIL_WIDTH = len(DIL_PATTERNS) * HEADS_PER_GROUP * DIL_HEAD_DIM
D_FF = 5504
ROPE_THETA = 10000.0
LN_EPS = 1e-5
RMS_EPS = 1e-6
NEG_INF = -1e30
ALPHA = (2 * DEPTH) ** 0.25
MLA_SCALE = (QK_NOPE + QK_ROPE) ** -0.5
MLA_Q_SCALE = MLA_SCALE * 1.4426950408889634
DIL_SCALE = DIL_HEAD_DIM ** -0.5

LANES = 128
D_FF_PAD = 5632
FF_CHUNK = 512
LAT_WIDTH = Q_LORA + KV_LORA + LANES
MLA_SLAB = 2 * LANES
GROUP_WIDTH = HEADS_PER_GROUP * DIL_HEAD_DIM
PROJ_SUB = 512
DIL_ROWS = 2048
DIL_HEADS_PER_STEP = 2
DIL_UNITS_PER_TRIP = 4
VMEM_LIMIT = 56 * 1024 * 1024

BF16 = jnp.bfloat16
F32 = jnp.float32
_NT = (((1,), (1,)), ((), ()))


def _params(semantics):
    return pltpu.CompilerParams(dimension_semantics=semantics, vmem_limit_bytes=VMEM_LIMIT)


def _layer_norm(v, g, b):
    mu = jnp.mean(v, axis=-1, keepdims=True)
    c = v - mu
    var = jnp.mean(c * c, axis=-1, keepdims=True)
    return c * lax.rsqrt(var + LN_EPS) * g + b


def _ffn_ln_kernel(n_chunks, emit_bf16, x_ref, wg_ref, wu_ref, wo_ref, g_ref, b_ref, *rest):
    if emit_bf16:
        y_ref, ybf_ref, xbf_ref, acc_ref = rest
    else:
        y_ref, xbf_ref, acc_ref = rest
    k = pl.program_id(1)

    @pl.when(k == 0)
    def _():
        xbf_ref[...] = x_ref[...].astype(BF16)
        acc_ref[...] = jnp.zeros_like(acc_ref)

    xb = xbf_ref[...]
    gate = jnp.dot(xb, wg_ref[...], preferred_element_type=F32)
    up = jnp.dot(xb, wu_ref[...], preferred_element_type=F32)
    act = gate * jax.nn.sigmoid(gate) * up
    acc_ref[...] += jnp.dot(act.astype(BF16), wo_ref[...], preferred_element_type=F32)

    @pl.when(k == n_chunks - 1)
    def _():
        y = _layer_norm(ALPHA * x_ref[...] + 0.5 * acc_ref[...], g_ref[...], b_ref[...])
        y_ref[...] = y
        if emit_bf16:
            ybf_ref[...] = y.astype(BF16)


def _ffn_ln(x, wg, wu, wo, g, b, *, emit_bf16, tm=512):
    rows = x.shape[0]
    n_chunks = D_FF_PAD // FF_CHUNK
    out_shape = [jax.ShapeDtypeStruct((rows, D_MODEL), F32)]
    out_specs = [pl.BlockSpec((tm, D_MODEL), lambda i, k: (i, 0))]
    if emit_bf16:
        out_shape.append(jax.ShapeDtypeStruct((rows, D_MODEL), BF16))
        out_specs.append(pl.BlockSpec((tm, D_MODEL), lambda i, k: (i, 0)))
    return pl.pallas_call(
        functools.partial(_ffn_ln_kernel, n_chunks, emit_bf16),
        grid=(rows // tm, n_chunks),
        in_specs=[
            pl.BlockSpec((tm, D_MODEL), lambda i, k: (i, 0)),
            pl.BlockSpec((D_MODEL, FF_CHUNK), lambda i, k: (0, k)),
            pl.BlockSpec((D_MODEL, FF_CHUNK), lambda i, k: (0, k)),
            pl.BlockSpec((FF_CHUNK, D_MODEL), lambda i, k: (k, 0)),
            pl.BlockSpec((1, D_MODEL), lambda i, k: (0, 0)),
            pl.BlockSpec((1, D_MODEL), lambda i, k: (0, 0)),
        ],
        out_specs=out_specs,
        out_shape=out_shape,
        scratch_shapes=[pltpu.VMEM((tm, D_MODEL), BF16), pltpu.VMEM((tm, D_MODEL), F32)],
        compiler_params=_params(("parallel", "arbitrary")),
        name="ffn_ln",
    )(x, wg, wu, wo, g, b)


def _proj_kernel(mode, x_ref, w_ref, *rest):
    tn = w_ref.shape[1]
    starts = range(0, tn, PROJ_SUB)
    if mode == "dilated":
        starts = starts[::-1]

    def products():
        x = x_ref[...]
        return [(c, jnp.dot(x, w_ref[:, c:min(c + PROJ_SUB, tn)], preferred_element_type=F32)) for c in starts]

    if mode == "dilated":
        cos_ref, sin_ref, *o_refs, slab_ref = rest
        n_rope_steps = 2 * DIL_WIDTH // tn
        tm = x_ref.shape[0]

        def emit(rope):
            if rope:
                cos = cos_ref[...]
                sin = sin_ref[...]
            for c, z in products():
                group = c // PROJ_SUB
                d = DIL_PATTERNS[group][1]
                o_ref = o_refs[group]
                for h in range(HEADS_PER_GROUP):
                    zh = z[:, h * LANES:(h + 1) * LANES]
                    if rope:
                        zh = zh * cos + pltpu.roll(zh, LANES // 2, 1) * sin
                    if d == 1:
                        o_ref[h, 0] = zh.astype(BF16)
                        continue
                    slab = group * HEADS_PER_GROUP + h
                    slab_ref[slab] = zh
                    for r in range(d):
                        o_ref[h, r] = slab_ref[slab, pl.ds(r, tm // d, stride=d), :].astype(BF16)

        pl.when(pl.program_id(1) < n_rope_steps)(functools.partial(emit, True))
        pl.when(pl.program_id(1) >= n_rope_steps)(functools.partial(emit, False))
    elif mode == "gate":
        b_ref, o_ref = rest
        for c, z in products():
            sl = slice(c, c + z.shape[1])
            o_ref[:, sl] = jax.nn.sigmoid(z + b_ref[:, sl]).astype(o_ref.dtype)
    else:
        (o_ref,) = rest
        for c, z in products():
            o_ref[:, c:c + z.shape[1]] = z.astype(o_ref.dtype)


def _proj(x, w, out_dtype, *, mode="plain", extras=(), seq=None, tm=1024, tn=512):
    rows = x.shape[0]
    n = w.shape[1]
    tn = min(tn, n)
    in_specs = [
        pl.BlockSpec((tm, D_MODEL), lambda i, j: (i, 0)),
        pl.BlockSpec((D_MODEL, tn), lambda i, j: (0, j)),
    ]
    out_specs = pl.BlockSpec((tm, tn), lambda i, j: (i, j))
    out_shape = jax.ShapeDtypeStruct((rows, n), out_dtype)
    scratch_shapes = []
    if mode == "dilated":
        assert tn == DIL_WIDTH and PROJ_SUB == GROUP_WIDTH
        tiles_per_seq = seq // tm
        in_specs += [pl.BlockSpec((tm, LANES), lambda i, j: (i % tiles_per_seq, 0))] * 2
        out_specs = [
            pl.BlockSpec((HEADS_PER_GROUP, None, d, tm // d, LANES),
                         lambda i, j: (j, i // tiles_per_seq, 0, i % tiles_per_seq, 0))
            for _, d in DIL_PATTERNS]
        out_shape = [jax.ShapeDtypeStruct((3 * HEADS_PER_GROUP, rows // seq, d, seq // d, LANES), out_dtype)
                     for _, d in DIL_PATTERNS]
        scratch_shapes = [pltpu.VMEM((len(DIL_PATTERNS) * HEADS_PER_GROUP, tm, LANES), F32)]
    elif mode == "gate":
        in_specs += [pl.BlockSpec((1, tn), lambda i, j: (0, j))]
    return pl.pallas_call(
        functools.partial(_proj_kernel, mode),
        grid=(rows // tm, n // tn),
        in_specs=in_specs,
        out_specs=out_specs,
        out_shape=out_shape,
        scratch_shapes=scratch_shapes,
        compiler_params=_params(("parallel", "arbitrary")),
        name="proj_" + mode,
    )(x, w, *extras)


def _latent_kernel(lat_ref, qg_ref, kvg_ref, wqn_ref, wqp_ref, wkn_ref, wvt_ref,
                   cos_ref, sin_lo_ref, sin_hi_ref, q_ref, k_ref, vt_ref):
    lat = lat_ref[...]
    c_q = lat[:, :Q_LORA]
    c_kv = lat[:, Q_LORA:Q_LORA + KV_LORA]
    k_rope = lat[:, Q_LORA + KV_LORA:]

    def rms(x, g):
        return (x * lax.rsqrt(jnp.mean(x * x, axis=-1, keepdims=True) + RMS_EPS) * g).astype(BF16)

    cos = cos_ref[...]
    sin_lo = sin_lo_ref[...]
    sin_hi = sin_hi_ref[...]

    def rope64(x):
        quarter = QK_ROPE // 2
        return (x * cos + pltpu.roll(x, LANES - quarter, 1) * sin_lo
                + pltpu.roll(x, quarter, 1) * sin_hi)

    nq = rms(c_q, qg_ref[...])
    nkv = rms(c_kv, kvg_ref[...])
    q_nope = jnp.dot(nq, wqn_ref[...], preferred_element_type=F32)
    q_pe = jnp.dot(nq, wqp_ref[...], preferred_element_type=F32)
    k_nope = jnp.dot(nkv, wkn_ref[...], preferred_element_type=F32)
    vt = lax.dot_general(wvt_ref[...], nkv, _NT, preferred_element_type=F32)
    k_pe = rope64(k_rope).astype(BF16)
    for h in range(MLA_HEADS):
        src = slice(h * LANES, (h + 1) * LANES)
        lo = slice(h * MLA_SLAB, h * MLA_SLAB + LANES)
        hi = slice(h * MLA_SLAB + LANES, (h + 1) * MLA_SLAB)
        q_ref[:, lo] = (q_nope[:, src] * MLA_Q_SCALE).astype(BF16)
        q_ref[:, hi] = (rope64(q_pe[:, src]) * MLA_Q_SCALE).astype(BF16)
        k_ref[:, lo] = k_nope[:, src].astype(BF16)
        k_ref[:, hi] = k_pe
    vt_ref[...] = vt.astype(BF16)


def _latent(lat, qg, kvg, wqn, wqp, wkn, wvt, tables, *, seq, tm=512):
    rows = lat.shape[0]
    tiles_per_seq = seq // tm
    width = MLA_HEADS * LANES
    row_spec = lambda w: pl.BlockSpec((tm, w), lambda i: (i, 0))
    full = lambda a: pl.BlockSpec(a.shape, lambda i: (0, 0), pipeline_mode=pl.Buffered(1))
    table_spec = pl.BlockSpec((tm, LANES), lambda i: (i % tiles_per_seq, 0))
    return pl.pallas_call(
        _latent_kernel,
        grid=(rows // tm,),
        in_specs=[row_spec(LAT_WIDTH), full(qg), full(kvg), full(wqn), full(wqp), full(wkn), full(wvt),
                  table_spec, table_spec, table_spec],
        out_specs=[row_spec(MLA_HEADS * MLA_SLAB), row_spec(MLA_HEADS * MLA_SLAB),
                   pl.BlockSpec((width, tm), lambda i: (0, i))],
        out_shape=[jax.ShapeDtypeStruct((rows, MLA_HEADS * MLA_SLAB), BF16),
                   jax.ShapeDtypeStruct((rows, MLA_HEADS * MLA_SLAB), BF16),
                   jax.ShapeDtypeStruct((width, rows), BF16)],
        compiler_params=_params(("parallel",)),
        name="latent",
    )(lat, qg, kvg, wqn, wqp, wkn, wvt, *tables)


def _mla_kernel(n_kv, tk, q_ref, k_ref, vt_ref, o_ref, acc_ref, s_ref, mblk_ref):
    n_sub, _, ts = acc_ref.shape
    acc_ref[...] = jnp.zeros_like(acc_ref)

    def scores_into(slot, j):
        k = k_ref[pl.ds(pl.multiple_of(j * tk, tk), tk), :]
        for t in range(n_sub):
            s = lax.dot_general(k, q_ref[pl.ds(t * ts, ts), :], _NT, preferred_element_type=F32)
            s_ref[slot, t] = s
            mblk_ref[slot, t] = jnp.max(s, axis=0, keepdims=True)

    def consume(slot, j, carry):
        vt = vt_ref[:, pl.ds(pl.multiple_of(j * tk, tk), tk)]
        out = []
        for t in range(n_sub):
            m_prev, l_prev = carry[t]
            m_new = jnp.maximum(m_prev, mblk_ref[slot, t])
            alpha = jnp.exp2(m_prev - m_new)
            p = jnp.exp2(s_ref[slot, t] - m_new)
            l_new = alpha * l_prev + jnp.sum(p, axis=0, keepdims=True)
            acc_ref[t] = alpha * acc_ref[t] + jnp.dot(vt, p.astype(BF16), preferred_element_type=F32)
            out.append((m_new, l_new))
        return tuple(out)

    def body(jj, carry):
        j = 2 * jj
        scores_into(1, j + 1)
        carry = consume(0, j, carry)
        scores_into(0, j + 2)
        return consume(1, j + 1, carry)

    scores_into(0, 0)
    init = tuple((jnp.full((1, ts), -jnp.inf, F32), jnp.zeros((1, ts), F32)) for _ in range(n_sub))
    carry = lax.fori_loop(0, n_kv // 2 - 1, body, init)
    scores_into(1, n_kv - 1)
    carry = consume(0, n_kv - 2, carry)
    carry = consume(1, n_kv - 1, carry)
    for t in range(n_sub):
        o_ref[pl.ds(t * ts, ts), :] = (acc_ref[t] / carry[t][1]).T.astype(o_ref.dtype)


def _mla(q, k, vt, *, batch, seq, tq=2048, ts=512, tk=512):
    n_q = seq // tq
    return pl.pallas_call(
        functools.partial(_mla_kernel, seq // tk, tk),
        grid=(batch, MLA_HEADS, n_q),
        in_specs=[
            pl.BlockSpec((tq, MLA_SLAB), lambda b, h, i: (b * n_q + i, h)),
            pl.BlockSpec((seq, MLA_SLAB), lambda b, h, i: (b, h)),
            pl.BlockSpec((V_DIM, seq), lambda b, h, i: (h, b)),
        ],
        out_specs=pl.BlockSpec((tq, V_DIM), lambda b, h, i: (b * n_q + i, h)),
        out_shape=jax.ShapeDtypeStruct((batch * seq, MLA_HEADS * V_DIM), BF16),
        scratch_shapes=[pltpu.VMEM((tq // ts, V_DIM, ts), F32), pltpu.VMEM((2, tq // ts, tk, ts), F32),
                        pltpu.VMEM((2, tq // ts, 1, ts), F32)],
        compiler_params=_params(("parallel", "parallel", "arbitrary")),
        name="mla",
    )(q, k, vt)


def _dilated_kernel(d, half, stream_len, q_ref, kp_ref, kc_ref, kn_ref, vp_ref, vc_ref, vn_ref,
                    o_ref, lse_ref, ks_ref, vs_ref):
    n_heads = q_ref.shape[0]
    tl = DIL_ROWS // d
    i = pl.program_id(2)
    for dst, before, cur, after in ((ks_ref, kp_ref, kc_ref, kn_ref), (vs_ref, vp_ref, vc_ref, vn_ref)):
        dst[:, :, 0:half] = before[...]
        dst[:, :, half:half + tl] = cur[...]
        dst[:, :, half + tl:] = after[...]

    chunk = 2 * half
    row = lax.broadcasted_iota(jnp.int32, (chunk, 2 * chunk), 0)
    col = lax.broadcasted_iota(jnp.int32, (chunk, 2 * chunk), 1)
    band = (col >= row) & (col <= row + 2 * half)
    shift = d.bit_length() - 1

    def unit_group(ug, carry):
        jobs = []
        for uu in range(DIL_UNITS_PER_TRIP):
            u = ug * DIL_UNITS_PER_TRIP + uu
            r = u & (d - 1)
            c = u >> shift
            row0 = pl.multiple_of(c * chunk, chunk)
            k_idx = i * tl + c * chunk - half + col
            mask = band & (k_idx >= 0) & (k_idx < stream_len)
            for h in range(n_heads):
                q = q_ref[h, r, pl.ds(row0, chunk), :]
                k = ks_ref[h, r, pl.ds(row0, 2 * chunk), :]
                s = lax.dot_general(q, k, _NT, preferred_element_type=F32) * DIL_SCALE
                jobs.append((h, r, row0, mask, s))
        for h, r, row0, mask, s in jobs:
            s = jnp.where(mask, s, NEG_INF)
            m = jnp.max(s, axis=1, keepdims=True)
            p = jnp.exp(s - m)
            denom = jnp.sum(p, axis=1, keepdims=True)
            pn = (p / denom).astype(BF16)
            o = jnp.dot(pn, vs_ref[h, r, pl.ds(row0, 2 * chunk), :], preferred_element_type=F32)
            start = r + d * row0
            o_ref[h, pl.ds(start, chunk, stride=d), :] = o
            lse_ref[h, pl.ds(start, chunk, stride=d), :] = jnp.broadcast_to(m + jnp.log(denom), (chunk, LANES))
        return carry

    lax.fori_loop(0, DIL_ROWS // chunk // DIL_UNITS_PER_TRIP, unit_group, 0)


def _dilated_group(qkv, *, group, window, dilation, batch, seq):
    d = dilation
    half = window // (2 * d)
    n_r = seq // DIL_ROWS
    tl = DIL_ROWS // d
    halo_per_step = tl // half
    n_halo = seq // d // half
    steps_per_group = HEADS_PER_GROUP // DIL_HEADS_PER_STEP
    blk = (DIL_HEADS_PER_STEP, None, d, tl, LANES)
    hblk = (DIL_HEADS_PER_STEP, None, d, half, LANES)

    def specs(kind):
        heads = lambda g: kind * steps_per_group + g
        cur = pl.BlockSpec(blk, lambda g, b, i: (heads(g), b, 0, i, 0))
        before = pl.BlockSpec(hblk, lambda g, b, i: (heads(g), b, 0, jnp.maximum(i * halo_per_step - 1, 0), 0))
        after = pl.BlockSpec(hblk, lambda g, b, i: (heads(g), b, 0,
                                                    jnp.minimum((i + 1) * halo_per_step, n_halo - 1), 0))
        return before, cur, after

    out_spec = pl.BlockSpec((DIL_HEADS_PER_STEP, DIL_ROWS, LANES), lambda g, b, i: (g, b * n_r + i, 0))
    out_shape = jax.ShapeDtypeStruct((HEADS_PER_GROUP, batch * seq, LANES), F32)
    return pl.pallas_call(
        functools.partial(_dilated_kernel, d, half, seq // d),
        grid=(steps_per_group, batch, n_r),
        in_specs=[specs(0)[1], *specs(1), *specs(2)],
        out_specs=[out_spec, out_spec],
        out_shape=[out_shape, out_shape],
        scratch_shapes=[pltpu.VMEM((DIL_HEADS_PER_STEP, d, tl + 2 * half, LANES), BF16),
                        pltpu.VMEM((DIL_HEADS_PER_STEP, d, tl + 2 * half, LANES), BF16)],
        compiler_params=_params(("parallel", "parallel", "parallel")),
        name="dilated_g%d" % group,
    )(*([qkv] * 7))


def _merge_kernel(oa_ref, o0_ref, o1_ref, o2_ref, l0_ref, l1_ref, l2_ref, ga_ref, gb_ref, y1_ref,
                  wa_ref, wb_ref, wo_ref, g_ref, b_ref, y_ref):
    heads = []
    for h in range(HEADS_PER_GROUP):
        l0, l1, l2 = l0_ref[h], l1_ref[h], l2_ref[h]
        m = jnp.maximum(jnp.maximum(l0, l1), l2)
        e0, e1, e2 = jnp.exp(l0 - m), jnp.exp(l1 - m), jnp.exp(l2 - m)
        denom = e0 + e1 + e2
        heads.append(((e0 / denom) * o0_ref[h] + (e1 / denom) * o1_ref[h] + (e2 / denom) * o2_ref[h]).astype(BF16))
    out_b = jnp.concatenate(heads, axis=1)
    branch_a = jnp.dot(oa_ref[...], wa_ref[...], preferred_element_type=F32)
    branch_b = jnp.dot(out_b, wb_ref[...], preferred_element_type=F32)
    merged = ga_ref[...] * branch_a + gb_ref[...] * branch_b
    mix = jnp.dot(merged.astype(BF16), wo_ref[...], preferred_element_type=F32)
    y_ref[...] = _layer_norm(ALPHA * y1_ref[...] + mix, g_ref[...], b_ref[...])


def _merge(out_a, outs, lses, gates, y1, wa, wb, wo, g, b, *, tm=256):
    rows = y1.shape[0]
    row_spec = lambda w, c=0: pl.BlockSpec((tm, w), lambda i: (i, c))
    full = lambda a: pl.BlockSpec(a.shape, lambda i: (0, 0), pipeline_mode=pl.Buffered(1))
    return pl.pallas_call(
        _merge_kernel,
        grid=(rows // tm,),
        in_specs=[row_spec(MLA_HEADS * V_DIM)]
                 + [pl.BlockSpec((HEADS_PER_GROUP, tm, LANES), lambda i: (0, i, 0))] * 6
                 + [row_spec(D_MODEL, 0), row_spec(D_MODEL, 1), row_spec(D_MODEL)]
                 + [full(wa), full(wb), full(wo), full(g), full(b)],
        out_specs=row_spec(D_MODEL),
        out_shape=jax.ShapeDtypeStruct((rows, D_MODEL), F32),
        compiler_params=_params(("parallel",)),
        name="merge",
    )(out_a, *outs, *lses, gates, gates, y1, wa, wb, wo, g, b)


def _ffn_in_kernel(w_ref, wg_ref, wu_ref):
    w = w_ref[...]
    pad = jnp.zeros((w.shape[0], D_FF_PAD - D_FF), BF16)
    for dst, part in ((wg_ref, w[:, :D_FF]), (wu_ref, w[:, D_FF:])):
        dst[:, :D_FF] = part.astype(BF16)
        dst[:, D_FF:] = pad


def _ffn_out_kernel(n_valid, w_ref, o_ref):
    @pl.when(pl.program_id(0) < n_valid)
    def _():
        o_ref[...] = w_ref[...].astype(BF16)

    @pl.when(pl.program_id(0) >= n_valid)
    def _():
        o_ref[...] = jnp.zeros_like(o_ref)


def _ffn_weights(w_in, w_out, *, tr=64, tc=LANES):
    half = jax.ShapeDtypeStruct((D_MODEL, D_FF_PAD), BF16)
    wg, wu = pl.pallas_call(
        _ffn_in_kernel,
        grid=(D_MODEL // tr,),
        in_specs=[pl.BlockSpec((tr, 2 * D_FF), lambda i: (i, 0))],
        out_specs=[pl.BlockSpec((tr, D_FF_PAD), lambda i: (i, 0))] * 2,
        out_shape=[half, half],
        compiler_params=_params(("parallel",)),
        name="ffn_w_in",
    )(w_in)
    n_valid = D_FF // tc
    wo = pl.pallas_call(
        functools.partial(_ffn_out_kernel, n_valid),
        grid=(D_FF_PAD // tc,),
        in_specs=[pl.BlockSpec((tc, D_MODEL), lambda i: (jnp.minimum(i, n_valid - 1), 0))],
        out_specs=pl.BlockSpec((tc, D_MODEL), lambda i: (i, 0)),
        out_shape=jax.ShapeDtypeStruct((D_FF_PAD, D_MODEL), BF16),
        compiler_params=_params(("parallel",)),
        name="ffn_w_out",
    )(w_out)
    return wg, wu, wo


def _rope_tables(seq):
    pos = jnp.arange(seq, dtype=F32)[:, None]

    def angles(d):
        inv_freq = ROPE_THETA ** (-jnp.arange(0, d, 2, dtype=F32) / d)
        ang = pos * inv_freq[None, :]
        return jnp.cos(ang), jnp.sin(ang)

    cos, sin = angles(DIL_HEAD_DIM)
    dil = (jnp.concatenate([cos, cos], axis=1), jnp.concatenate([-sin, sin], axis=1))
    cos, sin = angles(QK_ROPE)
    zeros = jnp.zeros_like(cos)
    pad = jnp.zeros((seq, LANES - QK_ROPE), F32)
    mla = (jnp.concatenate([cos, cos, pad], axis=1),
           jnp.concatenate([-sin, zeros, pad], axis=1),
           jnp.concatenate([zeros, sin, pad], axis=1))
    return dil, mla


def _prepare(ffn1_w_in, ffn1_w_out, ln1_g, ln1_b, w_in_mix, b_gate, q_norm_g, w_uq, kv_norm_g, w_ukv,
             w_branch_a, w_branch_b, w_out_mix, ln2_g, ln2_b, ffn2_w_in, ffn2_w_out, ln3_g, ln3_b):
    ffn = _ffn_weights
    row = lambda a: a.reshape(1, -1)
    o_lat = Q_LORA + KV_LORA + QK_ROPE
    o_qk = o_lat + 2 * DIL_WIDTH
    o_v = o_qk + DIL_WIDTH
    w_lat = jnp.pad(w_in_mix[:, :o_lat], ((0, 0), (0, LAT_WIDTH - o_lat))).astype(BF16)
    uq = w_uq.reshape(Q_LORA, MLA_HEADS, QK_NOPE + QK_ROPE)
    ukv = w_ukv.reshape(KV_LORA, MLA_HEADS, QK_NOPE + V_DIM)
    flat = lambda a: a.reshape(a.shape[0], -1).astype(BF16)
    return dict(
        ffn1=ffn(ffn1_w_in, ffn1_w_out), ln1=(row(ln1_g), row(ln1_b)),
        ffn2=ffn(ffn2_w_in, ffn2_w_out), ln3=(row(ln3_g), row(ln3_b)),
        w_lat=w_lat,
        w_dil=w_in_mix[:, o_lat:o_v].astype(BF16),
        w_gate=w_in_mix[:, o_v:].astype(BF16),
        b_gate=row(b_gate),
        q_norm_g=row(q_norm_g), kv_norm_g=row(kv_norm_g),
        wqn=flat(uq[:, :, :QK_NOPE]),
        wqp=flat(jnp.pad(uq[:, :, QK_NOPE:], ((0, 0), (0, 0), (0, LANES - QK_ROPE)))),
        wkn=flat(ukv[:, :, :QK_NOPE]),
        wvt=flat(ukv[:, :, QK_NOPE:]).T,
        wa=w_branch_a.astype(BF16), wb=w_branch_b.astype(BF16), wo=w_out_mix.astype(BF16),
        ln2=(row(ln2_g), row(ln2_b)),
    )


def _encoder_layer(x, p):
    batch, seq, _ = x.shape
    x2 = x.reshape(batch * seq, D_MODEL)
    dil_tables, mla_tables = _rope_tables(seq)

    y1, y1_bf = _ffn_ln(x2, *p["ffn1"], *p["ln1"], emit_bf16=True)

    lat = _proj(y1_bf, p["w_lat"], F32, tn=LAT_WIDTH)
    qkv_d = _proj(y1_bf, p["w_dil"], BF16, mode="dilated", extras=dil_tables, seq=seq, tn=DIL_WIDTH)
    gates = _proj(y1_bf, p["w_gate"], F32, mode="gate", extras=(p["b_gate"],), tn=D_MODEL)

    q, k, vt = _latent(lat, p["q_norm_g"], p["kv_norm_g"], p["wqn"], p["wqp"], p["wkn"], p["wvt"],
                       mla_tables, seq=seq)
    out_a = _mla(q, k, vt, batch=batch, seq=seq)

    outs, lses = [], []
    for group, (window, dilation) in enumerate(DIL_PATTERNS):
        o, lse = _dilated_group(qkv_d[group], group=group, window=window, dilation=dilation, batch=batch, seq=seq)
        outs.append(o)
        lses.append(lse)

    y2 = _merge(out_a, outs, lses, gates, y1, p["wa"], p["wb"], p["wo"], *p["ln2"])
    (y3,) = _ffn_ln(y2, *p["ffn2"], *p["ln3"], emit_bf16=False)
    return y3.reshape(batch, seq, D_MODEL)


def kernel(x_prompt, x_sample, ffn1_w_in, ffn1_w_out, ln1_g, ln1_b, w_in_mix, b_gate, q_norm_g, w_uq, kv_norm_g, w_ukv, w_branch_a, w_branch_b, w_out_mix, ln2_g, ln2_b, ffn2_w_in, ffn2_w_out, ln3_g, ln3_b):
    weights = (ffn1_w_in, ffn1_w_out, ln1_g, ln1_b, w_in_mix, b_gate, q_norm_g, w_uq, kv_norm_g, w_ukv,
               w_branch_a, w_branch_b, w_out_mix, ln2_g, ln2_b, ffn2_w_in, ffn2_w_out, ln3_g, ln3_b)
    y_prompt, y_sample = x_prompt, x_sample
    for layer in range(DEPTH):
        p = _prepare(*(w[layer] for w in weights))
        y_prompt = _encoder_layer(y_prompt, p)
        y_sample = _encoder_layer(y_sample, p)
    return (y_prompt, y_sample)
```

```python
import functools

import jax
import jax.numpy as jnp
from jax import lax
from jax.experimental import pallas as pl
from jax.experimental.pallas import tpu as pltpu

D_MODEL = 2048
DEPTH = 1
MLA_HEADS = 8
Q_LORA = 512
KV_LORA = 512
QK_NOPE = 128
QK_ROPE = 64
V_DIM = 128
DIL_PATTERNS = ((128, 1), (512, 4), (2048, 16))
HEADS_PER_GROUP = 4
DIL_HEAD_DIM = 128
DIL_WIDTH = len(DIL_PATTERNS) * HEADS_PER_GROUP * DIL_HEAD_DIM
D_FF = 5504
ROPE_THETA = 10000.0
LN_EPS = 1e-5
RMS_EPS = 1e-6
NEG_INF = -1e30
ALPHA = (2 * DEPTH) ** 0.25
MLA_SCALE = (QK_NOPE + QK_ROPE) ** -0.5
MLA_Q_SCALE = MLA_SCALE * 1.4426950408889634
DIL_SCALE = DIL_HEAD_DIM ** -0.5

LANES = 128
D_FF_PAD = 5632
FF_CHUNK = 512
LAT_WIDTH = Q_LORA + KV_LORA + LANES
MLA_SLAB = 2 * LANES
GROUP_WIDTH = HEADS_PER_GROUP * DIL_HEAD_DIM
PROJ_SUB = 512
DIL_ROWS = 2048
DIL_HEADS_PER_STEP = 4
DIL_UNITS_PER_TRIP = 2
VMEM_LIMIT = 56 * 1024 * 1024

BF16 = jnp.bfloat16
F32 = jnp.float32
_NT = (((1,), (1,)), ((), ()))


def _params(semantics):
    return pltpu.CompilerParams(dimension_semantics=semantics, vmem_limit_bytes=VMEM_LIMIT)


def _layer_norm(v, g, b, scale=1.0):
    mu = jnp.mean(v, axis=-1, keepdims=True)
    c = v - mu
    var = jnp.mean(c * c, axis=-1, keepdims=True)
    return c * (scale * lax.rsqrt(scale * scale * var + LN_EPS)) * g + b


def _ffn_ln_kernel(n_chunks, emit_bf16, x_ref, wg_ref, wu_ref, wo_ref, g_ref, b_ref, *rest):
    if emit_bf16:
        y_ref, ybf_ref, xbf_ref, acc_ref = rest
    else:
        y_ref, xbf_ref, acc_ref = rest
    k = pl.program_id(1)

    @pl.when(k == 0)
    def _():
        x = x_ref[...]
        xbf_ref[...] = x.astype(BF16)
        acc_ref[...] = (2.0 * ALPHA) * x

    xb = xbf_ref[...]
    gate = jnp.dot(xb, wg_ref[...], preferred_element_type=F32)
    up = jnp.dot(xb, wu_ref[...], preferred_element_type=F32)
    act = gate * jax.nn.sigmoid(gate) * up
    acc_ref[...] += jnp.dot(act.astype(BF16), wo_ref[...], preferred_element_type=F32)

    @pl.when(k == n_chunks - 1)
    def _():
        y = _layer_norm(acc_ref[...], g_ref[...], b_ref[...], scale=0.5)
        y_ref[...] = y
        if emit_bf16:
            ybf_ref[...] = y.astype(BF16)


def _ffn_ln(x, wg, wu, wo, g, b, *, emit_bf16, tm=512):
    rows = x.shape[0]
    n_chunks = D_FF_PAD // FF_CHUNK
    out_shape = [jax.ShapeDtypeStruct((rows, D_MODEL), F32)]
    out_specs = [pl.BlockSpec((tm, D_MODEL), lambda i, k: (i, 0))]
    if emit_bf16:
        out_shape.append(jax.ShapeDtypeStruct((rows, D_MODEL), BF16))
        out_specs.append(pl.BlockSpec((tm, D_MODEL), lambda i, k: (i, 0)))
    return pl.pallas_call(
        functools.partial(_ffn_ln_kernel, n_chunks, emit_bf16),
        grid=(rows // tm, n_chunks),
        in_specs=[
            pl.BlockSpec((tm, D_MODEL), lambda i, k: (i, 0)),
            pl.BlockSpec((D_MODEL, FF_CHUNK), lambda i, k: (0, k)),
            pl.BlockSpec((D_MODEL, FF_CHUNK), lambda i, k: (0, k)),
            pl.BlockSpec((FF_CHUNK, D_MODEL), lambda i, k: (k, 0)),
            pl.BlockSpec((1, D_MODEL), lambda i, k: (0, 0)),
            pl.BlockSpec((1, D_MODEL), lambda i, k: (0, 0)),
        ],
        out_specs=out_specs,
        out_shape=out_shape,
        scratch_shapes=[pltpu.VMEM((tm, D_MODEL), BF16), pltpu.VMEM((tm, D_MODEL), F32)],
        compiler_params=_params(("parallel", "arbitrary")),
        name="ffn_ln",
    )(x, wg, wu, wo, g, b)


def _proj_kernel(mode, x_ref, w_ref, *rest):
    tn = w_ref.shape[1]
    starts = range(0, tn, PROJ_SUB)

    def products():
        x = x_ref[...]
        return [(c, jnp.dot(x, w_ref[:, c:min(c + PROJ_SUB, tn)], preferred_element_type=F32)) for c in starts]

    if mode == "dilated":
        cos_ref, sin_ref, *o_refs, slab_ref = rest
        n_rope_steps = 2 * DIL_WIDTH // tn
        tm = x_ref.shape[0]

        def emit(rope):
            if rope:
                cos = cos_ref[...]
                sin = sin_ref[...]
            for c, z in products():
                group = c // PROJ_SUB
                d = DIL_PATTERNS[group][1]
                o_ref = o_refs[group]
                for h in range(HEADS_PER_GROUP):
                    zh = z[:, h * LANES:(h + 1) * LANES]
                    if rope:
                        zh = zh * cos + pltpu.roll(zh, LANES // 2, 1) * sin
                    if d == 1:
                        o_ref[h, 0] = zh.astype(BF16)
                        continue
                    slab = group * HEADS_PER_GROUP + h
                    slab_ref[slab] = zh
                    for r in range(d):
                        o_ref[h, r] = slab_ref[slab, pl.ds(r, tm // d, stride=d), :].astype(BF16)

        pl.when(pl.program_id(1) < n_rope_steps)(functools.partial(emit, True))
        pl.when(pl.program_id(1) >= n_rope_steps)(functools.partial(emit, False))
    elif mode == "gate":
        b_ref, o_ref = rest
        for c, z in products():
            sl = slice(c, c + z.shape[1])
            o_ref[:, sl] = jax.nn.sigmoid(z + b_ref[:, sl]).astype(o_ref.dtype)
    else:
        (o_ref,) = rest
        for c, z in products():
            o_ref[:, c:c + z.shape[1]] = z.astype(o_ref.dtype)


def _proj(x, w, out_dtype, *, mode="plain", extras=(), seq=None, tm=1024, tn=512):
    rows = x.shape[0]
    n = w.shape[1]
    tn = min(tn, n)
    in_specs = [
        pl.BlockSpec((tm, D_MODEL), lambda i, j: (i, 0)),
        pl.BlockSpec((D_MODEL, tn), lambda i, j: (0, j)),
    ]
    out_specs = pl.BlockSpec((tm, tn), lambda i, j: (i, j))
    out_shape = jax.ShapeDtypeStruct((rows, n), out_dtype)
    scratch_shapes = []
    if mode == "dilated":
        assert tn == DIL_WIDTH and PROJ_SUB == GROUP_WIDTH
        tiles_per_seq = seq // tm
        in_specs += [pl.BlockSpec((tm, LANES), lambda i, j: (i % tiles_per_seq, 0))] * 2
        out_specs = [
            pl.BlockSpec((HEADS_PER_GROUP, None, d, tm // d, LANES),
                         lambda i, j: (j, i // tiles_per_seq, 0, i % tiles_per_seq, 0))
            for _, d in DIL_PATTERNS]
        out_shape = [jax.ShapeDtypeStruct((3 * HEADS_PER_GROUP, rows // seq, d, seq // d, LANES), out_dtype)
                     for _, d in DIL_PATTERNS]
        scratch_shapes = [pltpu.VMEM((len(DIL_PATTERNS) * HEADS_PER_GROUP, tm, LANES), F32)]
    elif mode == "gate":
        in_specs += [pl.BlockSpec((1, tn), lambda i, j: (0, j))]
    return pl.pallas_call(
        functools.partial(_proj_kernel, mode),
        grid=(rows // tm, n // tn),
        in_specs=in_specs,
        out_specs=out_specs,
        out_shape=out_shape,
        scratch_shapes=scratch_shapes,
        compiler_params=_params(("parallel", "arbitrary")),
        name="proj_" + mode,
    )(x, w, *extras)


def _latent_kernel(lat_ref, qg_ref, kvg_ref, wqn_ref, wqp_ref, wkn_ref, wvt_ref,
                   cos_ref, sin_lo_ref, sin_hi_ref, q_ref, k_ref, vt_ref):
    lat = lat_ref[...]
    c_q = lat[:, :Q_LORA]
    c_kv = lat[:, Q_LORA:Q_LORA + KV_LORA]
    k_rope = lat[:, Q_LORA + KV_LORA:]

    def rms(x, g):
        return (x * lax.rsqrt(jnp.mean(x * x, axis=-1, keepdims=True) + RMS_EPS) * g).astype(BF16)

    cos = cos_ref[...]
    sin_lo = sin_lo_ref[...]
    sin_hi = sin_hi_ref[...]

    def rope64(x):
        quarter = QK_ROPE // 2
        return (x * cos + pltpu.roll(x, LANES - quarter, 1) * sin_lo
                + pltpu.roll(x, quarter, 1) * sin_hi)

    nq = rms(c_q, qg_ref[...])
    nkv = rms(c_kv, kvg_ref[...])
    q_nope = jnp.dot(nq, wqn_ref[...], preferred_element_type=F32)
    q_pe = jnp.dot(nq, wqp_ref[...], preferred_element_type=F32)
    k_nope = jnp.dot(nkv, wkn_ref[...], preferred_element_type=F32)
    vt = lax.dot_general(wvt_ref[...], nkv, _NT, preferred_element_type=F32)
    k_pe = rope64(k_rope).astype(BF16)
    for h in range(MLA_HEADS):
        src = slice(h * LANES, (h + 1) * LANES)
        lo = slice(h * MLA_SLAB, h * MLA_SLAB + LANES)
        hi = slice(h * MLA_SLAB + LANES, (h + 1) * MLA_SLAB)
        q_ref[:, lo] = (q_nope[:, src] * MLA_Q_SCALE).astype(BF16)
        q_ref[:, hi] = (rope64(q_pe[:, src]) * MLA_Q_SCALE).astype(BF16)
        k_ref[:, lo] = k_nope[:, src].astype(BF16)
        k_ref[:, hi] = k_pe
    vt_ref[...] = vt.astype(BF16)


def _latent(lat, qg, kvg, wqn, wqp, wkn, wvt, tables, *, seq, tm=512):
    rows = lat.shape[0]
    tiles_per_seq = seq // tm
    width = MLA_HEADS * LANES
    row_spec = lambda w: pl.BlockSpec((tm, w), lambda i: (i, 0))
    full = lambda a: pl.BlockSpec(a.shape, lambda i: (0, 0), pipeline_mode=pl.Buffered(1))
    table_spec = pl.BlockSpec((tm, LANES), lambda i: (i % tiles_per_seq, 0))
    return pl.pallas_call(
        _latent_kernel,
        grid=(rows // tm,),
        in_specs=[row_spec(LAT_WIDTH), full(qg), full(kvg), full(wqn), full(wqp), full(wkn), full(wvt),
                  table_spec, table_spec, table_spec],
        out_specs=[row_spec(MLA_HEADS * MLA_SLAB), row_spec(MLA_HEADS * MLA_SLAB),
                   pl.BlockSpec((width, tm), lambda i: (0, i))],
        out_shape=[jax.ShapeDtypeStruct((rows, MLA_HEADS * MLA_SLAB), BF16),
                   jax.ShapeDtypeStruct((rows, MLA_HEADS * MLA_SLAB), BF16),
                   jax.ShapeDtypeStruct((width, rows), BF16)],
        compiler_params=_params(("parallel",)),
        name="latent",
    )(lat, qg, kvg, wqn, wqp, wkn, wvt, *tables)


def _mla_kernel(n_kv, tk, q_ref, k_ref, vt_ref, o_ref, acc_ref, s_ref, mblk_ref):
    n_sub, _, ts = acc_ref.shape
    acc_ref[...] = jnp.zeros_like(acc_ref)

    def scores_into(slot, j):
        k = k_ref[pl.ds(pl.multiple_of(j * tk, tk), tk), :]
        for t in range(n_sub):
            s = lax.dot_general(k, q_ref[pl.ds(t * ts, ts), :], _NT, preferred_element_type=F32)
            s_ref[slot, t] = s
            mblk_ref[slot, t] = jnp.max(s, axis=0, keepdims=True)

    def consume(slot, j, carry):
        vt = vt_ref[:, pl.ds(pl.multiple_of(j * tk, tk), tk)]
        out = []
        for t in range(n_sub):
            m_prev, l_prev = carry[t]
            m_new = jnp.maximum(m_prev, mblk_ref[slot, t])
            alpha = jnp.exp2(m_prev - m_new)
            p = jnp.exp2(s_ref[slot, t] - m_new)
            l_new = alpha * l_prev + jnp.sum(p, axis=0, keepdims=True)
            acc_ref[t] = alpha * acc_ref[t] + jnp.dot(vt, p.astype(BF16), preferred_element_type=F32)
            out.append((m_new, l_new))
        return tuple(out)

    def body(jj, carry):
        j = 2 * jj
        scores_into(1, j + 1)
        carry = consume(0, j, carry)
        scores_into(0, j + 2)
        return consume(1, j + 1, carry)

    scores_into(0, 0)
    init = tuple((jnp.full((1, ts), -jnp.inf, F32), jnp.zeros((1, ts), F32)) for _ in range(n_sub))
    carry = lax.fori_loop(0, n_kv // 2 - 1, body, init)
    scores_into(1, n_kv - 1)
    carry = consume(0, n_kv - 2, carry)
    carry = consume(1, n_kv - 1, carry)
    for t in range(n_sub):
        o_ref[pl.ds(t * ts, ts), :] = (acc_ref[t] / carry[t][1]).T.astype(o_ref.dtype)


def _mla(q, k, vt, *, batch, seq, tq=2048, ts=512, tk=512):
    n_q = seq // tq
    return pl.pallas_call(
        functools.partial(_mla_kernel, seq // tk, tk),
        grid=(batch, MLA_HEADS, n_q),
        in_specs=[
            pl.BlockSpec((tq, MLA_SLAB), lambda b, h, i: (b * n_q + i, h)),
            pl.BlockSpec((seq, MLA_SLAB), lambda b, h, i: (b, h)),
            pl.BlockSpec((V_DIM, seq), lambda b, h, i: (h, b)),
        ],
        out_specs=pl.BlockSpec((tq, V_DIM), lambda b, h, i: (b * n_q + i, h)),
        out_shape=jax.ShapeDtypeStruct((batch * seq, MLA_HEADS * V_DIM), BF16),
        scratch_shapes=[pltpu.VMEM((tq // ts, V_DIM, ts), F32), pltpu.VMEM((2, tq // ts, tk, ts), F32),
                        pltpu.VMEM((2, tq // ts, 1, ts), F32)],
        compiler_params=_params(("parallel", "parallel", "arbitrary")),
        name="mla",
    )(q, k, vt)


def _dilated_kernel(d, half, stream_len, q_ref, kp_ref, kc_ref, kn_ref, vp_ref, vc_ref, vn_ref,
                    o_ref, lse_ref, ks_ref, vs_ref):
    n_heads = q_ref.shape[0]
    tl = DIL_ROWS // d
    i = pl.program_id(2)
    for dst, before, cur, after in ((ks_ref, kp_ref, kc_ref, kn_ref), (vs_ref, vp_ref, vc_ref, vn_ref)):
        dst[:, :, 0:half] = before[...]
        dst[:, :, half:half + tl] = cur[...]
        dst[:, :, half + tl:] = after[...]

    chunk = 2 * half
    row = lax.broadcasted_iota(jnp.int32, (chunk, 2 * chunk), 0)
    col = lax.broadcasted_iota(jnp.int32, (chunk, 2 * chunk), 1)
    band = (col >= row) & (col <= row + 2 * half)
    shift = d.bit_length() - 1

    def unit_group(ug, carry):
        jobs = []
        for uu in range(DIL_UNITS_PER_TRIP):
            u = ug * DIL_UNITS_PER_TRIP + uu
            r = u & (d - 1)
            c = u >> shift
            row0 = pl.multiple_of(c * chunk, chunk)
            k_idx = i * tl + c * chunk - half + col
            mask = band & (k_idx >= 0) & (k_idx < stream_len)
            for h in range(n_heads):
                q = q_ref[h, r, pl.ds(row0, chunk), :]
                k = ks_ref[h, r, pl.ds(row0, 2 * chunk), :]
                s = lax.dot_general(q, k, _NT, preferred_element_type=F32) * DIL_SCALE
                jobs.append((h, r, row0, mask, s))
        for h, r, row0, mask, s in jobs:
            s = jnp.where(mask, s, NEG_INF)
            m = jnp.max(s, axis=1, keepdims=True)
            p = jnp.exp(s - m)
            denom = jnp.sum(p, axis=1, keepdims=True)
            pn = (p / denom).astype(BF16)
            o = jnp.dot(pn, vs_ref[h, r, pl.ds(row0, 2 * chunk), :], preferred_element_type=F32)
            start = r + d * row0
            o_ref[h, pl.ds(start, chunk, stride=d), :] = o
            lse_ref[h, pl.ds(start, chunk, stride=d), :] = jnp.broadcast_to(m + jnp.log(denom), (chunk, LANES))
        return carry

    lax.fori_loop(0, DIL_ROWS // chunk // DIL_UNITS_PER_TRIP, unit_group, 0)


def _dilated_group(qkv, *, group, window, dilation, batch, seq):
    d = dilation
    half = window // (2 * d)
    n_r = seq // DIL_ROWS
    tl = DIL_ROWS // d
    halo_per_step = tl // half
    n_halo = seq // d // half
    steps_per_group = HEADS_PER_GROUP // DIL_HEADS_PER_STEP
    blk = (DIL_HEADS_PER_STEP, None, d, tl, LANES)
    hblk = (DIL_HEADS_PER_STEP, None, d, half, LANES)

    def specs(kind):
        heads = lambda g: kind * steps_per_group + g
        cur = pl.BlockSpec(blk, lambda g, b, i: (heads(g), b, 0, i, 0))
        before = pl.BlockSpec(hblk, lambda g, b, i: (heads(g), b, 0, jnp.maximum(i * halo_per_step - 1, 0), 0))
        after = pl.BlockSpec(hblk, lambda g, b, i: (heads(g), b, 0,
                                                    jnp.minimum((i + 1) * halo_per_step, n_halo - 1), 0))
        return before, cur, after

    out_spec = pl.BlockSpec((DIL_HEADS_PER_STEP, DIL_ROWS, LANES), lambda g, b, i: (g, b * n_r + i, 0))
    out_shape = jax.ShapeDtypeStruct((HEADS_PER_GROUP, batch * seq, LANES), F32)
    return pl.pallas_call(
        functools.partial(_dilated_kernel, d, half, seq // d),
        grid=(steps_per_group, batch, n_r),
        in_specs=[specs(0)[1], *specs(1), *specs(2)],
        out_specs=[out_spec, out_spec],
        out_shape=[out_shape, out_shape],
        scratch_shapes=[pltpu.VMEM((DIL_HEADS_PER_STEP, d, tl + 2 * half, LANES), BF16),
                        pltpu.VMEM((DIL_HEADS_PER_STEP, d, tl + 2 * half, LANES), BF16)],
        compiler_params=_params(("parallel", "parallel", "parallel")),
        name="dilated_g%d" % group,
    )(*([qkv] * 7))


def _merge_kernel(oa_ref, o0_ref, o1_ref, o2_ref, l0_ref, l1_ref, l2_ref, ga_ref, gb_ref, y1_ref,
                  wa_ref, wb_ref, wo_ref, g_ref, b_ref, y_ref):
    heads = []
    for h in range(HEADS_PER_GROUP):
        l0, l1, l2 = l0_ref[h], l1_ref[h], l2_ref[h]
        m = jnp.maximum(jnp.maximum(l0, l1), l2)
        e0, e1, e2 = jnp.exp(l0 - m), jnp.exp(l1 - m), jnp.exp(l2 - m)
        denom = e0 + e1 + e2
        heads.append(((e0 / denom) * o0_ref[h] + (e1 / denom) * o1_ref[h] + (e2 / denom) * o2_ref[h]).astype(BF16))
    out_a = oa_ref[...]
    out_b = jnp.concatenate(heads, axis=1)
    cols = [slice(c, c + PROJ_SUB) for c in range(0, D_MODEL, PROJ_SUB)]
    branches = [(jnp.dot(out_a, wa_ref[:, sl], preferred_element_type=F32),
                 jnp.dot(out_b, wb_ref[:, sl], preferred_element_type=F32)) for sl in cols]
    mix = None
    for sl, (branch_a, branch_b) in zip(cols, branches):
        merged = (ga_ref[:, sl] * branch_a + gb_ref[:, sl] * branch_b).astype(BF16)
        part = jnp.dot(merged, wo_ref[sl, :], preferred_element_type=F32)
        mix = part if mix is None else mix + part
    y_ref[...] = _layer_norm(ALPHA * y1_ref[...] + mix, g_ref[...], b_ref[...])


def _merge(out_a, outs, lses, gates, y1, wa, wb, wo, g, b, *, tm=256):
    rows = y1.shape[0]
    row_spec = lambda w, c=0: pl.BlockSpec((tm, w), lambda i: (i, c))
    full = lambda a: pl.BlockSpec(a.shape, lambda i: (0, 0), pipeline_mode=pl.Buffered(1))
    return pl.pallas_call(
        _merge_kernel,
        grid=(rows // tm,),
        in_specs=[row_spec(MLA_HEADS * V_DIM)]
                 + [pl.BlockSpec((HEADS_PER_GROUP, tm, LANES), lambda i: (0, i, 0))] * 6
                 + [row_spec(D_MODEL, 0), row_spec(D_MODEL, 1), row_spec(D_MODEL)]
                 + [full(wa), full(wb), full(wo), full(g), full(b)],
        out_specs=row_spec(D_MODEL),
        out_shape=jax.ShapeDtypeStruct((rows, D_MODEL), F32),
        compiler_params=_params(("parallel",)),
        name="merge",
    )(out_a, *outs, *lses, gates, gates, y1, wa, wb, wo, g, b)


def _ffn_in_kernel(w_ref, wg_ref, wu_ref):
    w = w_ref[...]
    pad = jnp.zeros((w.shape[0], D_FF_PAD - D_FF), BF16)
    for dst, part in ((wg_ref, w[:, :D_FF]), (wu_ref, w[:, D_FF:])):
        dst[:, :D_FF] = part.astype(BF16)
        dst[:, D_FF:] = pad


def _ffn_out_kernel(n_valid, w_ref, o_ref):
    @pl.when(pl.program_id(0) < n_valid)
    def _():
        o_ref[...] = w_ref[...].astype(BF16)

    @pl.when(pl.program_id(0) >= n_valid)
    def _():
        o_ref[...] = jnp.zeros_like(o_ref)


def _ffn_weights(w_in, w_out, *, tr=64, tc=LANES):
    half = jax.ShapeDtypeStruct((D_MODEL, D_FF_PAD), BF16)
    wg, wu = pl.pallas_call(
        _ffn_in_kernel,
        grid=(D_MODEL // tr,),
        in_specs=[pl.BlockSpec((tr, 2 * D_FF), lambda i: (i, 0))],
        out_specs=[pl.BlockSpec((tr, D_FF_PAD), lambda i: (i, 0))] * 2,
        out_shape=[half, half],
        compiler_params=_params(("parallel",)),
        name="ffn_w_in",
    )(w_in)
    n_valid = D_FF // tc
    wo = pl.pallas_call(
        functools.partial(_ffn_out_kernel, n_valid),
        grid=(D_FF_PAD // tc,),
        in_specs=[pl.BlockSpec((tc, D_MODEL), lambda i: (jnp.minimum(i, n_valid - 1), 0))],
        out_specs=pl.BlockSpec((tc, D_MODEL), lambda i: (i, 0)),
        out_shape=jax.ShapeDtypeStruct((D_FF_PAD, D_MODEL), BF16),
        compiler_params=_params(("parallel",)),
        name="ffn_w_out",
    )(w_out)
    return wg, wu, wo


def _rope_tables(seq):
    pos = jnp.arange(seq, dtype=F32)[:, None]

    def angles(d):
        inv_freq = ROPE_THETA ** (-jnp.arange(0, d, 2, dtype=F32) / d)
        ang = pos * inv_freq[None, :]
        return jnp.cos(ang), jnp.sin(ang)

    cos, sin = angles(DIL_HEAD_DIM)
    dil = (jnp.concatenate([cos, cos], axis=1), jnp.concatenate([-sin, sin], axis=1))
    cos, sin = angles(QK_ROPE)
    zeros = jnp.zeros_like(cos)
    pad = jnp.zeros((seq, LANES - QK_ROPE), F32)
    mla = (jnp.concatenate([cos, cos, pad], axis=1),
           jnp.concatenate([-sin, zeros, pad], axis=1),
           jnp.concatenate([zeros, sin, pad], axis=1))
    return dil, mla


def _prepare(ffn1_w_in, ffn1_w_out, ln1_g, ln1_b, w_in_mix, b_gate, q_norm_g, w_uq, kv_norm_g, w_ukv,
             w_branch_a, w_branch_b, w_out_mix, ln2_g, ln2_b, ffn2_w_in, ffn2_w_out, ln3_g, ln3_b):
    ffn = _ffn_weights
    row = lambda a: a.reshape(1, -1)
    o_lat = Q_LORA + KV_LORA + QK_ROPE
    o_qk = o_lat + 2 * DIL_WIDTH
    o_v = o_qk + DIL_WIDTH
    w_lat = jnp.pad(w_in_mix[:, :o_lat], ((0, 0), (0, LAT_WIDTH - o_lat))).astype(BF16)
    uq = w_uq.reshape(Q_LORA, MLA_HEADS, QK_NOPE + QK_ROPE)
    ukv = w_ukv.reshape(KV_LORA, MLA_HEADS, QK_NOPE + V_DIM)
    flat = lambda a: a.reshape(a.shape[0], -1).astype(BF16)
    return dict(
        ffn1=ffn(ffn1_w_in, ffn1_w_out), ln1=(row(ln1_g), row(ln1_b)),
        ffn2=ffn(ffn2_w_in, ffn2_w_out), ln3=(row(ln3_g), row(ln3_b)),
        w_lat=w_lat,
        w_dil=w_in_mix[:, o_lat:o_v].astype(BF16),
        w_gate=w_in_mix[:, o_v:].astype(BF16),
        b_gate=row(b_gate),
        q_norm_g=row(q_norm_g), kv_norm_g=row(kv_norm_g),
        wqn=flat(uq[:, :, :QK_NOPE]),
        wqp=flat(jnp.pad(uq[:, :, QK_NOPE:], ((0, 0), (0, 0), (0, LANES - QK_ROPE)))),
        wkn=flat(ukv[:, :, :QK_NOPE]),
        wvt=flat(ukv[:, :, QK_NOPE:]).T,
        wa=w_branch_a.astype(BF16), wb=w_branch_b.astype(BF16), wo=w_out_mix.astype(BF16),
        ln2=(row(ln2_g), row(ln2_b)),
    )


def _encoder_layer(x, p):
    batch, seq, _ = x.shape
    x2 = x.reshape(batch * seq, D_MODEL)
    dil_tables, mla_tables = _rope_tables(seq)

    y1, y1_bf = _ffn_ln(x2, *p["ffn1"], *p["ln1"], emit_bf16=True)

    lat = _proj(y1_bf, p["w_lat"], F32, tn=LAT_WIDTH)
    qkv_d = _proj(y1_bf, p["w_dil"], BF16, mode="dilated", extras=dil_tables, seq=seq, tn=DIL_WIDTH)
    gates = _proj(y1_bf, p["w_gate"], F32, mode="gate", extras=(p["b_gate"],), tn=D_MODEL)

    q, k, vt = _latent(lat, p["q_norm_g"], p["kv_norm_g"], p["wqn"], p["wqp"], p["wkn"], p["wvt"],
                       mla_tables, seq=seq)
    out_a = _mla(q, k, vt, batch=batch, seq=seq)

    outs, lses = [], []
    for group, (window, dilation) in enumerate(DIL_PATTERNS):
        o, lse = _dilated_group(qkv_d[group], group=group, window=window, dilation=dilation, batch=batch, seq=seq)
        outs.append(o)
        lses.append(lse)

    y2 = _merge(out_a, outs, lses, gates, y1, p["wa"], p["wb"], p["wo"], *p["ln2"])
    (y3,) = _ffn_ln(y2, *p["ffn2"], *p["ln3"], emit_bf16=False)
    return y3.reshape(batch, seq, D_MODEL)


def kernel(x_prompt, x_sample, ffn1_w_in, ffn1_w_out, ln1_g, ln1_b, w_in_mix, b_gate, q_norm_g, w_uq, kv_norm_g, w_ukv, w_branch_a, w_branch_b, w_out_mix, ln2_g, ln2_b, ffn2_w_in, ffn2_w_out, ln3_g, ln3_b):
    weights = (ffn1_w_in, ffn1_w_out, ln1_g, ln1_b, w_in_mix, b_gate, q_norm_g, w_uq, kv_norm_g, w_ukv,
               w_branch_a, w_branch_b, w_out_mix, ln2_g, ln2_b, ffn2_w_in, ffn2_w_out, ln3_g, ln3_b)
    y_prompt, y_sample = x_prompt, x_sample
    for layer in range(DEPTH):
        p = _prepare(*(w[layer] for w in weights))
        y_prompt = _encoder_layer(y_prompt, p)
        y_sample = _encoder_layer(y_sample, p)
    return (y_prompt, y_sample)
```

```python
import functools

import jax
import jax.numpy as jnp
from jax import lax
from jax.experimental import pallas as pl
from jax.experimental.pallas import tpu as pltpu

D_MODEL = 2048
DEPTH = 1
MLA_HEADS = 8
Q_LORA = 512
KV_LORA = 512
QK_NOPE = 128
QK_ROPE = 64
V_DIM = 128
DIL_PATTERNS = ((128, 1), (512, 4), (2048, 16))
HEADS_PER_GROUP = 4
DIL_HEAD_DIM = 128
DIL_WIDTH = len(DIL_PATTERNS) * HEADS_PER_GROUP * DIL_HEAD_DIM
D_FF = 5504
ROPE_THETA = 10000.0
LN_EPS = 1e-5
RMS_EPS = 1e-6
NEG_INF = -1e30
ALPHA = (2 * DEPTH) ** 0.25
MLA_SCALE = (QK_NOPE + QK_ROPE) ** -0.5
LOG2_E = 1.4426950408889634
MLA_Q_SCALE = MLA_SCALE * LOG2_E
DIL_KINDS = 3
DIL_SCALE = DIL_HEAD_DIM ** -0.5

LANES = 128
D_FF_PAD = 5632
FF_CHUNK = 512
LAT_WIDTH = Q_LORA + KV_LORA + LANES
MLA_SLAB = 2 * LANES
GROUP_WIDTH = HEADS_PER_GROUP * DIL_HEAD_DIM
PROJ_SUB = 512
DIL_ROWS = 2048
DIL_HEADS_PER_STEP = 4
DIL_UNITS_PER_TRIP = 2
VMEM_LIMIT = 56 * 1024 * 1024

BF16 = jnp.bfloat16
F32 = jnp.float32
_NT = (((1,), (1,)), ((), ()))


def _params(semantics):
    return pltpu.CompilerParams(dimension_semantics=semantics, vmem_limit_bytes=VMEM_LIMIT)


def _layer_norm(v, g, b, scale=1.0):
    mu = jnp.mean(v, axis=-1, keepdims=True)
    c = v - mu
    var = jnp.mean(c * c, axis=-1, keepdims=True)
    return c * (scale * lax.rsqrt(scale * scale * var + LN_EPS)) * g + b


def _ffn_ln_kernel(n_chunks, emit_bf16, x_ref, wg_ref, wu_ref, wo_ref, g_ref, b_ref, *rest):
    if emit_bf16:
        y_ref, ybf_ref, xbf_ref, acc_ref = rest
    else:
        y_ref, xbf_ref, acc_ref = rest
    k = pl.program_id(1)

    @pl.when(k == 0)
    def _():
        x = x_ref[...]
        xbf_ref[...] = x.astype(BF16)
        acc_ref[...] = (2.0 * ALPHA) * x

    xb = xbf_ref[...]
    gate = jnp.dot(xb, wg_ref[...], preferred_element_type=F32)
    up = jnp.dot(xb, wu_ref[...], preferred_element_type=F32)
    act = gate * jax.nn.sigmoid(gate) * up
    acc_ref[...] += jnp.dot(act.astype(BF16), wo_ref[...], preferred_element_type=F32)

    @pl.when(k == n_chunks - 1)
    def _():
        y = _layer_norm(acc_ref[...], g_ref[...], b_ref[...], scale=0.5)
        y_ref[...] = y
        if emit_bf16:
            ybf_ref[...] = y.astype(BF16)


def _ffn_ln(x, wg, wu, wo, g, b, *, emit_bf16, tm=512):
    rows = x.shape[0]
    n_chunks = D_FF_PAD // FF_CHUNK
    out_shape = [jax.ShapeDtypeStruct((rows, D_MODEL), F32)]
    out_specs = [pl.BlockSpec((tm, D_MODEL), lambda i, k: (i, 0))]
    if emit_bf16:
        out_shape.append(jax.ShapeDtypeStruct((rows, D_MODEL), BF16))
        out_specs.append(pl.BlockSpec((tm, D_MODEL), lambda i, k: (i, 0)))
    return pl.pallas_call(
        functools.partial(_ffn_ln_kernel, n_chunks, emit_bf16),
        grid=(rows // tm, n_chunks),
        in_specs=[
            pl.BlockSpec((tm, D_MODEL), lambda i, k: (i, 0)),
            pl.BlockSpec((D_MODEL, FF_CHUNK), lambda i, k: (0, k)),
            pl.BlockSpec((D_MODEL, FF_CHUNK), lambda i, k: (0, k)),
            pl.BlockSpec((FF_CHUNK, D_MODEL), lambda i, k: (k, 0)),
            pl.BlockSpec((1, D_MODEL), lambda i, k: (0, 0)),
            pl.BlockSpec((1, D_MODEL), lambda i, k: (0, 0)),
        ],
        out_specs=out_specs,
        out_shape=out_shape,
        scratch_shapes=[pltpu.VMEM((tm, D_MODEL), BF16), pltpu.VMEM((tm, D_MODEL), F32)],
        compiler_params=_params(("parallel", "arbitrary")),
        name="ffn_ln",
    )(x, wg, wu, wo, g, b)


def _proj_kernel(mode, x_ref, w_ref, *rest):
    tn = w_ref.shape[1]
    starts = range(0, tn, PROJ_SUB)

    def products():
        x = x_ref[...]
        return [(c, jnp.dot(x, w_ref[:, c:min(c + PROJ_SUB, tn)], preferred_element_type=F32)) for c in starts]

    if mode == "dilated":
        cos_ref, sin_ref, *o_refs, slab_ref = rest
        n_rope_steps = 2 * DIL_WIDTH // tn
        tm = x_ref.shape[0]

        def emit(rope):
            if rope:
                cos = cos_ref[...]
                sin = sin_ref[...]
            for c, z in products():
                group = c // PROJ_SUB
                d = DIL_PATTERNS[group][1]
                o_ref = o_refs[group]
                for h in range(HEADS_PER_GROUP):
                    zh = z[:, h * LANES:(h + 1) * LANES]
                    if rope:
                        zh = zh * cos + pltpu.roll(zh, LANES // 2, 1) * sin
                    if d == 1:
                        o_ref[h, 0] = zh.astype(BF16)
                        continue
                    slab = group * HEADS_PER_GROUP + h
                    slab_ref[slab] = zh
                    for r in range(d):
                        o_ref[h, r] = slab_ref[slab, pl.ds(r, tm // d, stride=d), :].astype(BF16)

        pl.when(pl.program_id(1) < n_rope_steps)(functools.partial(emit, True))
        pl.when(pl.program_id(1) >= n_rope_steps)(functools.partial(emit, False))
    elif mode == "gate":
        b_ref, o_ref = rest
        for c, z in products():
            sl = slice(c, c + z.shape[1])
            o_ref[:, sl] = jax.nn.sigmoid(z + b_ref[:, sl]).astype(o_ref.dtype)
    else:
        (o_ref,) = rest
        for c, z in products():
            o_ref[:, c:c + z.shape[1]] = z.astype(o_ref.dtype)


def _proj(x, w, out_dtype, *, mode="plain", extras=(), seq=None, tm=1024, tn=512):
    rows = x.shape[0]
    n = w.shape[1]
    tn = min(tn, n)
    in_specs = [
        pl.BlockSpec((tm, D_MODEL), lambda i, j: (i, 0)),
        pl.BlockSpec((D_MODEL, tn), lambda i, j: (0, j)),
    ]
    out_specs = pl.BlockSpec((tm, tn), lambda i, j: (i, j))
    out_shape = jax.ShapeDtypeStruct((rows, n), out_dtype)
    scratch_shapes = []
    if mode == "dilated":
        assert tn == DIL_WIDTH and PROJ_SUB == GROUP_WIDTH
        tiles_per_seq = seq // tm
        in_specs += [pl.BlockSpec((tm, LANES), lambda i, j: (i % tiles_per_seq, 0))] * 2
        out_specs = [
            pl.BlockSpec((HEADS_PER_GROUP, None, d, tm // d, LANES),
                         lambda i, j: (j, i // tiles_per_seq, 0, i % tiles_per_seq, 0))
            for _, d in DIL_PATTERNS]
        out_shape = [jax.ShapeDtypeStruct((DIL_KINDS * HEADS_PER_GROUP, rows // seq, d, seq // d, LANES), out_dtype)
                     for _, d in DIL_PATTERNS]
        scratch_shapes = [pltpu.VMEM((len(DIL_PATTERNS) * HEADS_PER_GROUP, tm, LANES), F32)]
    elif mode == "gate":
        in_specs += [pl.BlockSpec((1, tn), lambda i, j: (0, j))]
    return pl.pallas_call(
        functools.partial(_proj_kernel, mode),
        grid=(rows // tm, n // tn),
        in_specs=in_specs,
        out_specs=out_specs,
        out_shape=out_shape,
        scratch_shapes=scratch_shapes,
        compiler_params=_params(("parallel", "arbitrary")),
        name="proj_" + mode,
    )(x, w, *extras)


def _latent_kernel(lat_ref, qg_ref, kvg_ref, wqn_ref, wqp_ref, wkn_ref, wvt_ref,
                   cos_ref, sin_lo_ref, sin_hi_ref, q_ref, k_ref, vt_ref):
    lat = lat_ref[...]
    c_q = lat[:, :Q_LORA]
    c_kv = lat[:, Q_LORA:Q_LORA + KV_LORA]
    k_rope = lat[:, Q_LORA + KV_LORA:]

    def rms(x, g):
        return (x * lax.rsqrt(jnp.mean(x * x, axis=-1, keepdims=True) + RMS_EPS) * g).astype(BF16)

    cos = cos_ref[...]
    sin_lo = sin_lo_ref[...]
    sin_hi = sin_hi_ref[...]

    def rope64(x):
        quarter = QK_ROPE // 2
        return (x * cos + pltpu.roll(x, LANES - quarter, 1) * sin_lo
                + pltpu.roll(x, quarter, 1) * sin_hi)

    nq = rms(c_q, qg_ref[...])
    nkv = rms(c_kv, kvg_ref[...])
    q_nope = jnp.dot(nq, wqn_ref[...], preferred_element_type=F32)
    q_pe = jnp.dot(nq, wqp_ref[...], preferred_element_type=F32)
    k_nope = jnp.dot(nkv, wkn_ref[...], preferred_element_type=F32)
    vt = lax.dot_general(wvt_ref[...], nkv, _NT, preferred_element_type=F32)
    k_pe = rope64(k_rope).astype(BF16)
    for h in range(MLA_HEADS):
        src = slice(h * LANES, (h + 1) * LANES)
        lo = slice(h * MLA_SLAB, h * MLA_SLAB + LANES)
        hi = slice(h * MLA_SLAB + LANES, (h + 1) * MLA_SLAB)
        q_ref[:, lo] = (q_nope[:, src] * MLA_Q_SCALE).astype(BF16)
        q_ref[:, hi] = (rope64(q_pe[:, src]) * MLA_Q_SCALE).astype(BF16)
        k_ref[:, lo] = k_nope[:, src].astype(BF16)
        k_ref[:, hi] = k_pe
    vt_ref[...] = vt.astype(BF16)


def _latent(lat, qg, kvg, wqn, wqp, wkn, wvt, tables, *, seq, tm=512):
    rows = lat.shape[0]
    tiles_per_seq = seq // tm
    width = MLA_HEADS * LANES
    row_spec = lambda w: pl.BlockSpec((tm, w), lambda i: (i, 0))
    full = lambda a: pl.BlockSpec(a.shape, lambda i: (0, 0), pipeline_mode=pl.Buffered(1))
    table_spec = pl.BlockSpec((tm, LANES), lambda i: (i % tiles_per_seq, 0))
    return pl.pallas_call(
        _latent_kernel,
        grid=(rows // tm,),
        in_specs=[row_spec(LAT_WIDTH), full(qg), full(kvg), full(wqn), full(wqp), full(wkn), full(wvt),
                  table_spec, table_spec, table_spec],
        out_specs=[row_spec(MLA_HEADS * MLA_SLAB), row_spec(MLA_HEADS * MLA_SLAB),
                   pl.BlockSpec((width, tm), lambda i: (0, i))],
        out_shape=[jax.ShapeDtypeStruct((rows, MLA_HEADS * MLA_SLAB), BF16),
                   jax.ShapeDtypeStruct((rows, MLA_HEADS * MLA_SLAB), BF16),
                   jax.ShapeDtypeStruct((width, rows), BF16)],
        compiler_params=_params(("parallel",)),
        name="latent",
    )(lat, qg, kvg, wqn, wqp, wkn, wvt, *tables)


def _mla_kernel(n_kv, tk, q_ref, k_ref, vt_ref, o_ref, acc_ref, s_ref, mblk_ref):
    n_sub, _, ts = acc_ref.shape
    acc_ref[...] = jnp.zeros_like(acc_ref)

    def scores_into(slot, j):
        k = k_ref[pl.ds(pl.multiple_of(j * tk, tk), tk), :]
        for t in range(n_sub):
            s = lax.dot_general(k, q_ref[pl.ds(t * ts, ts), :], _NT, preferred_element_type=F32)
            s_ref[slot, t] = s
            mblk_ref[slot, t] = jnp.max(s, axis=0, keepdims=True)

    def consume(slot, j, carry):
        vt = vt_ref[:, pl.ds(pl.multiple_of(j * tk, tk), tk)]
        out = []
        for t in range(n_sub):
            m_prev, l_prev = carry[t]
            m_new = jnp.maximum(m_prev, mblk_ref[slot, t])
            alpha = jnp.exp2(m_prev - m_new)
            p = jnp.exp2(s_ref[slot, t] - m_new)
            l_new = alpha * l_prev + jnp.sum(p, axis=0, keepdims=True)
            acc_ref[t] = alpha * acc_ref[t] + jnp.dot(vt, p.astype(BF16), preferred_element_type=F32)
            out.append((m_new, l_new))
        return tuple(out)

    def body(jj, carry):
        j = 2 * jj
        scores_into(1, j + 1)
        carry = consume(0, j, carry)
        scores_into(0, j + 2)
        return consume(1, j + 1, carry)

    scores_into(0, 0)
    init = tuple((jnp.full((1, ts), -jnp.inf, F32), jnp.zeros((1, ts), F32)) for _ in range(n_sub))
    carry = lax.fori_loop(0, n_kv // 2 - 1, body, init)
    scores_into(1, n_kv - 1)
    carry = consume(0, n_kv - 2, carry)
    carry = consume(1, n_kv - 1, carry)
    for t in range(n_sub):
        o_ref[pl.ds(t * ts, ts), :] = (acc_ref[t] / carry[t][1]).T.astype(o_ref.dtype)


def _mla(q, k, vt, *, batch, seq, tq=2048, ts=512, tk=1024):
    n_q = seq // tq
    return pl.pallas_call(
        functools.partial(_mla_kernel, seq // tk, tk),
        grid=(batch, MLA_HEADS, n_q),
        in_specs=[
            pl.BlockSpec((tq, MLA_SLAB), lambda b, h, i: (b * n_q + i, h)),
            pl.BlockSpec((seq, MLA_SLAB), lambda b, h, i: (b, h)),
            pl.BlockSpec((V_DIM, seq), lambda b, h, i: (h, b)),
        ],
        out_specs=pl.BlockSpec((tq, V_DIM), lambda b, h, i: (b * n_q + i, h)),
        out_shape=jax.ShapeDtypeStruct((batch * seq, MLA_HEADS * V_DIM), BF16),
        scratch_shapes=[pltpu.VMEM((tq // ts, V_DIM, ts), F32), pltpu.VMEM((2, tq // ts, tk, ts), F32),
                        pltpu.VMEM((2, tq // ts, 1, ts), F32)],
        compiler_params=_params(("parallel", "parallel", "arbitrary")),
        name="mla",
    )(q, k, vt)


def _dilated_kernel(d, half, stream_len, q_ref, kp_ref, kc_ref, kn_ref, vp_ref, vc_ref, vn_ref,
                    o_ref, lse_ref, ks_ref, vs_ref):
    n_heads = q_ref.shape[0]
    tl = DIL_ROWS // d
    i = pl.program_id(2)
    for dst, before, cur, after in ((ks_ref, kp_ref, kc_ref, kn_ref), (vs_ref, vp_ref, vc_ref, vn_ref)):
        dst[:, :, 0:half] = before[...]
        dst[:, :, half:half + tl] = cur[...]
        dst[:, :, half + tl:] = after[...]

    chunk = 2 * half
    row = lax.broadcasted_iota(jnp.int32, (chunk, 2 * chunk), 0)
    col = lax.broadcasted_iota(jnp.int32, (chunk, 2 * chunk), 1)
    band = (col >= row) & (col <= row + 2 * half)
    shift = d.bit_length() - 1

    def unit_group(ug, carry):
        jobs = []
        for uu in range(DIL_UNITS_PER_TRIP):
            u = ug * DIL_UNITS_PER_TRIP + uu
            r = u & (d - 1)
            c = u >> shift
            row0 = pl.multiple_of(c * chunk, chunk)
            k_idx = i * tl + c * chunk - half + col
            mask = band & (k_idx >= 0) & (k_idx < stream_len)
            for h in range(n_heads):
                q = q_ref[h, r, pl.ds(row0, chunk), :]
                k = ks_ref[h, r, pl.ds(row0, 2 * chunk), :]
                s = lax.dot_general(q, k, _NT, preferred_element_type=F32) * DIL_SCALE
                jobs.append((h, r, row0, mask, s))
        for h, r, row0, mask, s in jobs:
            s = jnp.where(mask, s, NEG_INF)
            m = jnp.max(s, axis=1, keepdims=True)
            p = jnp.exp(s - m)
            denom = jnp.sum(p, axis=1, keepdims=True)
            pn = (p / denom).astype(BF16)
            o = jnp.dot(pn, vs_ref[h, r, pl.ds(row0, 2 * chunk), :], preferred_element_type=F32)
            start = r + d * row0
            o_ref[h, pl.ds(start, chunk, stride=d), :] = o
            lse_ref[h, pl.ds(start, chunk, stride=d), :] = jnp.broadcast_to(m + jnp.log(denom), (chunk, LANES))
        return carry

    lax.fori_loop(0, DIL_ROWS // chunk // DIL_UNITS_PER_TRIP, unit_group, 0)


def _dilated_group(qkv, *, group, window, dilation, batch, seq):
    d = dilation
    half = window // (2 * d)
    n_r = seq // DIL_ROWS
    tl = DIL_ROWS // d
    halo_per_step = tl // half
    n_halo = seq // d // half
    steps_per_group = HEADS_PER_GROUP // DIL_HEADS_PER_STEP
    blk = (DIL_HEADS_PER_STEP, None, d, tl, LANES)
    hblk = (DIL_HEADS_PER_STEP, None, d, half, LANES)

    def specs(kind):
        heads = lambda g: kind * steps_per_group + g
        cur = pl.BlockSpec(blk, lambda g, b, i: (heads(g), b, 0, i, 0))
        before = pl.BlockSpec(hblk, lambda g, b, i: (heads(g), b, 0, jnp.maximum(i * halo_per_step - 1, 0), 0))
        after = pl.BlockSpec(hblk, lambda g, b, i: (heads(g), b, 0,
                                                    jnp.minimum((i + 1) * halo_per_step, n_halo - 1), 0))
        return before, cur, after

    out_spec = pl.BlockSpec((DIL_HEADS_PER_STEP, DIL_ROWS, LANES), lambda g, b, i: (g, b * n_r + i, 0))
    out_shape = jax.ShapeDtypeStruct((HEADS_PER_GROUP, batch * seq, LANES), F32)
    return pl.pallas_call(
        functools.partial(_dilated_kernel, d, half, seq // d),
        grid=(steps_per_group, batch, n_r),
        in_specs=[specs(0)[1], *specs(1), *specs(2)],
        out_specs=[out_spec, out_spec],
        out_shape=[out_shape, out_shape],
        scratch_shapes=[pltpu.VMEM((DIL_HEADS_PER_STEP, d, tl + 2 * half, LANES), BF16),
                        pltpu.VMEM((DIL_HEADS_PER_STEP, d, tl + 2 * half, LANES), BF16)],
        compiler_params=_params(("parallel", "parallel", "parallel")),
        name="dilated_g%d" % group,
    )(*([qkv] * 7))


def _merge_kernel(oa_ref, o0_ref, o1_ref, o2_ref, l0_ref, l1_ref, l2_ref, ga_ref, gb_ref, y1_ref,
                  wa_ref, wb_ref, wo_ref, g_ref, b_ref, y_ref):
    heads = []
    for h in range(HEADS_PER_GROUP):
        l0, l1, l2 = l0_ref[h], l1_ref[h], l2_ref[h]
        m = jnp.maximum(jnp.maximum(l0, l1), l2)
        e0, e1, e2 = jnp.exp(l0 - m), jnp.exp(l1 - m), jnp.exp(l2 - m)
        denom = e0 + e1 + e2
        heads.append(((e0 / denom) * o0_ref[h] + (e1 / denom) * o1_ref[h] + (e2 / denom) * o2_ref[h]).astype(BF16))
    out_a = oa_ref[...]
    out_b = jnp.concatenate(heads, axis=1)
    cols = [slice(c, c + PROJ_SUB) for c in range(0, D_MODEL, PROJ_SUB)]
    branches = [(jnp.dot(out_a, wa_ref[:, sl], preferred_element_type=F32),
                 jnp.dot(out_b, wb_ref[:, sl], preferred_element_type=F32)) for sl in cols]
    mix = None
    for sl, (branch_a, branch_b) in zip(cols, branches):
        merged = (ga_ref[:, sl] * branch_a + gb_ref[:, sl] * branch_b).astype(BF16)
        part = jnp.dot(merged, wo_ref[sl, :], preferred_element_type=F32)
        mix = part if mix is None else mix + part
    y_ref[...] = _layer_norm(ALPHA * y1_ref[...] + mix, g_ref[...], b_ref[...])


def _merge(out_a, outs, lses, gates, y1, wa, wb, wo, g, b, *, tm=256):
    rows = y1.shape[0]
    row_spec = lambda w, c=0: pl.BlockSpec((tm, w), lambda i: (i, c))
    full = lambda a: pl.BlockSpec(a.shape, lambda i: (0, 0), pipeline_mode=pl.Buffered(1))
    return pl.pallas_call(
        _merge_kernel,
        grid=(rows // tm,),
        in_specs=[row_spec(MLA_HEADS * V_DIM)]
                 + [pl.BlockSpec((HEADS_PER_GROUP, tm, LANES), lambda i: (0, i, 0))] * 6
                 + [row_spec(D_MODEL, 0), row_spec(D_MODEL, 1), row_spec(D_MODEL)]
                 + [full(wa), full(wb), full(wo), full(g), full(b)],
        out_specs=row_spec(D_MODEL),
        out_shape=jax.ShapeDtypeStruct((rows, D_MODEL), F32),
        compiler_params=_params(("parallel",)),
        name="merge",
    )(out_a, *outs, *lses, gates, gates, y1, wa, wb, wo, g, b)


def _ffn_in_kernel(w_ref, wg_ref, wu_ref):
    w = w_ref[...]
    pad = jnp.zeros((w.shape[0], D_FF_PAD - D_FF), BF16)
    for dst, part in ((wg_ref, w[:, :D_FF]), (wu_ref, w[:, D_FF:])):
        dst[:, :D_FF] = part.astype(BF16)
        dst[:, D_FF:] = pad


def _ffn_out_kernel(n_valid, w_ref, o_ref):
    @pl.when(pl.program_id(0) < n_valid)
    def _():
        o_ref[...] = w_ref[...].astype(BF16)

    @pl.when(pl.program_id(0) >= n_valid)
    def _():
        o_ref[...] = jnp.zeros_like(o_ref)


def _ffn_weights(w_in, w_out, *, tr=64, tc=LANES):
    half = jax.ShapeDtypeStruct((D_MODEL, D_FF_PAD), BF16)
    wg, wu = pl.pallas_call(
        _ffn_in_kernel,
        grid=(D_MODEL // tr,),
        in_specs=[pl.BlockSpec((tr, 2 * D_FF), lambda i: (i, 0))],
        out_specs=[pl.BlockSpec((tr, D_FF_PAD), lambda i: (i, 0))] * 2,
        out_shape=[half, half],
        compiler_params=_params(("parallel",)),
        name="ffn_w_in",
    )(w_in)
    n_valid = D_FF // tc
    wo = pl.pallas_call(
        functools.partial(_ffn_out_kernel, n_valid),
        grid=(D_FF_PAD // tc,),
        in_specs=[pl.BlockSpec((tc, D_MODEL), lambda i: (jnp.minimum(i, n_valid - 1), 0))],
        out_specs=pl.BlockSpec((tc, D_MODEL), lambda i: (i, 0)),
        out_shape=jax.ShapeDtypeStruct((D_FF_PAD, D_MODEL), BF16),
        compiler_params=_params(("parallel",)),
        name="ffn_w_out",
    )(w_out)
    return wg, wu, wo


def _rope_tables(seq):
    pos = jnp.arange(seq, dtype=F32)[:, None]

    def angles(d):
        inv_freq = ROPE_THETA ** (-jnp.arange(0, d, 2, dtype=F32) / d)
        ang = pos * inv_freq[None, :]
        return jnp.cos(ang), jnp.sin(ang)

    cos, sin = angles(DIL_HEAD_DIM)
    dil = (jnp.concatenate([cos, cos], axis=1), jnp.concatenate([-sin, sin], axis=1))
    cos, sin = angles(QK_ROPE)
    zeros = jnp.zeros_like(cos)
    pad = jnp.zeros((seq, LANES - QK_ROPE), F32)
    mla = (jnp.concatenate([cos, cos, pad], axis=1),
           jnp.concatenate([-sin, zeros, pad], axis=1),
           jnp.concatenate([zeros, sin, pad], axis=1))
    return dil, mla


def _prepare(ffn1_w_in, ffn1_w_out, ln1_g, ln1_b, w_in_mix, b_gate, q_norm_g, w_uq, kv_norm_g, w_ukv,
             w_branch_a, w_branch_b, w_out_mix, ln2_g, ln2_b, ffn2_w_in, ffn2_w_out, ln3_g, ln3_b):
    ffn = _ffn_weights
    row = lambda a: a.reshape(1, -1)
    o_lat = Q_LORA + KV_LORA + QK_ROPE
    o_qk = o_lat + 2 * DIL_WIDTH
    o_v = o_qk + DIL_WIDTH
    w_lat = jnp.pad(w_in_mix[:, :o_lat], ((0, 0), (0, LAT_WIDTH - o_lat))).astype(BF16)
    uq = w_uq.reshape(Q_LORA, MLA_HEADS, QK_NOPE + QK_ROPE)
    ukv = w_ukv.reshape(KV_LORA, MLA_HEADS, QK_NOPE + V_DIM)
    flat = lambda a: a.reshape(a.shape[0], -1).astype(BF16)
    return dict(
        ffn1=ffn(ffn1_w_in, ffn1_w_out), ln1=(row(ln1_g), row(ln1_b)),
        ffn2=ffn(ffn2_w_in, ffn2_w_out), ln3=(row(ln3_g), row(ln3_b)),
        w_lat=w_lat,
        w_dil=w_in_mix[:, o_lat:o_v].astype(BF16),
        w_gate=w_in_mix[:, o_v:].astype(BF16),
        b_gate=row(b_gate),
        q_norm_g=row(q_norm_g), kv_norm_g=row(kv_norm_g),
        wqn=flat(uq[:, :, :QK_NOPE]),
        wqp=flat(jnp.pad(uq[:, :, QK_NOPE:], ((0, 0), (0, 0), (0, LANES - QK_ROPE)))),
        wkn=flat(ukv[:, :, :QK_NOPE]),
        wvt=flat(ukv[:, :, QK_NOPE:]).T,
        wa=w_branch_a.astype(BF16), wb=w_branch_b.astype(BF16), wo=w_out_mix.astype(BF16),
        ln2=(row(ln2_g), row(ln2_b)),
    )


def _encoder_layer(x, p):
    batch, seq, _ = x.shape
    x2 = x.reshape(batch * seq, D_MODEL)
    dil_tables, mla_tables = _rope_tables(seq)

    y1, y1_bf = _ffn_ln(x2, *p["ffn1"], *p["ln1"], emit_bf16=True)

    lat = _proj(y1_bf, p["w_lat"], F32, tn=LAT_WIDTH)
    qkv_d = _proj(y1_bf, p["w_dil"], BF16, mode="dilated", extras=dil_tables, seq=seq, tn=DIL_WIDTH)
    gates = _proj(y1_bf, p["w_gate"], F32, mode="gate", extras=(p["b_gate"],), tn=D_MODEL)

    q, k, vt = _latent(lat, p["q_norm_g"], p["kv_norm_g"], p["wqn"], p["wqp"], p["wkn"], p["wvt"],
                       mla_tables, seq=seq)
    out_a = _mla(q, k, vt, batch=batch, seq=seq)

    outs, lses = [], []
    for group, (window, dilation) in enumerate(DIL_PATTERNS):
        o, lse = _dilated_group(qkv_d[group], group=group, window=window, dilation=dilation, batch=batch, seq=seq)
        outs.append(o)
        lses.append(lse)

    y2 = _merge(out_a, outs, lses, gates, y1, p["wa"], p["wb"], p["wo"], *p["ln2"])
    (y3,) = _ffn_ln(y2, *p["ffn2"], *p["ln3"], emit_bf16=False)
    return y3.reshape(batch, seq, D_MODEL)


def kernel(x_prompt, x_sample, ffn1_w_in, ffn1_w_out, ln1_g, ln1_b, w_in_mix, b_gate, q_norm_g, w_uq, kv_norm_g, w_ukv, w_branch_a, w_branch_b, w_out_mix, ln2_g, ln2_b, ffn2_w_in, ffn2_w_out, ln3_g, ln3_b):
    weights = (ffn1_w_in, ffn1_w_out, ln1_g, ln1_b, w_in_mix, b_gate, q_norm_g, w_uq, kv_norm_g, w_ukv,
               w_branch_a, w_branch_b, w_out_mix, ln2_g, ln2_b, ffn2_w_in, ffn2_w_out, ln3_g, ln3_b)
    y_prompt, y_sample = x_prompt, x_sample
    for layer in range(DEPTH):
        p = _prepare(*(w[layer] for w in weights))
        y_prompt = _encoder_layer(y_prompt, p)
        y_sample = _encoder_layer(y_sample, p)
    return (y_prompt, y_sample)
```

```python
import functools

import jax
import jax.numpy as jnp
from jax import lax
from jax.experimental import pallas as pl
from jax.experimental.pallas import tpu as pltpu

D_MODEL = 2048
DEPTH = 1
MLA_HEADS = 8
Q_LORA = 512
KV_LORA = 512
QK_NOPE = 128
QK_ROPE = 64
V_DIM = 128
DIL_PATTERNS = ((128, 1), (512, 4), (2048, 16))
HEADS_PER_GROUP = 4
DIL_HEAD_DIM = 128
DIL_WIDTH = len(DIL_PATTERNS) * HEADS_PER_GROUP * DIL_HEAD_DIM
D_FF = 5504
ROPE_THETA = 10000.0
LN_EPS = 1e-5
RMS_EPS = 1e-6
NEG_INF = -1e30
ALPHA = (2 * DEPTH) ** 0.25
MLA_SCALE = (QK_NOPE + QK_ROPE) ** -0.5
LOG2_E = 1.4426950408889634
MLA_Q_SCALE = MLA_SCALE * LOG2_E
DIL_KINDS = 3
DIL_SCALE = DIL_HEAD_DIM ** -0.5

LANES = 128
D_FF_PAD = 5632
FF_CHUNK = 512
LAT_WIDTH = Q_LORA + KV_LORA + LANES
MLA_SLAB = 2 * LANES
GROUP_WIDTH = HEADS_PER_GROUP * DIL_HEAD_DIM
PROJ_SUB = 512
DIL_ROWS = 2048
DIL_HEADS_PER_STEP = 4
DIL_UNITS_PER_TRIP = 2
VMEM_LIMIT = 56 * 1024 * 1024
FFN_VMEM_LIMIT = 60 * 1024 * 1024

BF16 = jnp.bfloat16
F32 = jnp.float32
_NT = (((1,), (1,)), ((), ()))


def _params(semantics, vmem_limit=VMEM_LIMIT):
    return pltpu.CompilerParams(dimension_semantics=semantics, vmem_limit_bytes=vmem_limit)


def _layer_norm(v, g, b, scale=1.0):
    mu = jnp.mean(v, axis=-1, keepdims=True)
    c = v - mu
    var = jnp.mean(c * c, axis=-1, keepdims=True)
    return c * (scale * lax.rsqrt(scale * scale * var + LN_EPS)) * g + b


def _ffn_ln_kernel(n_chunks, emit_bf16, x_ref, wg_ref, wu_ref, wo_ref, g_ref, b_ref, *rest):
    if emit_bf16:
        y_ref, ybf_ref, xbf_ref = rest
    else:
        y_ref, xbf_ref = rest
    k = pl.program_id(1)

    @pl.when(k == 0)
    def _():
        x = x_ref[...]
        xbf_ref[...] = x.astype(BF16)
        y_ref[...] = (2.0 * ALPHA) * x

    xb = xbf_ref[...]
    gate = jnp.dot(xb, wg_ref[...], preferred_element_type=F32)
    up = jnp.dot(xb, wu_ref[...], preferred_element_type=F32)
    act = gate * jax.nn.sigmoid(gate) * up
    y_ref[...] += jnp.dot(act.astype(BF16), wo_ref[...], preferred_element_type=F32)

    @pl.when(k == n_chunks - 1)
    def _():
        y = _layer_norm(y_ref[...], g_ref[...], b_ref[...], scale=0.5)
        y_ref[...] = y
        if emit_bf16:
            ybf_ref[...] = y.astype(BF16)


def _ffn_ln(x, wg, wu, wo, g, b, *, emit_bf16, tm=1024):
    rows = x.shape[0]
    n_chunks = D_FF_PAD // FF_CHUNK
    x_mode = dict(pipeline_mode=pl.Buffered(1)) if emit_bf16 else {}
    out_shape = [jax.ShapeDtypeStruct((rows, D_MODEL), F32)]
    out_specs = [pl.BlockSpec((tm, D_MODEL), lambda i, k: (i, 0))]
    if emit_bf16:
        out_shape.append(jax.ShapeDtypeStruct((rows, D_MODEL), BF16))
        out_specs.append(pl.BlockSpec((tm, D_MODEL), lambda i, k: (i, 0)))
    return pl.pallas_call(
        functools.partial(_ffn_ln_kernel, n_chunks, emit_bf16),
        grid=(rows // tm, n_chunks),
        in_specs=[
            pl.BlockSpec((tm, D_MODEL), lambda i, k: (i, 0), **x_mode),
            pl.BlockSpec((D_MODEL, FF_CHUNK), lambda i, k: (0, k)),
            pl.BlockSpec((D_MODEL, FF_CHUNK), lambda i, k: (0, k)),
            pl.BlockSpec((FF_CHUNK, D_MODEL), lambda i, k: (k, 0)),
            pl.BlockSpec((1, D_MODEL), lambda i, k: (0, 0)),
            pl.BlockSpec((1, D_MODEL), lambda i, k: (0, 0)),
        ],
        out_specs=out_specs,
        out_shape=out_shape,
        scratch_shapes=[pltpu.VMEM((tm, D_MODEL), BF16)],
        compiler_params=_params(("parallel", "arbitrary"), FFN_VMEM_LIMIT),
        name="ffn_ln",
    )(x, wg, wu, wo, g, b)


def _proj_kernel(mode, x_ref, w_ref, *rest):
    tn = w_ref.shape[1]
    starts = range(0, tn, PROJ_SUB)

    def products():
        x = x_ref[...]
        return [(c, jnp.dot(x, w_ref[:, c:min(c + PROJ_SUB, tn)], preferred_element_type=F32)) for c in starts]

    if mode == "dilated":
        cos_ref, sin_ref, *o_refs, slab_ref = rest
        n_rope_steps = 2 * DIL_WIDTH // tn
        tm = x_ref.shape[0]

        def emit(rope):
            if rope:
                cos = cos_ref[...]
                sin = sin_ref[...]
            for c, z in products():
                group = c // PROJ_SUB
                d = DIL_PATTERNS[group][1]
                o_ref = o_refs[group]
                for h in range(HEADS_PER_GROUP):
                    zh = z[:, h * LANES:(h + 1) * LANES]
                    if rope:
                        zh = zh * cos + pltpu.roll(zh, LANES // 2, 1) * sin
                    if d == 1:
                        o_ref[h, 0] = zh.astype(BF16)
                        continue
                    slab = group * HEADS_PER_GROUP + h
                    slab_ref[slab] = zh
                    for r in range(d):
                        o_ref[h, r] = slab_ref[slab, pl.ds(r, tm // d, stride=d), :].astype(BF16)

        pl.when(pl.program_id(1) < n_rope_steps)(functools.partial(emit, True))
        pl.when(pl.program_id(1) >= n_rope_steps)(functools.partial(emit, False))
    elif mode == "gate":
        b_ref, o_ref = rest
        for c, z in products():
            sl = slice(c, c + z.shape[1])
            o_ref[:, sl] = jax.nn.sigmoid(z + b_ref[:, sl]).astype(o_ref.dtype)
    else:
        (o_ref,) = rest
        for c, z in products():
            o_ref[:, c:c + z.shape[1]] = z.astype(o_ref.dtype)


def _proj(x, w, out_dtype, *, mode="plain", extras=(), seq=None, tm=1024, tn=512):
    rows = x.shape[0]
    n = w.shape[1]
    tn = min(tn, n)
    in_specs = [
        pl.BlockSpec((tm, D_MODEL), lambda i, j: (i, 0)),
        pl.BlockSpec((D_MODEL, tn), lambda i, j: (0, j)),
    ]
    out_specs = pl.BlockSpec((tm, tn), lambda i, j: (i, j))
    out_shape = jax.ShapeDtypeStruct((rows, n), out_dtype)
    scratch_shapes = []
    if mode == "dilated":
        assert tn == DIL_WIDTH and PROJ_SUB == GROUP_WIDTH
        tiles_per_seq = seq // tm
        in_specs += [pl.BlockSpec((tm, LANES), lambda i, j: (i % tiles_per_seq, 0))] * 2
        out_specs = [
            pl.BlockSpec((HEADS_PER_GROUP, None, d, tm // d, LANES),
                         lambda i, j: (j, i // tiles_per_seq, 0, i % tiles_per_seq, 0))
            for _, d in DIL_PATTERNS]
        out_shape = [jax.ShapeDtypeStruct((DIL_KINDS * HEADS_PER_GROUP, rows // seq, d, seq // d, LANES), out_dtype)
                     for _, d in DIL_PATTERNS]
        scratch_shapes = [pltpu.VMEM((len(DIL_PATTERNS) * HEADS_PER_GROUP, tm, LANES), F32)]
    elif mode == "gate":
        in_specs += [pl.BlockSpec((1, tn), lambda i, j: (0, j))]
    return pl.pallas_call(
        functools.partial(_proj_kernel, mode),
        grid=(rows // tm, n // tn),
        in_specs=in_specs,
        out_specs=out_specs,
        out_shape=out_shape,
        scratch_shapes=scratch_shapes,
        compiler_params=_params(("parallel", "arbitrary")),
        name="proj_" + mode,
    )(x, w, *extras)


def _latent_kernel(lat_ref, qg_ref, kvg_ref, wqn_ref, wqp_ref, wkn_ref, wvt_ref,
                   cos_ref, sin_lo_ref, sin_hi_ref, q_ref, k_ref, vt_ref):
    lat = lat_ref[...]
    c_q = lat[:, :Q_LORA]
    c_kv = lat[:, Q_LORA:Q_LORA + KV_LORA]
    k_rope = lat[:, Q_LORA + KV_LORA:]

    def rms(x, g):
        return (x * lax.rsqrt(jnp.mean(x * x, axis=-1, keepdims=True) + RMS_EPS) * g).astype(BF16)

    cos = cos_ref[...]
    sin_lo = sin_lo_ref[...]
    sin_hi = sin_hi_ref[...]

    def rope64(x):
        quarter = QK_ROPE // 2
        return (x * cos + pltpu.roll(x, LANES - quarter, 1) * sin_lo
                + pltpu.roll(x, quarter, 1) * sin_hi)

    nq = rms(c_q, qg_ref[...])
    nkv = rms(c_kv, kvg_ref[...])
    q_nope = jnp.dot(nq, wqn_ref[...], preferred_element_type=F32)
    q_pe = jnp.dot(nq, wqp_ref[...], preferred_element_type=F32)
    k_nope = jnp.dot(nkv, wkn_ref[...], preferred_element_type=F32)
    vt = lax.dot_general(wvt_ref[...], nkv, _NT, preferred_element_type=F32)
    k_pe = rope64(k_rope).astype(BF16)
    for h in range(MLA_HEADS):
        src = slice(h * LANES, (h + 1) * LANES)
        lo = slice(h * MLA_SLAB, h * MLA_SLAB + LANES)
        hi = slice(h * MLA_SLAB + LANES, (h + 1) * MLA_SLAB)
        q_ref[:, lo] = (q_nope[:, src] * MLA_Q_SCALE).astype(BF16)
        q_ref[:, hi] = (rope64(q_pe[:, src]) * MLA_Q_SCALE).astype(BF16)
        k_ref[:, lo] = k_nope[:, src].astype(BF16)
        k_ref[:, hi] = k_pe
    vt_ref[...] = vt.astype(BF16)


def _latent(lat, qg, kvg, wqn, wqp, wkn, wvt, tables, *, seq, tm=512):
    rows = lat.shape[0]
    tiles_per_seq = seq // tm
    width = MLA_HEADS * LANES
    row_spec = lambda w: pl.BlockSpec((tm, w), lambda i: (i, 0))
    full = lambda a: pl.BlockSpec(a.shape, lambda i: (0, 0), pipeline_mode=pl.Buffered(1))
    table_spec = pl.BlockSpec((tm, LANES), lambda i: (i % tiles_per_seq, 0))
    return pl.pallas_call(
        _latent_kernel,
        grid=(rows // tm,),
        in_specs=[row_spec(LAT_WIDTH), full(qg), full(kvg), full(wqn), full(wqp), full(wkn), full(wvt),
                  table_spec, table_spec, table_spec],
        out_specs=[row_spec(MLA_HEADS * MLA_SLAB), row_spec(MLA_HEADS * MLA_SLAB),
                   pl.BlockSpec((width, tm), lambda i: (0, i))],
        out_shape=[jax.ShapeDtypeStruct((rows, MLA_HEADS * MLA_SLAB), BF16),
                   jax.ShapeDtypeStruct((rows, MLA_HEADS * MLA_SLAB), BF16),
                   jax.ShapeDtypeStruct((width, rows), BF16)],
        compiler_params=_params(("parallel",)),
        name="latent",
    )(lat, qg, kvg, wqn, wqp, wkn, wvt, *tables)


def _mla_kernel(n_kv, tk, q_ref, k_ref, vt_ref, o_ref, acc_ref, s_ref, mblk_ref):
    n_sub, _, ts = acc_ref.shape
    acc_ref[...] = jnp.zeros_like(acc_ref)

    def scores_into(slot, j):
        k = k_ref[pl.ds(pl.multiple_of(j * tk, tk), tk), :]
        for t in range(n_sub):
            s = lax.dot_general(k, q_ref[pl.ds(t * ts, ts), :], _NT, preferred_element_type=F32)
            s_ref[slot, t] = s
            mblk_ref[slot, t] = jnp.max(s, axis=0, keepdims=True)

    def consume(slot, j, carry):
        vt = vt_ref[:, pl.ds(pl.multiple_of(j * tk, tk), tk)]
        out = []
        for t in range(n_sub):
            m_prev, l_prev = carry[t]
            m_new = jnp.maximum(m_prev, mblk_ref[slot, t])
            alpha = jnp.exp2(m_prev - m_new)
            p = jnp.exp2(s_ref[slot, t] - m_new)
            l_new = alpha * l_prev + jnp.sum(p, axis=0, keepdims=True)
            acc_ref[t] = alpha * acc_ref[t] + jnp.dot(vt, p.astype(BF16), preferred_element_type=F32)
            out.append((m_new, l_new))
        return tuple(out)

    def body(jj, carry):
        j = 2 * jj
        scores_into(1, j + 1)
        carry = consume(0, j, carry)
        scores_into(0, j + 2)
        return consume(1, j + 1, carry)

    scores_into(0, 0)
    init = tuple((jnp.full((1, ts), -jnp.inf, F32), jnp.zeros((1, ts), F32)) for _ in range(n_sub))
    carry = lax.fori_loop(0, n_kv // 2 - 1, body, init)
    scores_into(1, n_kv - 1)
    carry = consume(0, n_kv - 2, carry)
    carry = consume(1, n_kv - 1, carry)
    for t in range(n_sub):
        o_ref[pl.ds(t * ts, ts), :] = (acc_ref[t] / carry[t][1]).T.astype(o_ref.dtype)


def _mla(q, k, vt, *, batch, seq, tq=2048, ts=512, tk=1024):
    n_q = seq // tq
    return pl.pallas_call(
        functools.partial(_mla_kernel, seq // tk, tk),
        grid=(batch, MLA_HEADS, n_q),
        in_specs=[
            pl.BlockSpec((tq, MLA_SLAB), lambda b, h, i: (b * n_q + i, h)),
            pl.BlockSpec((seq, MLA_SLAB), lambda b, h, i: (b, h)),
            pl.BlockSpec((V_DIM, seq), lambda b, h, i: (h, b)),
        ],
        out_specs=pl.BlockSpec((tq, V_DIM), lambda b, h, i: (b * n_q + i, h)),
        out_shape=jax.ShapeDtypeStruct((batch * seq, MLA_HEADS * V_DIM), BF16),
        scratch_shapes=[pltpu.VMEM((tq // ts, V_DIM, ts), F32), pltpu.VMEM((2, tq // ts, tk, ts), F32),
                        pltpu.VMEM((2, tq // ts, 1, ts), F32)],
        compiler_params=_params(("parallel", "parallel", "arbitrary")),
        name="mla",
    )(q, k, vt)


def _dilated_kernel(d, half, stream_len, q_ref, kp_ref, kc_ref, kn_ref, vp_ref, vc_ref, vn_ref,
                    o_ref, lse_ref, ks_ref, vs_ref):
    n_heads = q_ref.shape[0]
    tl = DIL_ROWS // d
    i = pl.program_id(2)
    for dst, before, cur, after in ((ks_ref, kp_ref, kc_ref, kn_ref), (vs_ref, vp_ref, vc_ref, vn_ref)):
        dst[:, :, 0:half] = before[...]
        dst[:, :, half:half + tl] = cur[...]
        dst[:, :, half + tl:] = after[...]

    chunk = 2 * half
    row = lax.broadcasted_iota(jnp.int32, (chunk, 2 * chunk), 0)
    col = lax.broadcasted_iota(jnp.int32, (chunk, 2 * chunk), 1)
    band = (col >= row) & (col <= row + 2 * half)
    shift = d.bit_length() - 1

    def unit_group(ug, carry):
        jobs = []
        for uu in range(DIL_UNITS_PER_TRIP):
            u = ug * DIL_UNITS_PER_TRIP + uu
            r = u & (d - 1)
            c = u >> shift
            row0 = pl.multiple_of(c * chunk, chunk)
            k_idx = i * tl + c * chunk - half + col
            mask = band & (k_idx >= 0) & (k_idx < stream_len)
            for h in range(n_heads):
                q = q_ref[h, r, pl.ds(row0, chunk), :]
                k = ks_ref[h, r, pl.ds(row0, 2 * chunk), :]
                s = lax.dot_general(q, k, _NT, preferred_element_type=F32) * DIL_SCALE
                jobs.append((h, r, row0, mask, s))
        for h, r, row0, mask, s in jobs:
            s = jnp.where(mask, s, NEG_INF)
            m = jnp.max(s, axis=1, keepdims=True)
            p = jnp.exp(s - m)
            denom = jnp.sum(p, axis=1, keepdims=True)
            pn = (p / denom).astype(BF16)
            o = jnp.dot(pn, vs_ref[h, r, pl.ds(row0, 2 * chunk), :], preferred_element_type=F32)
            start = r + d * row0
            o_ref[h, pl.ds(start, chunk, stride=d), :] = o
            lse_ref[h, pl.ds(start, chunk, stride=d), :] = jnp.broadcast_to(m + jnp.log(denom), (chunk, LANES))
        return carry

    lax.fori_loop(0, DIL_ROWS // chunk // DIL_UNITS_PER_TRIP, unit_group, 0)


def _dilated_group(qkv, *, group, window, dilation, batch, seq):
    d = dilation
    half = window // (2 * d)
    n_r = seq // DIL_ROWS
    tl = DIL_ROWS // d
    halo_per_step = tl // half
    n_halo = seq // d // half
    steps_per_group = HEADS_PER_GROUP // DIL_HEADS_PER_STEP
    blk = (DIL_HEADS_PER_STEP, None, d, tl, LANES)
    hblk = (DIL_HEADS_PER_STEP, None, d, half, LANES)

    def specs(kind):
        heads = lambda g: kind * steps_per_group + g
        cur = pl.BlockSpec(blk, lambda g, b, i: (heads(g), b, 0, i, 0))
        before = pl.BlockSpec(hblk, lambda g, b, i: (heads(g), b, 0, jnp.maximum(i * halo_per_step - 1, 0), 0))
        after = pl.BlockSpec(hblk, lambda g, b, i: (heads(g), b, 0,
                                                    jnp.minimum((i + 1) * halo_per_step, n_halo - 1), 0))
        return before, cur, after

    out_spec = pl.BlockSpec((DIL_HEADS_PER_STEP, DIL_ROWS, LANES), lambda g, b, i: (g, b * n_r + i, 0))
    out_shape = jax.ShapeDtypeStruct((HEADS_PER_GROUP, batch * seq, LANES), F32)
    return pl.pallas_call(
        functools.partial(_dilated_kernel, d, half, seq // d),
        grid=(steps_per_group, batch, n_r),
        in_specs=[specs(0)[1], *specs(1), *specs(2)],
        out_specs=[out_spec, out_spec],
        out_shape=[out_shape, out_shape],
        scratch_shapes=[pltpu.VMEM((DIL_HEADS_PER_STEP, d, tl + 2 * half, LANES), BF16),
                        pltpu.VMEM((DIL_HEADS_PER_STEP, d, tl + 2 * half, LANES), BF16)],
        compiler_params=_params(("parallel", "parallel", "parallel")),
        name="dilated_g%d" % group,
    )(*([qkv] * 7))


def _merge_kernel(oa_ref, o0_ref, o1_ref, o2_ref, l0_ref, l1_ref, l2_ref, ga_ref, gb_ref, y1_ref,
                  wa_ref, wb_ref, wo_ref, g_ref, b_ref, y_ref):
    heads = []
    for h in range(HEADS_PER_GROUP):
        l0, l1, l2 = l0_ref[h], l1_ref[h], l2_ref[h]
        m = jnp.maximum(jnp.maximum(l0, l1), l2)
        e0, e1, e2 = jnp.exp(l0 - m), jnp.exp(l1 - m), jnp.exp(l2 - m)
        denom = e0 + e1 + e2
        heads.append(((e0 / denom) * o0_ref[h] + (e1 / denom) * o1_ref[h] + (e2 / denom) * o2_ref[h]).astype(BF16))
    out_a = oa_ref[...]
    out_b = jnp.concatenate(heads, axis=1)
    cols = [slice(c, c + PROJ_SUB) for c in range(0, D_MODEL, PROJ_SUB)]
    branches = [(jnp.dot(out_a, wa_ref[:, sl], preferred_element_type=F32),
                 jnp.dot(out_b, wb_ref[:, sl], preferred_element_type=F32)) for sl in cols]
    mix = None
    for sl, (branch_a, branch_b) in zip(cols, branches):
        merged = (ga_ref[:, sl] * branch_a + gb_ref[:, sl] * branch_b).astype(BF16)
        part = jnp.dot(merged, wo_ref[sl, :], preferred_element_type=F32)
        mix = part if mix is None else mix + part
    y_ref[...] = _layer_norm(ALPHA * y1_ref[...] + mix, g_ref[...], b_ref[...])


def _merge(out_a, outs, lses, gates, y1, wa, wb, wo, g, b, *, tm=256):
    rows = y1.shape[0]
    row_spec = lambda w, c=0: pl.BlockSpec((tm, w), lambda i: (i, c))
    full = lambda a: pl.BlockSpec(a.shape, lambda i: (0, 0), pipeline_mode=pl.Buffered(1))
    return pl.pallas_call(
        _merge_kernel,
        grid=(rows // tm,),
        in_specs=[row_spec(MLA_HEADS * V_DIM)]
                 + [pl.BlockSpec((HEADS_PER_GROUP, tm, LANES), lambda i: (0, i, 0))] * 6
                 + [row_spec(D_MODEL, 0), row_spec(D_MODEL, 1), row_spec(D_MODEL)]
                 + [full(wa), full(wb), full(wo), full(g), full(b)],
        out_specs=row_spec(D_MODEL),
        out_shape=jax.ShapeDtypeStruct((rows, D_MODEL), F32),
        compiler_params=_params(("parallel",)),
        name="merge",
    )(out_a, *outs, *lses, gates, gates, y1, wa, wb, wo, g, b)


def _ffn_in_kernel(w_ref, wg_ref, wu_ref):
    w = w_ref[...]
    pad = jnp.zeros((w.shape[0], D_FF_PAD - D_FF), BF16)
    for dst, part in ((wg_ref, w[:, :D_FF]), (wu_ref, w[:, D_FF:])):
        dst[:, :D_FF] = part.astype(BF16)
        dst[:, D_FF:] = pad


def _ffn_out_kernel(n_valid, w_ref, o_ref):
    @pl.when(pl.program_id(0) < n_valid)
    def _():
        o_ref[...] = w_ref[...].astype(BF16)

    @pl.when(pl.program_id(0) >= n_valid)
    def _():
        o_ref[...] = jnp.zeros_like(o_ref)


def _ffn_weights(w_in, w_out, *, tr=64, tc=LANES):
    half = jax.ShapeDtypeStruct((D_MODEL, D_FF_PAD), BF16)
    wg, wu = pl.pallas_call(
        _ffn_in_kernel,
        grid=(D_MODEL // tr,),
        in_specs=[pl.BlockSpec((tr, 2 * D_FF), lambda i: (i, 0))],
        out_specs=[pl.BlockSpec((tr, D_FF_PAD), lambda i: (i, 0))] * 2,
        out_shape=[half, half],
        compiler_params=_params(("parallel",)),
        name="ffn_w_in",
    )(w_in)
    n_valid = D_FF // tc
    wo = pl.pallas_call(
        functools.partial(_ffn_out_kernel, n_valid),
        grid=(D_FF_PAD // tc,),
        in_specs=[pl.BlockSpec((tc, D_MODEL), lambda i: (jnp.minimum(i, n_valid - 1), 0))],
        out_specs=pl.BlockSpec((tc, D_MODEL), lambda i: (i, 0)),
        out_shape=jax.ShapeDtypeStruct((D_FF_PAD, D_MODEL), BF16),
        compiler_params=_params(("parallel",)),
        name="ffn_w_out",
    )(w_out)
    return wg, wu, wo


def _rope_tables(seq):
    pos = jnp.arange(seq, dtype=F32)[:, None]

    def angles(d):
        inv_freq = ROPE_THETA ** (-jnp.arange(0, d, 2, dtype=F32) / d)
        ang = pos * inv_freq[None, :]
        return jnp.cos(ang), jnp.sin(ang)

    cos, sin = angles(DIL_HEAD_DIM)
    dil = (jnp.concatenate([cos, cos], axis=1), jnp.concatenate([-sin, sin], axis=1))
    cos, sin = angles(QK_ROPE)
    zeros = jnp.zeros_like(cos)
    pad = jnp.zeros((seq, LANES - QK_ROPE), F32)
    mla = (jnp.concatenate([cos, cos, pad], axis=1),
           jnp.concatenate([-sin, zeros, pad], axis=1),
           jnp.concatenate([zeros, sin, pad], axis=1))
    return dil, mla


def _prepare(ffn1_w_in, ffn1_w_out, ln1_g, ln1_b, w_in_mix, b_gate, q_norm_g, w_uq, kv_norm_g, w_ukv,
             w_branch_a, w_branch_b, w_out_mix, ln2_g, ln2_b, ffn2_w_in, ffn2_w_out, ln3_g, ln3_b):
    ffn = _ffn_weights
    row = lambda a: a.reshape(1, -1)
    o_lat = Q_LORA + KV_LORA + QK_ROPE
    o_qk = o_lat + 2 * DIL_WIDTH
    o_v = o_qk + DIL_WIDTH
    w_lat = jnp.pad(w_in_mix[:, :o_lat], ((0, 0), (0, LAT_WIDTH - o_lat))).astype(BF16)
    uq = w_uq.reshape(Q_LORA, MLA_HEADS, QK_NOPE + QK_ROPE)
    ukv = w_ukv.reshape(KV_LORA, MLA_HEADS, QK_NOPE + V_DIM)
    flat = lambda a: a.reshape(a.shape[0], -1).astype(BF16)
    return dict(
        ffn1=ffn(ffn1_w_in, ffn1_w_out), ln1=(row(ln1_g), row(ln1_b)),
        ffn2=ffn(ffn2_w_in, ffn2_w_out), ln3=(row(ln3_g), row(ln3_b)),
        w_lat=w_lat,
        w_dil=w_in_mix[:, o_lat:o_v].astype(BF16),
        w_gate=w_in_mix[:, o_v:].astype(BF16),
        b_gate=row(b_gate),
        q_norm_g=row(q_norm_g), kv_norm_g=row(kv_norm_g),
        wqn=flat(uq[:, :, :QK_NOPE]),
        wqp=flat(jnp.pad(uq[:, :, QK_NOPE:], ((0, 0), (0, 0), (0, LANES - QK_ROPE)))),
        wkn=flat(ukv[:, :, :QK_NOPE]),
        wvt=flat(ukv[:, :, QK_NOPE:]).T,
        wa=w_branch_a.astype(BF16), wb=w_branch_b.astype(BF16), wo=w_out_mix.astype(BF16),
        ln2=(row(ln2_g), row(ln2_b)),
    )


def _encoder_layer(x, p):
    batch, seq, _ = x.shape
    x2 = x.reshape(batch * seq, D_MODEL)
    dil_tables, mla_tables = _rope_tables(seq)

    y1, y1_bf = _ffn_ln(x2, *p["ffn1"], *p["ln1"], emit_bf16=True)

    lat = _proj(y1_bf, p["w_lat"], F32, tn=LAT_WIDTH)
    qkv_d = _proj(y1_bf, p["w_dil"], BF16, mode="dilated", extras=dil_tables, seq=seq, tn=DIL_WIDTH)
    gates = _proj(y1_bf, p["w_gate"], F32, mode="gate", extras=(p["b_gate"],), tn=D_MODEL)

    q, k, vt = _latent(lat, p["q_norm_g"], p["kv_norm_g"], p["wqn"], p["wqp"], p["wkn"], p["wvt"],
                       mla_tables, seq=seq)
    out_a = _mla(q, k, vt, batch=batch, seq=seq)

    outs, lses = [], []
    for group, (window, dilation) in enumerate(DIL_PATTERNS):
        o, lse = _dilated_group(qkv_d[group], group=group, window=window, dilation=dilation, batch=batch, seq=seq)
        outs.append(o)
        lses.append(lse)

    y2 = _merge(out_a, outs, lses, gates, y1, p["wa"], p["wb"], p["wo"], *p["ln2"])
    (y3,) = _ffn_ln(y2, *p["ffn2"], *p["ln3"], emit_bf16=False)
    return y3.reshape(batch, seq, D_MODEL)


def kernel(x_prompt, x_sample, ffn1_w_in, ffn1_w_out, ln1_g, ln1_b, w_in_mix, b_gate, q_norm_g, w_uq, kv_norm_g, w_ukv, w_branch_a, w_branch_b, w_out_mix, ln2_g, ln2_b, ffn2_w_in, ffn2_w_out, ln3_g, ln3_b):
    weights = (ffn1_w_in, ffn1_w_out, ln1_g, ln1_b, w_in_mix, b_gate, q_norm_g, w_uq, kv_norm_g, w_ukv,
               w_branch_a, w_branch_b, w_out_mix, ln2_g, ln2_b, ffn2_w_in, ffn2_w_out, ln3_g, ln3_b)
    y_prompt, y_sample = x_prompt, x_sample
    for layer in range(DEPTH):
        p = _prepare(*(w[layer] for w in weights))
        y_prompt = _encoder_layer(y_prompt, p)
        y_sample = _encoder_layer(y_sample, p)
    return (y_prompt, y_sample)
```

```python
import functools

import jax
import jax.numpy as jnp
from jax import lax
from jax.experimental import pallas as pl
from jax.experimental.pallas import tpu as pltpu

D_MODEL = 2048
DEPTH = 1
MLA_HEADS = 8
Q_LORA = 512
KV_LORA = 512
QK_NOPE = 128
QK_ROPE = 64
V_DIM = 128
DIL_PATTERNS = ((128, 1), (512, 4), (2048, 16))
HEADS_PER_GROUP = 4
DIL_HEAD_DIM = 128
DIL_WIDTH = len(DIL_PATTERNS) * HEADS_PER_GROUP * DIL_HEAD_DIM
D_FF = 5504
ROPE_THETA = 10000.0
LN_EPS = 1e-5
RMS_EPS = 1e-6
NEG_INF = -1e30
ALPHA = (2 * DEPTH) ** 0.25
MLA_SCALE = (QK_NOPE + QK_ROPE) ** -0.5
LOG2_E = 1.4426950408889634
MLA_Q_SCALE = MLA_SCALE * LOG2_E
DIL_KINDS = 3
DIL_SCALE = DIL_HEAD_DIM ** -0.5

LANES = 128
D_FF_PAD = 5632
FF_CHUNK = 512
LAT_WIDTH = Q_LORA + KV_LORA + LANES
MLA_SLAB = 2 * LANES
GROUP_WIDTH = HEADS_PER_GROUP * DIL_HEAD_DIM
PROJ_SUB = 512
DIL_ROWS = 2048
DIL_HEADS_PER_STEP = 4
DIL_UNITS_PER_TRIP = 2
VMEM_LIMIT = 56 * 1024 * 1024
FFN_VMEM_LIMIT = 60 * 1024 * 1024

BF16 = jnp.bfloat16
F32 = jnp.float32
_NT = (((1,), (1,)), ((), ()))


def _params(semantics, vmem_limit=VMEM_LIMIT):
    return pltpu.CompilerParams(dimension_semantics=semantics, vmem_limit_bytes=vmem_limit)


def _layer_norm(v, g, b, scale=1.0):
    mu = jnp.mean(v, axis=-1, keepdims=True)
    c = v - mu
    var = jnp.mean(c * c, axis=-1, keepdims=True)
    return c * (scale * lax.rsqrt(scale * scale * var + LN_EPS)) * g + b


def _ffn_ln_kernel(n_chunks, x_ref, wg_ref, wu_ref, wo_ref, g_ref, b_ref, y_ref, xbf_ref):
    k = pl.program_id(1)

    @pl.when(k == 0)
    def _():
        x = x_ref[...]
        xbf_ref[...] = x.astype(BF16)
        y_ref[...] = (2.0 * ALPHA) * x

    xb = xbf_ref[...]
    gate = jnp.dot(xb, wg_ref[...], preferred_element_type=F32)
    up = jnp.dot(xb, wu_ref[...], preferred_element_type=F32)
    act = gate * jax.nn.sigmoid(gate) * up
    y_ref[...] += jnp.dot(act.astype(BF16), wo_ref[...], preferred_element_type=F32)

    @pl.when(k == n_chunks - 1)
    def _():
        y_ref[...] = _layer_norm(y_ref[...], g_ref[...], b_ref[...], scale=0.5)


def _ffn_ln(x, wg, wu, wo, g, b, *, tm=1024):
    rows = x.shape[0]
    n_chunks = D_FF_PAD // FF_CHUNK
    return pl.pallas_call(
        functools.partial(_ffn_ln_kernel, n_chunks),
        grid=(rows // tm, n_chunks),
        in_specs=[
            pl.BlockSpec((tm, D_MODEL), lambda i, k: (i, 0)),
            pl.BlockSpec((D_MODEL, FF_CHUNK), lambda i, k: (0, k)),
            pl.BlockSpec((D_MODEL, FF_CHUNK), lambda i, k: (0, k)),
            pl.BlockSpec((FF_CHUNK, D_MODEL), lambda i, k: (k, 0)),
            pl.BlockSpec((1, D_MODEL), lambda i, k: (0, 0)),
            pl.BlockSpec((1, D_MODEL), lambda i, k: (0, 0)),
        ],
        out_specs=pl.BlockSpec((tm, D_MODEL), lambda i, k: (i, 0)),
        out_shape=jax.ShapeDtypeStruct((rows, D_MODEL), F32),
        scratch_shapes=[pltpu.VMEM((tm, D_MODEL), BF16)],
        compiler_params=_params(("parallel", "arbitrary"), FFN_VMEM_LIMIT),
        name="ffn_ln",
    )(x, wg, wu, wo, g, b)


def _proj_kernel(mode, x_ref, w_ref, *rest):
    tn = w_ref.shape[1]
    starts = range(0, tn, PROJ_SUB)

    def products(x=None):
        x = x_ref[...] if x is None else x
        return [(c, jnp.dot(x, w_ref[:, c:min(c + PROJ_SUB, tn)], preferred_element_type=F32)) for c in starts]

    if mode == "dilated":
        cos_ref, sin_ref, *o_refs, slab_ref = rest
        n_rope_steps = 2 * DIL_WIDTH // tn
        tm = x_ref.shape[0]

        def emit(rope):
            if rope:
                cos = cos_ref[...]
                sin = sin_ref[...]
            for c, z in products():
                group = c // PROJ_SUB
                d = DIL_PATTERNS[group][1]
                o_ref = o_refs[group]
                for h in range(HEADS_PER_GROUP):
                    zh = z[:, h * LANES:(h + 1) * LANES]
                    if rope:
                        zh = zh * cos + pltpu.roll(zh, LANES // 2, 1) * sin
                    if d == 1:
                        o_ref[h, 0] = zh.astype(BF16)
                        continue
                    slab = group * HEADS_PER_GROUP + h
                    slab_ref[slab] = zh
                    for r in range(d):
                        o_ref[h, r] = slab_ref[slab, pl.ds(r, tm // d, stride=d), :].astype(BF16)

        pl.when(pl.program_id(1) < n_rope_steps)(functools.partial(emit, True))
        pl.when(pl.program_id(1) >= n_rope_steps)(functools.partial(emit, False))
    elif mode == "gate":
        b_ref, o_ref = rest
        for c, z in products():
            sl = slice(c, c + z.shape[1])
            o_ref[:, sl] = jax.nn.sigmoid(z + b_ref[:, sl]).astype(o_ref.dtype)
    else:
        o_ref, xbf_ref = rest
        x = x_ref[...].astype(BF16)
        xbf_ref[...] = x
        for c, z in products(x):
            o_ref[:, c:c + z.shape[1]] = z.astype(o_ref.dtype)


def _proj(x, w, out_dtype, *, mode="cast", extras=(), seq=None, tm=1024, tn=512):
    rows = x.shape[0]
    n = w.shape[1]
    tn = min(tn, n)
    in_specs = [
        pl.BlockSpec((tm, D_MODEL), lambda i, j: (i, 0)),
        pl.BlockSpec((D_MODEL, tn), lambda i, j: (0, j)),
    ]
    out_specs = pl.BlockSpec((tm, tn), lambda i, j: (i, j))
    out_shape = jax.ShapeDtypeStruct((rows, n), out_dtype)
    scratch_shapes = []
    if mode == "dilated":
        assert tn == DIL_WIDTH and PROJ_SUB == GROUP_WIDTH
        tiles_per_seq = seq // tm
        in_specs += [pl.BlockSpec((tm, LANES), lambda i, j: (i % tiles_per_seq, 0))] * 2
        out_specs = [
            pl.BlockSpec((HEADS_PER_GROUP, None, d, tm // d, LANES),
                         lambda i, j: (j, i // tiles_per_seq, 0, i % tiles_per_seq, 0))
            for _, d in DIL_PATTERNS]
        out_shape = [jax.ShapeDtypeStruct((DIL_KINDS * HEADS_PER_GROUP, rows // seq, d, seq // d, LANES), out_dtype)
                     for _, d in DIL_PATTERNS]
        scratch_shapes = [pltpu.VMEM((len(DIL_PATTERNS) * HEADS_PER_GROUP, tm, LANES), F32)]
    elif mode == "gate":
        in_specs += [pl.BlockSpec((1, tn), lambda i, j: (0, j))]
    else:
        assert tn == n
        out_specs = [out_specs, pl.BlockSpec((tm, D_MODEL), lambda i, j: (i, 0))]
        out_shape = [out_shape, jax.ShapeDtypeStruct((rows, D_MODEL), BF16)]
    return pl.pallas_call(
        functools.partial(_proj_kernel, mode),
        grid=(rows // tm, n // tn),
        in_specs=in_specs,
        out_specs=out_specs,
        out_shape=out_shape,
        scratch_shapes=scratch_shapes,
        compiler_params=_params(("parallel", "arbitrary")),
        name="proj_" + mode,
    )(x, w, *extras)


def _latent_kernel(lat_ref, qg_ref, kvg_ref, wqn_ref, wqp_ref, wkn_ref, wvt_ref,
                   cos_ref, sin_lo_ref, sin_hi_ref, q_ref, k_ref, vt_ref):
    lat = lat_ref[...]
    c_q = lat[:, :Q_LORA]
    c_kv = lat[:, Q_LORA:Q_LORA + KV_LORA]
    k_rope = lat[:, Q_LORA + KV_LORA:]

    def rms(x, g):
        return (x * lax.rsqrt(jnp.mean(x * x, axis=-1, keepdims=True) + RMS_EPS) * g).astype(BF16)

    cos = cos_ref[...]
    sin_lo = sin_lo_ref[...]
    sin_hi = sin_hi_ref[...]

    def rope64(x):
        quarter = QK_ROPE // 2
        return (x * cos + pltpu.roll(x, LANES - quarter, 1) * sin_lo
                + pltpu.roll(x, quarter, 1) * sin_hi)

    nq = rms(c_q, qg_ref[...])
    nkv = rms(c_kv, kvg_ref[...])
    q_nope = jnp.dot(nq, wqn_ref[...], preferred_element_type=F32)
    q_pe = jnp.dot(nq, wqp_ref[...], preferred_element_type=F32)
    k_nope = jnp.dot(nkv, wkn_ref[...], preferred_element_type=F32)
    vt = lax.dot_general(wvt_ref[...], nkv, _NT, preferred_element_type=F32)
    k_pe = rope64(k_rope).astype(BF16)
    for h in range(MLA_HEADS):
        src = slice(h * LANES, (h + 1) * LANES)
        lo = slice(h * MLA_SLAB, h * MLA_SLAB + LANES)
        hi = slice(h * MLA_SLAB + LANES, (h + 1) * MLA_SLAB)
        q_ref[:, lo] = (q_nope[:, src] * MLA_Q_SCALE).astype(BF16)
        q_ref[:, hi] = (rope64(q_pe[:, src]) * MLA_Q_SCALE).astype(BF16)
        k_ref[:, lo] = k_nope[:, src].astype(BF16)
        k_ref[:, hi] = k_pe
    vt_ref[...] = vt.astype(BF16)


def _latent(lat, qg, kvg, wqn, wqp, wkn, wvt, tables, *, seq, tm=512):
    rows = lat.shape[0]
    tiles_per_seq = seq // tm
    width = MLA_HEADS * LANES
    row_spec = lambda w: pl.BlockSpec((tm, w), lambda i: (i, 0))
    full = lambda a: pl.BlockSpec(a.shape, lambda i: (0, 0), pipeline_mode=pl.Buffered(1))
    table_spec = pl.BlockSpec((tm, LANES), lambda i: (i % tiles_per_seq, 0))
    return pl.pallas_call(
        _latent_kernel,
        grid=(rows // tm,),
        in_specs=[row_spec(LAT_WIDTH), full(qg), full(kvg), full(wqn), full(wqp), full(wkn), full(wvt),
                  table_spec, table_spec, table_spec],
        out_specs=[row_spec(MLA_HEADS * MLA_SLAB), row_spec(MLA_HEADS * MLA_SLAB),
                   pl.BlockSpec((width, tm), lambda i: (0, i))],
        out_shape=[jax.ShapeDtypeStruct((rows, MLA_HEADS * MLA_SLAB), BF16),
                   jax.ShapeDtypeStruct((rows, MLA_HEADS * MLA_SLAB), BF16),
                   jax.ShapeDtypeStruct((width, rows), BF16)],
        compiler_params=_params(("parallel",)),
        name="latent",
    )(lat, qg, kvg, wqn, wqp, wkn, wvt, *tables)


def _mla_kernel(n_kv, tk, q_ref, k_ref, vt_ref, o_ref, acc_ref, s_ref, mblk_ref):
    n_sub, _, ts = acc_ref.shape
    acc_ref[...] = jnp.zeros_like(acc_ref)

    def scores_into(slot, j):
        k = k_ref[pl.ds(pl.multiple_of(j * tk, tk), tk), :]
        for t in range(n_sub):
            s = lax.dot_general(k, q_ref[pl.ds(t * ts, ts), :], _NT, preferred_element_type=F32)
            s_ref[slot, t] = s
            mblk_ref[slot, t] = jnp.max(s, axis=0, keepdims=True)

    def consume(slot, j, carry):
        vt = vt_ref[:, pl.ds(pl.multiple_of(j * tk, tk), tk)]
        out = []
        for t in range(n_sub):
            m_prev, l_prev = carry[t]
            m_new = jnp.maximum(m_prev, mblk_ref[slot, t])
            alpha = jnp.exp2(m_prev - m_new)
            p = jnp.exp2(s_ref[slot, t] - m_new)
            l_new = alpha * l_prev + jnp.sum(p, axis=0, keepdims=True)
            acc_ref[t] = alpha * acc_ref[t] + jnp.dot(vt, p.astype(BF16), preferred_element_type=F32)
            out.append((m_new, l_new))
        return tuple(out)

    def body(jj, carry):
        j = 2 * jj
        scores_into(1, j + 1)
        carry = consume(0, j, carry)
        scores_into(0, j + 2)
        return consume(1, j + 1, carry)

    scores_into(0, 0)
    init = tuple((jnp.full((1, ts), -jnp.inf, F32), jnp.zeros((1, ts), F32)) for _ in range(n_sub))
    carry = lax.fori_loop(0, n_kv // 2 - 1, body, init)
    scores_into(1, n_kv - 1)
    carry = consume(0, n_kv - 2, carry)
    carry = consume(1, n_kv - 1, carry)
    for t in range(n_sub):
        o_ref[pl.ds(t * ts, ts), :] = (acc_ref[t] / carry[t][1]).T.astype(o_ref.dtype)


def _mla(q, k, vt, *, batch, seq, tq=2048, ts=512, tk=1024):
    n_q = seq // tq
    return pl.pallas_call(
        functools.partial(_mla_kernel, seq // tk, tk),
        grid=(batch, MLA_HEADS, n_q),
        in_specs=[
            pl.BlockSpec((tq, MLA_SLAB), lambda b, h, i: (b * n_q + i, h)),
            pl.BlockSpec((seq, MLA_SLAB), lambda b, h, i: (b, h)),
            pl.BlockSpec((V_DIM, seq), lambda b, h, i: (h, b)),
        ],
        out_specs=pl.BlockSpec((tq, V_DIM), lambda b, h, i: (b * n_q + i, h)),
        out_shape=jax.ShapeDtypeStruct((batch * seq, MLA_HEADS * V_DIM), BF16),
        scratch_shapes=[pltpu.VMEM((tq // ts, V_DIM, ts), F32), pltpu.VMEM((2, tq // ts, tk, ts), F32),
                        pltpu.VMEM((2, tq // ts, 1, ts), F32)],
        compiler_params=_params(("parallel", "parallel", "arbitrary")),
        name="mla",
    )(q, k, vt)


def _dilated_kernel(d, half, stream_len, q_ref, kp_ref, kc_ref, kn_ref, vp_ref, vc_ref, vn_ref,
                    o_ref, lse_ref, ks_ref, vs_ref):
    n_heads = q_ref.shape[0]
    tl = DIL_ROWS // d
    i = pl.program_id(2)
    for dst, before, cur, after in ((ks_ref, kp_ref, kc_ref, kn_ref), (vs_ref, vp_ref, vc_ref, vn_ref)):
        dst[:, :, 0:half] = before[...]
        dst[:, :, half:half + tl] = cur[...]
        dst[:, :, half + tl:] = after[...]

    chunk = 2 * half
    row = lax.broadcasted_iota(jnp.int32, (chunk, 2 * chunk), 0)
    col = lax.broadcasted_iota(jnp.int32, (chunk, 2 * chunk), 1)
    band = (col >= row) & (col <= row + 2 * half)
    shift = d.bit_length() - 1

    def unit_group(ug, carry):
        jobs = []
        for uu in range(DIL_UNITS_PER_TRIP):
            u = ug * DIL_UNITS_PER_TRIP + uu
            r = u & (d - 1)
            c = u >> shift
            row0 = pl.multiple_of(c * chunk, chunk)
            k_idx = i * tl + c * chunk - half + col
            mask = band & (k_idx >= 0) & (k_idx < stream_len)
            for h in range(n_heads):
                q = q_ref[h, r, pl.ds(row0, chunk), :]
                k = ks_ref[h, r, pl.ds(row0, 2 * chunk), :]
                s = lax.dot_general(q, k, _NT, preferred_element_type=F32) * DIL_SCALE
                jobs.append((h, r, row0, mask, s))
        for h, r, row0, mask, s in jobs:
            s = jnp.where(mask, s, NEG_INF)
            m = jnp.max(s, axis=1, keepdims=True)
            p = jnp.exp(s - m)
            denom = jnp.sum(p, axis=1, keepdims=True)
            pn = (p / denom).astype(BF16)
            o = jnp.dot(pn, vs_ref[h, r, pl.ds(row0, 2 * chunk), :], preferred_element_type=F32)
            start = r + d * row0
            o_ref[h, pl.ds(start, chunk, stride=d), :] = o
            lse_ref[h, pl.ds(start, chunk, stride=d), :] = jnp.broadcast_to(m + jnp.log(denom), (chunk, LANES))
        return carry

    lax.fori_loop(0, DIL_ROWS // chunk // DIL_UNITS_PER_TRIP, unit_group, 0)


def _dilated_group(qkv, *, group, window, dilation, batch, seq):
    d = dilation
    half = window // (2 * d)
    n_r = seq // DIL_ROWS
    tl = DIL_ROWS // d
    halo_per_step = tl // half
    n_halo = seq // d // half
    steps_per_group = HEADS_PER_GROUP // DIL_HEADS_PER_STEP
    blk = (DIL_HEADS_PER_STEP, None, d, tl, LANES)
    hblk = (DIL_HEADS_PER_STEP, None, d, half, LANES)

    def specs(kind):
        heads = lambda g: kind * steps_per_group + g
        cur = pl.BlockSpec(blk, lambda g, b, i: (heads(g), b, 0, i, 0))
        before = pl.BlockSpec(hblk, lambda g, b, i: (heads(g), b, 0, jnp.maximum(i * halo_per_step - 1, 0), 0))
        after = pl.BlockSpec(hblk, lambda g, b, i: (heads(g), b, 0,
                                                    jnp.minimum((i + 1) * halo_per_step, n_halo - 1), 0))
        return before, cur, after

    out_spec = pl.BlockSpec((DIL_HEADS_PER_STEP, DIL_ROWS, LANES), lambda g, b, i: (g, b * n_r + i, 0))
    out_shape = jax.ShapeDtypeStruct((HEADS_PER_GROUP, batch * seq, LANES), F32)
    return pl.pallas_call(
        functools.partial(_dilated_kernel, d, half, seq // d),
        grid=(steps_per_group, batch, n_r),
        in_specs=[specs(0)[1], *specs(1), *specs(2)],
        out_specs=[out_spec, out_spec],
        out_shape=[out_shape, out_shape],
        scratch_shapes=[pltpu.VMEM((DIL_HEADS_PER_STEP, d, tl + 2 * half, LANES), BF16),
                        pltpu.VMEM((DIL_HEADS_PER_STEP, d, tl + 2 * half, LANES), BF16)],
        compiler_params=_params(("parallel", "parallel", "parallel")),
        name="dilated_g%d" % group,
    )(*([qkv] * 7))


def _merge_kernel(oa_ref, o0_ref, o1_ref, o2_ref, l0_ref, l1_ref, l2_ref, ga_ref, gb_ref, y1_ref,
                  wa_ref, wb_ref, wo_ref, g_ref, b_ref, y_ref):
    heads = []
    for h in range(HEADS_PER_GROUP):
        l0, l1, l2 = l0_ref[h], l1_ref[h], l2_ref[h]
        m = jnp.maximum(jnp.maximum(l0, l1), l2)
        e0, e1, e2 = jnp.exp(l0 - m), jnp.exp(l1 - m), jnp.exp(l2 - m)
        denom = e0 + e1 + e2
        heads.append(((e0 / denom) * o0_ref[h] + (e1 / denom) * o1_ref[h] + (e2 / denom) * o2_ref[h]).astype(BF16))
    out_a = oa_ref[...]
    out_b = jnp.concatenate(heads, axis=1)
    cols = [slice(c, c + PROJ_SUB) for c in range(0, D_MODEL, PROJ_SUB)]
    branches = [(jnp.dot(out_a, wa_ref[:, sl], preferred_element_type=F32),
                 jnp.dot(out_b, wb_ref[:, sl], preferred_element_type=F32)) for sl in cols]
    mix = None
    for sl, (branch_a, branch_b) in zip(cols, branches):
        merged = (ga_ref[:, sl] * branch_a + gb_ref[:, sl] * branch_b).astype(BF16)
        part = jnp.dot(merged, wo_ref[sl, :], preferred_element_type=F32)
        mix = part if mix is None else mix + part
    y_ref[...] = _layer_norm(ALPHA * y1_ref[...] + mix, g_ref[...], b_ref[...])


def _merge(out_a, outs, lses, gates, y1, wa, wb, wo, g, b, *, tm=256):
    rows = y1.shape[0]
    row_spec = lambda w, c=0: pl.BlockSpec((tm, w), lambda i: (i, c))
    full = lambda a: pl.BlockSpec(a.shape, lambda i: (0, 0), pipeline_mode=pl.Buffered(1))
    return pl.pallas_call(
        _merge_kernel,
        grid=(rows // tm,),
        in_specs=[row_spec(MLA_HEADS * V_DIM)]
                 + [pl.BlockSpec((HEADS_PER_GROUP, tm, LANES), lambda i: (0, i, 0))] * 6
                 + [row_spec(D_MODEL, 0), row_spec(D_MODEL, 1), row_spec(D_MODEL)]
                 + [full(wa), full(wb), full(wo), full(g), full(b)],
        out_specs=row_spec(D_MODEL),
        out_shape=jax.ShapeDtypeStruct((rows, D_MODEL), F32),
        compiler_params=_params(("parallel",)),
        name="merge",
    )(out_a, *outs, *lses, gates, gates, y1, wa, wb, wo, g, b)


def _ffn_in_kernel(w_ref, wg_ref, wu_ref):
    w = w_ref[...]
    pad = jnp.zeros((w.shape[0], D_FF_PAD - D_FF), BF16)
    for dst, part in ((wg_ref, w[:, :D_FF]), (wu_ref, w[:, D_FF:])):
        dst[:, :D_FF] = part.astype(BF16)
        dst[:, D_FF:] = pad


def _ffn_out_kernel(n_valid, w_ref, o_ref):
    @pl.when(pl.program_id(0) < n_valid)
    def _():
        o_ref[...] = w_ref[...].astype(BF16)

    @pl.when(pl.program_id(0) >= n_valid)
    def _():
        o_ref[...] = jnp.zeros_like(o_ref)


def _ffn_weights(w_in, w_out, *, tr=64, tc=LANES):
    half = jax.ShapeDtypeStruct((D_MODEL, D_FF_PAD), BF16)
    wg, wu = pl.pallas_call(
        _ffn_in_kernel,
        grid=(D_MODEL // tr,),
        in_specs=[pl.BlockSpec((tr, 2 * D_FF), lambda i: (i, 0))],
        out_specs=[pl.BlockSpec((tr, D_FF_PAD), lambda i: (i, 0))] * 2,
        out_shape=[half, half],
        compiler_params=_params(("parallel",)),
        name="ffn_w_in",
    )(w_in)
    n_valid = D_FF // tc
    wo = pl.pallas_call(
        functools.partial(_ffn_out_kernel, n_valid),
        grid=(D_FF_PAD // tc,),
        in_specs=[pl.BlockSpec((tc, D_MODEL), lambda i: (jnp.minimum(i, n_valid - 1), 0))],
        out_specs=pl.BlockSpec((tc, D_MODEL), lambda i: (i, 0)),
        out_shape=jax.ShapeDtypeStruct((D_FF_PAD, D_MODEL), BF16),
        compiler_params=_params(("parallel",)),
        name="ffn_w_out",
    )(w_out)
    return wg, wu, wo


def _rope_tables(seq):
    pos = jnp.arange(seq, dtype=F32)[:, None]

    def angles(d):
        inv_freq = ROPE_THETA ** (-jnp.arange(0, d, 2, dtype=F32) / d)
        ang = pos * inv_freq[None, :]
        return jnp.cos(ang), jnp.sin(ang)

    cos, sin = angles(DIL_HEAD_DIM)
    dil = (jnp.concatenate([cos, cos], axis=1), jnp.concatenate([-sin, sin], axis=1))
    cos, sin = angles(QK_ROPE)
    zeros = jnp.zeros_like(cos)
    pad = jnp.zeros((seq, LANES - QK_ROPE), F32)
    mla = (jnp.concatenate([cos, cos, pad], axis=1),
           jnp.concatenate([-sin, zeros, pad], axis=1),
           jnp.concatenate([zeros, sin, pad], axis=1))
    return dil, mla


def _prepare(ffn1_w_in, ffn1_w_out, ln1_g, ln1_b, w_in_mix, b_gate, q_norm_g, w_uq, kv_norm_g, w_ukv,
             w_branch_a, w_branch_b, w_out_mix, ln2_g, ln2_b, ffn2_w_in, ffn2_w_out, ln3_g, ln3_b):
    ffn = _ffn_weights
    row = lambda a: a.reshape(1, -1)
    o_lat = Q_LORA + KV_LORA + QK_ROPE
    o_qk = o_lat + 2 * DIL_WIDTH
    o_v = o_qk + DIL_WIDTH
    w_lat = jnp.pad(w_in_mix[:, :o_lat], ((0, 0), (0, LAT_WIDTH - o_lat))).astype(BF16)
    uq = w_uq.reshape(Q_LORA, MLA_HEADS, QK_NOPE + QK_ROPE)
    ukv = w_ukv.reshape(KV_LORA, MLA_HEADS, QK_NOPE + V_DIM)
    flat = lambda a: a.reshape(a.shape[0], -1).astype(BF16)
    return dict(
        ffn1=ffn(ffn1_w_in, ffn1_w_out), ln1=(row(ln1_g), row(ln1_b)),
        ffn2=ffn(ffn2_w_in, ffn2_w_out), ln3=(row(ln3_g), row(ln3_b)),
        w_lat=w_lat,
        w_dil=w_in_mix[:, o_lat:o_v].astype(BF16),
        w_gate=w_in_mix[:, o_v:].astype(BF16),
        b_gate=row(b_gate),
        q_norm_g=row(q_norm_g), kv_norm_g=row(kv_norm_g),
        wqn=flat(uq[:, :, :QK_NOPE]),
        wqp=flat(jnp.pad(uq[:, :, QK_NOPE:], ((0, 0), (0, 0), (0, LANES - QK_ROPE)))),
        wkn=flat(ukv[:, :, :QK_NOPE]),
        wvt=flat(ukv[:, :, QK_NOPE:]).T,
        wa=w_branch_a.astype(BF16), wb=w_branch_b.astype(BF16), wo=w_out_mix.astype(BF16),
        ln2=(row(ln2_g), row(ln2_b)),
    )


def _encoder_layer(x, p):
    batch, seq, _ = x.shape
    x2 = x.reshape(batch * seq, D_MODEL)
    dil_tables, mla_tables = _rope_tables(seq)

    y1 = _ffn_ln(x2, *p["ffn1"], *p["ln1"])

    lat, y1_bf = _proj(y1, p["w_lat"], F32, tn=LAT_WIDTH)
    qkv_d = _proj(y1_bf, p["w_dil"], BF16, mode="dilated", extras=dil_tables, seq=seq, tn=DIL_WIDTH)
    gates = _proj(y1_bf, p["w_gate"], F32, mode="gate", extras=(p["b_gate"],), tn=D_MODEL)

    q, k, vt = _latent(lat, p["q_norm_g"], p["kv_norm_g"], p["wqn"], p["wqp"], p["wkn"], p["wvt"],
                       mla_tables, seq=seq)
    out_a = _mla(q, k, vt, batch=batch, seq=seq)

    outs, lses = [], []
    for group, (window, dilation) in enumerate(DIL_PATTERNS):
        o, lse = _dilated_group(qkv_d[group], group=group, window=window, dilation=dilation, batch=batch, seq=seq)
        outs.append(o)
        lses.append(lse)

    y2 = _merge(out_a, outs, lses, gates, y1, p["wa"], p["wb"], p["wo"], *p["ln2"])
    y3 = _ffn_ln(y2, *p["ffn2"], *p["ln3"])
    return y3.reshape(batch, seq, D_MODEL)


def kernel(x_prompt, x_sample, ffn1_w_in, ffn1_w_out, ln1_g, ln1_b, w_in_mix, b_gate, q_norm_g, w_uq, kv_norm_g, w_ukv, w_branch_a, w_branch_b, w_out_mix, ln2_g, ln2_b, ffn2_w_in, ffn2_w_out, ln3_g, ln3_b):
    weights = (ffn1_w_in, ffn1_w_out, ln1_g, ln1_b, w_in_mix, b_gate, q_norm_g, w_uq, kv_norm_g, w_ukv,
               w_branch_a, w_branch_b, w_out_mix, ln2_g, ln2_b, ffn2_w_in, ffn2_w_out, ln3_g, ln3_b)
    y_prompt, y_sample = x_prompt, x_sample
    for layer in range(DEPTH):
        p = _prepare(*(w[layer] for w in weights))
        y_prompt = _encoder_layer(y_prompt, p)
        y_sample = _encoder_layer(y_sample, p)
    return (y_prompt, y_sample)
```

```python
import functools

import jax
import jax.numpy as jnp
from jax import lax
from jax.experimental import pallas as pl
from jax.experimental.pallas import tpu as pltpu

D_MODEL = 2048
DEPTH = 1
MLA_HEADS = 8
Q_LORA = 512
KV_LORA = 512
QK_NOPE = 128
QK_ROPE = 64
V_DIM = 128
DIL_PATTERNS = ((128, 1), (512, 4), (2048, 16))
HEADS_PER_GROUP = 4
DIL_HEAD_DIM = 128
DIL_WIDTH = len(DIL_PATTERNS) * HEADS_PER_GROUP * DIL_HEAD_DIM
D_FF = 5504
ROPE_THETA = 10000.0
LN_EPS = 1e-5
RMS_EPS = 1e-6
NEG_INF = -1e30
ALPHA = (2 * DEPTH) ** 0.25
MLA_SCALE = (QK_NOPE + QK_ROPE) ** -0.5
LOG2_E = 1.4426950408889634
MLA_Q_SCALE = MLA_SCALE * LOG2_E
DIL_KINDS = 3
DIL_SCALE = DIL_HEAD_DIM ** -0.5

LANES = 128
D_FF_PAD = 5632
FF_CHUNK = 512
LAT_WIDTH = Q_LORA + KV_LORA + LANES
MLA_SLAB = 2 * LANES
GROUP_WIDTH = HEADS_PER_GROUP * DIL_HEAD_DIM
PROJ_SUB = 512
DIL_ROWS = 2048
DIL_HEADS_PER_STEP = 4
DIL_UNITS_PER_TRIP = 2
VMEM_LIMIT = 56 * 1024 * 1024
FFN_VMEM_LIMIT = 60 * 1024 * 1024

BF16 = jnp.bfloat16
F32 = jnp.float32
_NT = (((1,), (1,)), ((), ()))


def _params(semantics, vmem_limit=VMEM_LIMIT):
    return pltpu.CompilerParams(dimension_semantics=semantics, vmem_limit_bytes=vmem_limit)


def _layer_norm(v, g, b, scale=1.0):
    mu = jnp.mean(v, axis=-1, keepdims=True)
    c = v - mu
    var = jnp.mean(c * c, axis=-1, keepdims=True)
    return c * (scale * lax.rsqrt(scale * scale * var + LN_EPS)) * g + b


def _ffn_ln_kernel(n_chunks, x_ref, wg_ref, wu_ref, wo_ref, g_ref, b_ref, y_ref, xbf_ref):
    k = pl.program_id(1)

    def chunk(xb):
        gate = jnp.dot(xb, wg_ref[...], preferred_element_type=F32)
        up = jnp.dot(xb, wu_ref[...], preferred_element_type=F32)
        act = gate * jax.nn.sigmoid(gate) * up
        return jnp.dot(act.astype(BF16), wo_ref[...], preferred_element_type=F32)

    @pl.when(k == 0)
    def _():
        x = x_ref[...]
        xb = x.astype(BF16)
        xbf_ref[...] = xb
        y_ref[...] = (2.0 * ALPHA) * x + chunk(xb)

    @pl.when(k > 0)
    def _():
        y_ref[...] += chunk(xbf_ref[...])

    @pl.when(k == n_chunks - 1)
    def _():
        y_ref[...] = _layer_norm(y_ref[...], g_ref[...], b_ref[...], scale=0.5)


def _ffn_ln(x, wg, wu, wo, g, b, *, tm=1024):
    rows = x.shape[0]
    n_chunks = D_FF_PAD // FF_CHUNK
    return pl.pallas_call(
        functools.partial(_ffn_ln_kernel, n_chunks),
        grid=(rows // tm, n_chunks),
        in_specs=[
            pl.BlockSpec((tm, D_MODEL), lambda i, k: (i, 0)),
            pl.BlockSpec((D_MODEL, FF_CHUNK), lambda i, k: (0, k)),
            pl.BlockSpec((D_MODEL, FF_CHUNK), lambda i, k: (0, k)),
            pl.BlockSpec((FF_CHUNK, D_MODEL), lambda i, k: (k, 0)),
            pl.BlockSpec((1, D_MODEL), lambda i, k: (0, 0)),
            pl.BlockSpec((1, D_MODEL), lambda i, k: (0, 0)),
        ],
        out_specs=pl.BlockSpec((tm, D_MODEL), lambda i, k: (i, 0)),
        out_shape=jax.ShapeDtypeStruct((rows, D_MODEL), F32),
        scratch_shapes=[pltpu.VMEM((tm, D_MODEL), BF16)],
        compiler_params=_params(("parallel", "arbitrary"), FFN_VMEM_LIMIT),
        name="ffn_ln",
    )(x, wg, wu, wo, g, b)


def _proj_kernel(mode, x_ref, w_ref, *rest):
    tn = w_ref.shape[1]
    starts = range(0, tn, PROJ_SUB)

    def products(x=None):
        x = x_ref[...] if x is None else x
        return [(c, jnp.dot(x, w_ref[:, c:min(c + PROJ_SUB, tn)], preferred_element_type=F32)) for c in starts]

    if mode == "dilated":
        cos_ref, sin_ref, *o_refs, slab_ref = rest
        n_rope_steps = 2 * DIL_WIDTH // tn
        tm = x_ref.shape[0]

        def emit(rope):
            if rope:
                cos = cos_ref[...]
                sin = sin_ref[...]
            for c, z in products():
                group = c // PROJ_SUB
                d = DIL_PATTERNS[group][1]
                o_ref = o_refs[group]
                for h in range(HEADS_PER_GROUP):
                    zh = z[:, h * LANES:(h + 1) * LANES]
                    if rope:
                        zh = zh * cos + pltpu.roll(zh, LANES // 2, 1) * sin
                    if d == 1:
                        o_ref[h, 0] = zh.astype(BF16)
                        continue
                    slab = group * HEADS_PER_GROUP + h
                    slab_ref[slab] = zh
                    for r in range(d):
                        o_ref[h, r] = slab_ref[slab, pl.ds(r, tm // d, stride=d), :].astype(BF16)

        pl.when(pl.program_id(1) < n_rope_steps)(functools.partial(emit, True))
        pl.when(pl.program_id(1) >= n_rope_steps)(functools.partial(emit, False))
    elif mode == "gate":
        b_ref, o_ref = rest
        for c, z in products():
            sl = slice(c, c + z.shape[1])
            o_ref[:, sl] = jax.nn.sigmoid(z + b_ref[:, sl]).astype(o_ref.dtype)
    else:
        o_ref, xbf_ref = rest
        x = x_ref[...].astype(BF16)
        xbf_ref[...] = x
        for c, z in products(x):
            o_ref[:, c:c + z.shape[1]] = z.astype(o_ref.dtype)


def _proj(x, w, out_dtype, *, mode="cast", extras=(), seq=None, tm=1024, tn=512):
    rows = x.shape[0]
    n = w.shape[1]
    tn = min(tn, n)
    in_specs = [
        pl.BlockSpec((tm, D_MODEL), lambda i, j: (i, 0)),
        pl.BlockSpec((D_MODEL, tn), lambda i, j: (0, j)),
    ]
    out_specs = pl.BlockSpec((tm, tn), lambda i, j: (i, j))
    out_shape = jax.ShapeDtypeStruct((rows, n), out_dtype)
    scratch_shapes = []
    if mode == "dilated":
        assert tn == DIL_WIDTH and PROJ_SUB == GROUP_WIDTH
        tiles_per_seq = seq // tm
        in_specs += [pl.BlockSpec((tm, LANES), lambda i, j: (i % tiles_per_seq, 0))] * 2
        out_specs = [
            pl.BlockSpec((HEADS_PER_GROUP, None, d, tm // d, LANES),
                         lambda i, j: (j, i // tiles_per_seq, 0, i % tiles_per_seq, 0))
            for _, d in DIL_PATTERNS]
        out_shape = [jax.ShapeDtypeStruct((DIL_KINDS * HEADS_PER_GROUP, rows // seq, d, seq // d, LANES), out_dtype)
                     for _, d in DIL_PATTERNS]
        scratch_shapes = [pltpu.VMEM((len(DIL_PATTERNS) * HEADS_PER_GROUP, tm, LANES), F32)]
    elif mode == "gate":
        in_specs += [pl.BlockSpec((1, tn), lambda i, j: (0, j))]
    else:
        assert tn == n
        out_specs = [out_specs, pl.BlockSpec((tm, D_MODEL), lambda i, j: (i, 0))]
        out_shape = [out_shape, jax.ShapeDtypeStruct((rows, D_MODEL), BF16)]
    return pl.pallas_call(
        functools.partial(_proj_kernel, mode),
        grid=(rows // tm, n // tn),
        in_specs=in_specs,
        out_specs=out_specs,
        out_shape=out_shape,
        scratch_shapes=scratch_shapes,
        compiler_params=_params(("parallel", "arbitrary")),
        name="proj_" + mode,
    )(x, w, *extras)


def _latent_kernel(lat_ref, qg_ref, kvg_ref, wqn_ref, wqp_ref, wkn_ref, wvt_ref,
                   cos_ref, sin_lo_ref, sin_hi_ref, q_ref, k_ref, vt_ref):
    lat = lat_ref[...]
    c_q = lat[:, :Q_LORA]
    c_kv = lat[:, Q_LORA:Q_LORA + KV_LORA]
    k_rope = lat[:, Q_LORA + KV_LORA:]

    def rms(x, g):
        return (x * lax.rsqrt(jnp.mean(x * x, axis=-1, keepdims=True) + RMS_EPS) * g).astype(BF16)

    cos = cos_ref[...]
    sin_lo = sin_lo_ref[...]
    sin_hi = sin_hi_ref[...]

    def rope64(x):
        quarter = QK_ROPE // 2
        return (x * cos + pltpu.roll(x, LANES - quarter, 1) * sin_lo
                + pltpu.roll(x, quarter, 1) * sin_hi)

    nq = rms(c_q, qg_ref[...])
    nkv = rms(c_kv, kvg_ref[...])
    q_nope = jnp.dot(nq, wqn_ref[...], preferred_element_type=F32)
    q_pe = jnp.dot(nq, wqp_ref[...], preferred_element_type=F32)
    k_nope = jnp.dot(nkv, wkn_ref[...], preferred_element_type=F32)
    vt = lax.dot_general(wvt_ref[...], nkv, _NT, preferred_element_type=F32)
    k_pe = rope64(k_rope).astype(BF16)
    for h in range(MLA_HEADS):
        src = slice(h * LANES, (h + 1) * LANES)
        lo = slice(h * MLA_SLAB, h * MLA_SLAB + LANES)
        hi = slice(h * MLA_SLAB + LANES, (h + 1) * MLA_SLAB)
        q_ref[:, lo] = (q_nope[:, src] * MLA_Q_SCALE).astype(BF16)
        q_ref[:, hi] = (rope64(q_pe[:, src]) * MLA_Q_SCALE).astype(BF16)
        k_ref[:, lo] = k_nope[:, src].astype(BF16)
        k_ref[:, hi] = k_pe
    vt_ref[...] = vt.astype(BF16)


def _latent(lat, qg, kvg, wqn, wqp, wkn, wvt, tables, *, seq, tm=512):
    rows = lat.shape[0]
    tiles_per_seq = seq // tm
    width = MLA_HEADS * LANES
    row_spec = lambda w: pl.BlockSpec((tm, w), lambda i: (i, 0))
    full = lambda a: pl.BlockSpec(a.shape, lambda i: (0, 0), pipeline_mode=pl.Buffered(1))
    table_spec = pl.BlockSpec((tm, LANES), lambda i: (i % tiles_per_seq, 0))
    return pl.pallas_call(
        _latent_kernel,
        grid=(rows // tm,),
        in_specs=[row_spec(LAT_WIDTH), full(qg), full(kvg), full(wqn), full(wqp), full(wkn), full(wvt),
                  table_spec, table_spec, table_spec],
        out_specs=[row_spec(MLA_HEADS * MLA_SLAB), row_spec(MLA_HEADS * MLA_SLAB),
                   pl.BlockSpec((width, tm), lambda i: (0, i))],
        out_shape=[jax.ShapeDtypeStruct((rows, MLA_HEADS * MLA_SLAB), BF16),
                   jax.ShapeDtypeStruct((rows, MLA_HEADS * MLA_SLAB), BF16),
                   jax.ShapeDtypeStruct((width, rows), BF16)],
        compiler_params=_params(("parallel",)),
        name="latent",
    )(lat, qg, kvg, wqn, wqp, wkn, wvt, *tables)


def _mla_kernel(n_kv, tk, q_ref, k_ref, vt_ref, o_ref, acc_ref, s_ref, mblk_ref):
    n_sub, _, ts = acc_ref.shape
    acc_ref[...] = jnp.zeros_like(acc_ref)

    def scores_into(slot, j):
        k = k_ref[pl.ds(pl.multiple_of(j * tk, tk), tk), :]
        for t in range(n_sub):
            s = lax.dot_general(k, q_ref[pl.ds(t * ts, ts), :], _NT, preferred_element_type=F32)
            s_ref[slot, t] = s
            mblk_ref[slot, t] = jnp.max(s, axis=0, keepdims=True)

    def consume(slot, j, carry):
        vt = vt_ref[:, pl.ds(pl.multiple_of(j * tk, tk), tk)]
        out = []
        for t in range(n_sub):
            m_prev, l_prev = carry[t]
            m_new = jnp.maximum(m_prev, mblk_ref[slot, t])
            alpha = jnp.exp2(m_prev - m_new)
            p = jnp.exp2(s_ref[slot, t] - m_new)
            l_new = alpha * l_prev + jnp.sum(p, axis=0, keepdims=True)
            acc_ref[t] = alpha * acc_ref[t] + jnp.dot(vt, p.astype(BF16), preferred_element_type=F32)
            out.append((m_new, l_new))
        return tuple(out)

    def body(jj, carry):
        j = 2 * jj
        scores_into(1, j + 1)
        carry = consume(0, j, carry)
        scores_into(0, j + 2)
        return consume(1, j + 1, carry)

    scores_into(0, 0)
    init = tuple((jnp.full((1, ts), -jnp.inf, F32), jnp.zeros((1, ts), F32)) for _ in range(n_sub))
    carry = lax.fori_loop(0, n_kv // 2 - 1, body, init)
    scores_into(1, n_kv - 1)
    carry = consume(0, n_kv - 2, carry)
    carry = consume(1, n_kv - 1, carry)
    for t in range(n_sub):
        o_ref[pl.ds(t * ts, ts), :] = (acc_ref[t] / carry[t][1]).T.astype(o_ref.dtype)


def _mla(q, k, vt, *, batch, seq, tq=2048, ts=512, tk=1024):
    n_q = seq // tq
    return pl.pallas_call(
        functools.partial(_mla_kernel, seq // tk, tk),
        grid=(batch, MLA_HEADS, n_q),
        in_specs=[
            pl.BlockSpec((tq, MLA_SLAB), lambda b, h, i: (b * n_q + i, h)),
            pl.BlockSpec((seq, MLA_SLAB), lambda b, h, i: (b, h)),
            pl.BlockSpec((V_DIM, seq), lambda b, h, i: (h, b)),
        ],
        out_specs=pl.BlockSpec((tq, V_DIM), lambda b, h, i: (b * n_q + i, h)),
        out_shape=jax.ShapeDtypeStruct((batch * seq, MLA_HEADS * V_DIM), BF16),
        scratch_shapes=[pltpu.VMEM((tq // ts, V_DIM, ts), F32), pltpu.VMEM((2, tq // ts, tk, ts), F32),
                        pltpu.VMEM((2, tq // ts, 1, ts), F32)],
        compiler_params=_params(("parallel", "parallel", "arbitrary")),
        name="mla",
    )(q, k, vt)


def _dilated_kernel(d, half, stream_len, q_ref, kp_ref, kc_ref, kn_ref, vp_ref, vc_ref, vn_ref,
                    o_ref, lse_ref, ks_ref, vs_ref):
    n_heads = q_ref.shape[0]
    tl = DIL_ROWS // d
    i = pl.program_id(2)
    for dst, before, cur, after in ((ks_ref, kp_ref, kc_ref, kn_ref), (vs_ref, vp_ref, vc_ref, vn_ref)):
        dst[:, :, 0:half] = before[...]
        dst[:, :, half:half + tl] = cur[...]
        dst[:, :, half + tl:] = after[...]

    chunk = 2 * half
    row = lax.broadcasted_iota(jnp.int32, (chunk, 2 * chunk), 0)
    col = lax.broadcasted_iota(jnp.int32, (chunk, 2 * chunk), 1)
    band = (col >= row) & (col <= row + 2 * half)
    shift = d.bit_length() - 1

    def unit_group(ug, carry):
        jobs = []
        for uu in range(DIL_UNITS_PER_TRIP):
            u = ug * DIL_UNITS_PER_TRIP + uu
            r = u & (d - 1)
            c = u >> shift
            row0 = pl.multiple_of(c * chunk, chunk)
            k_idx = i * tl + c * chunk - half + col
            mask = band & (k_idx >= 0) & (k_idx < stream_len)
            for h in range(n_heads):
                q = q_ref[h, r, pl.ds(row0, chunk), :]
                k = ks_ref[h, r, pl.ds(row0, 2 * chunk), :]
                s = lax.dot_general(q, k, _NT, preferred_element_type=F32) * DIL_SCALE
                jobs.append((h, r, row0, mask, s))
        for h, r, row0, mask, s in jobs:
            s = jnp.where(mask, s, NEG_INF)
            m = jnp.max(s, axis=1, keepdims=True)
            p = jnp.exp(s - m)
            denom = jnp.sum(p, axis=1, keepdims=True)
            pn = (p / denom).astype(BF16)
            o = jnp.dot(pn, vs_ref[h, r, pl.ds(row0, 2 * chunk), :], preferred_element_type=F32)
            start = r + d * row0
            o_ref[h, pl.ds(start, chunk, stride=d), :] = o
            lse_ref[h, pl.ds(start, chunk, stride=d), :] = jnp.broadcast_to(m + jnp.log(denom), (chunk, LANES))
        return carry

    lax.fori_loop(0, DIL_ROWS // chunk // DIL_UNITS_PER_TRIP, unit_group, 0)


def _dilated_group(qkv, *, group, window, dilation, batch, seq):
    d = dilation
    half = window // (2 * d)
    n_r = seq // DIL_ROWS
    tl = DIL_ROWS // d
    halo_per_step = tl // half
    n_halo = seq // d // half
    steps_per_group = HEADS_PER_GROUP // DIL_HEADS_PER_STEP
    blk = (DIL_HEADS_PER_STEP, None, d, tl, LANES)
    hblk = (DIL_HEADS_PER_STEP, None, d, half, LANES)

    def specs(kind):
        heads = lambda g: kind * steps_per_group + g
        cur = pl.BlockSpec(blk, lambda g, b, i: (heads(g), b, 0, i, 0))
        before = pl.BlockSpec(hblk, lambda g, b, i: (heads(g), b, 0, jnp.maximum(i * halo_per_step - 1, 0), 0))
        after = pl.BlockSpec(hblk, lambda g, b, i: (heads(g), b, 0,
                                                    jnp.minimum((i + 1) * halo_per_step, n_halo - 1), 0))
        return before, cur, after

    out_spec = pl.BlockSpec((DIL_HEADS_PER_STEP, DIL_ROWS, LANES), lambda g, b, i: (g, b * n_r + i, 0))
    out_shape = jax.ShapeDtypeStruct((HEADS_PER_GROUP, batch * seq, LANES), F32)
    return pl.pallas_call(
        functools.partial(_dilated_kernel, d, half, seq // d),
        grid=(steps_per_group, batch, n_r),
        in_specs=[specs(0)[1], *specs(1), *specs(2)],
        out_specs=[out_spec, out_spec],
        out_shape=[out_shape, out_shape],
        scratch_shapes=[pltpu.VMEM((DIL_HEADS_PER_STEP, d, tl + 2 * half, LANES), BF16),
                        pltpu.VMEM((DIL_HEADS_PER_STEP, d, tl + 2 * half, LANES), BF16)],
        compiler_params=_params(("parallel", "parallel", "parallel")),
        name="dilated_g%d" % group,
    )(*([qkv] * 7))


def _merge_kernel(oa_ref, o0_ref, o1_ref, o2_ref, l0_ref, l1_ref, l2_ref, ga_ref, gb_ref, y1_ref,
                  wa_ref, wb_ref, wo_ref, g_ref, b_ref, y_ref):
    heads = []
    for h in range(HEADS_PER_GROUP):
        l0, l1, l2 = l0_ref[h], l1_ref[h], l2_ref[h]
        m = jnp.maximum(jnp.maximum(l0, l1), l2)
        e0, e1, e2 = jnp.exp(l0 - m), jnp.exp(l1 - m), jnp.exp(l2 - m)
        denom = e0 + e1 + e2
        heads.append(((e0 / denom) * o0_ref[h] + (e1 / denom) * o1_ref[h] + (e2 / denom) * o2_ref[h]).astype(BF16))
    out_a = oa_ref[...]
    out_b = jnp.concatenate(heads, axis=1)
    cols = [slice(c, c + PROJ_SUB) for c in range(0, D_MODEL, PROJ_SUB)]
    branches = [(jnp.dot(out_a, wa_ref[:, sl], preferred_element_type=F32),
                 jnp.dot(out_b, wb_ref[:, sl], preferred_element_type=F32)) for sl in cols]
    mix = None
    for sl, (branch_a, branch_b) in zip(cols, branches):
        merged = (ga_ref[:, sl] * branch_a + gb_ref[:, sl] * branch_b).astype(BF16)
        part = jnp.dot(merged, wo_ref[sl, :], preferred_element_type=F32)
        mix = part if mix is None else mix + part
    y_ref[...] = _layer_norm(ALPHA * y1_ref[...] + mix, g_ref[...], b_ref[...])


def _merge(out_a, outs, lses, gates, y1, wa, wb, wo, g, b, *, tm=256):
    rows = y1.shape[0]
    row_spec = lambda w, c=0: pl.BlockSpec((tm, w), lambda i: (i, c))
    full = lambda a: pl.BlockSpec(a.shape, lambda i: (0, 0), pipeline_mode=pl.Buffered(1))
    return pl.pallas_call(
        _merge_kernel,
        grid=(rows // tm,),
        in_specs=[row_spec(MLA_HEADS * V_DIM)]
                 + [pl.BlockSpec((HEADS_PER_GROUP, tm, LANES), lambda i: (0, i, 0))] * 6
                 + [row_spec(D_MODEL, 0), row_spec(D_MODEL, 1), row_spec(D_MODEL)]
                 + [full(wa), full(wb), full(wo), full(g), full(b)],
        out_specs=row_spec(D_MODEL),
        out_shape=jax.ShapeDtypeStruct((rows, D_MODEL), F32),
        compiler_params=_params(("parallel",)),
        name="merge",
    )(out_a, *outs, *lses, gates, gates, y1, wa, wb, wo, g, b)


def _ffn_in_kernel(w_ref, wg_ref, wu_ref):
    w = w_ref[...]
    pad = jnp.zeros((w.shape[0], D_FF_PAD - D_FF), BF16)
    for dst, part in ((wg_ref, w[:, :D_FF]), (wu_ref, w[:, D_FF:])):
        dst[:, :D_FF] = part.astype(BF16)
        dst[:, D_FF:] = pad


def _ffn_out_kernel(n_valid, w_ref, o_ref):
    @pl.when(pl.program_id(0) < n_valid)
    def _():
        o_ref[...] = w_ref[...].astype(BF16)

    @pl.when(pl.program_id(0) >= n_valid)
    def _():
        o_ref[...] = jnp.zeros_like(o_ref)


def _ffn_weights(w_in, w_out, *, tr=64, tc=LANES):
    half = jax.ShapeDtypeStruct((D_MODEL, D_FF_PAD), BF16)
    wg, wu = pl.pallas_call(
        _ffn_in_kernel,
        grid=(D_MODEL // tr,),
        in_specs=[pl.BlockSpec((tr, 2 * D_FF), lambda i: (i, 0))],
        out_specs=[pl.BlockSpec((tr, D_FF_PAD), lambda i: (i, 0))] * 2,
        out_shape=[half, half],
        compiler_params=_params(("parallel",)),
        name="ffn_w_in",
    )(w_in)
    n_valid = D_FF // tc
    wo = pl.pallas_call(
        functools.partial(_ffn_out_kernel, n_valid),
        grid=(D_FF_PAD // tc,),
        in_specs=[pl.BlockSpec((tc, D_MODEL), lambda i: (jnp.minimum(i, n_valid - 1), 0))],
        out_specs=pl.BlockSpec((tc, D_MODEL), lambda i: (i, 0)),
        out_shape=jax.ShapeDtypeStruct((D_FF_PAD, D_MODEL), BF16),
        compiler_params=_params(("parallel",)),
        name="ffn_w_out",
    )(w_out)
    return wg, wu, wo


def _rope_tables(seq):
    pos = jnp.arange(seq, dtype=F32)[:, None]

    def angles(d):
        inv_freq = ROPE_THETA ** (-jnp.arange(0, d, 2, dtype=F32) / d)
        ang = pos * inv_freq[None, :]
        return jnp.cos(ang), jnp.sin(ang)

    cos, sin = angles(DIL_HEAD_DIM)
    dil = (jnp.concatenate([cos, cos], axis=1), jnp.concatenate([-sin, sin], axis=1))
    cos, sin = angles(QK_ROPE)
    zeros = jnp.zeros_like(cos)
    pad = jnp.zeros((seq, LANES - QK_ROPE), F32)
    mla = (jnp.concatenate([cos, cos, pad], axis=1),
           jnp.concatenate([-sin, zeros, pad], axis=1),
           jnp.concatenate([zeros, sin, pad], axis=1))
    return dil, mla


def _prepare(ffn1_w_in, ffn1_w_out, ln1_g, ln1_b, w_in_mix, b_gate, q_norm_g, w_uq, kv_norm_g, w_ukv,
             w_branch_a, w_branch_b, w_out_mix, ln2_g, ln2_b, ffn2_w_in, ffn2_w_out, ln3_g, ln3_b):
    ffn = _ffn_weights
    row = lambda a: a.reshape(1, -1)
    o_lat = Q_LORA + KV_LORA + QK_ROPE
    o_qk = o_lat + 2 * DIL_WIDTH
    o_v = o_qk + DIL_WIDTH
    w_lat = jnp.pad(w_in_mix[:, :o_lat], ((0, 0), (0, LAT_WIDTH - o_lat))).astype(BF16)
    uq = w_uq.reshape(Q_LORA, MLA_HEADS, QK_NOPE + QK_ROPE)
    ukv = w_ukv.reshape(KV_LORA, MLA_HEADS, QK_NOPE + V_DIM)
    flat = lambda a: a.reshape(a.shape[0], -1).astype(BF16)
    return dict(
        ffn1=ffn(ffn1_w_in, ffn1_w_out), ln1=(row(ln1_g), row(ln1_b)),
        ffn2=ffn(ffn2_w_in, ffn2_w_out), ln3=(row(ln3_g), row(ln3_b)),
        w_lat=w_lat,
        w_dil=w_in_mix[:, o_lat:o_v].astype(BF16),
        w_gate=w_in_mix[:, o_v:].astype(BF16),
        b_gate=row(b_gate),
        q_norm_g=row(q_norm_g), kv_norm_g=row(kv_norm_g),
        wqn=flat(uq[:, :, :QK_NOPE]),
        wqp=flat(jnp.pad(uq[:, :, QK_NOPE:], ((0, 0), (0, 0), (0, LANES - QK_ROPE)))),
        wkn=flat(ukv[:, :, :QK_NOPE]),
        wvt=flat(ukv[:, :, QK_NOPE:]).T,
        wa=w_branch_a.astype(BF16), wb=w_branch_b.astype(BF16), wo=w_out_mix.astype(BF16),
        ln2=(row(ln2_g), row(ln2_b)),
    )


def _encoder_layer(x, p, tables):
    batch, seq, _ = x.shape
    x2 = x.reshape(batch * seq, D_MODEL)
    dil_tables, mla_tables = tables

    y1 = _ffn_ln(x2, *p["ffn1"], *p["ln1"])

    lat, y1_bf = _proj(y1, p["w_lat"], F32, tn=LAT_WIDTH)
    qkv_d = _proj(y1_bf, p["w_dil"], BF16, mode="dilated", extras=dil_tables, seq=seq, tn=DIL_WIDTH)
    gates = _proj(y1_bf, p["w_gate"], F32, mode="gate", extras=(p["b_gate"],), tn=D_MODEL)

    q, k, vt = _latent(lat, p["q_norm_g"], p["kv_norm_g"], p["wqn"], p["wqp"], p["wkn"], p["wvt"],
                       mla_tables, seq=seq)
    out_a = _mla(q, k, vt, batch=batch, seq=seq)

    outs, lses = [], []
    for group, (window, dilation) in enumerate(DIL_PATTERNS):
        o, lse = _dilated_group(qkv_d[group], group=group, window=window, dilation=dilation, batch=batch, seq=seq)
        outs.append(o)
        lses.append(lse)

    y2 = _merge(out_a, outs, lses, gates, y1, p["wa"], p["wb"], p["wo"], *p["ln2"])
    y3 = _ffn_ln(y2, *p["ffn2"], *p["ln3"])
    return y3.reshape(batch, seq, D_MODEL)


def kernel(x_prompt, x_sample, ffn1_w_in, ffn1_w_out, ln1_g, ln1_b, w_in_mix, b_gate, q_norm_g, w_uq, kv_norm_g, w_ukv, w_branch_a, w_branch_b, w_out_mix, ln2_g, ln2_b, ffn2_w_in, ffn2_w_out, ln3_g, ln3_b):
    weights = (ffn1_w_in, ffn1_w_out, ln1_g, ln1_b, w_in_mix, b_gate, q_norm_g, w_uq, kv_norm_g, w_ukv,
               w_branch_a, w_branch_b, w_out_mix, ln2_g, ln2_b, ffn2_w_in, ffn2_w_out, ln3_g, ln3_b)
    y_prompt, y_sample = x_prompt, x_sample
    tables = _rope_tables(max(x_prompt.shape[1], x_sample.shape[1]))
    for layer in range(DEPTH):
        p = _prepare(*(w[layer] for w in weights))
        y_prompt = _encoder_layer(y_prompt, p, tables)
        y_sample = _encoder_layer(y_sample, p, tables)
    return (y_prompt, y_sample)
```

```python
import functools

import jax
import jax.numpy as jnp
from jax import lax
from jax.experimental import pallas as pl
from jax.experimental.pallas import tpu as pltpu

D_MODEL = 2048
DEPTH = 1
MLA_HEADS = 8
Q_LORA = 512
KV_LORA = 512
QK_NOPE = 128
QK_ROPE = 64
V_DIM = 128
DIL_PATTERNS = ((128, 1), (512, 4), (2048, 16))
HEADS_PER_GROUP = 4
DIL_HEAD_DIM = 128
DIL_WIDTH = len(DIL_PATTERNS) * HEADS_PER_GROUP * DIL_HEAD_DIM
D_FF = 5504
ROPE_THETA = 10000.0
LN_EPS = 1e-5
RMS_EPS = 1e-6
NEG_INF = -1e30
ALPHA = (2 * DEPTH) ** 0.25
MLA_SCALE = (QK_NOPE + QK_ROPE) ** -0.5
LOG2_E = 1.4426950408889634
MLA_Q_SCALE = MLA_SCALE * LOG2_E
DIL_KINDS = 3
DIL_SCALE = DIL_HEAD_DIM ** -0.5

LANES = 128
D_FF_PAD = 5632
FF_CHUNK = 512
LAT_WIDTH = Q_LORA + KV_LORA + LANES
MLA_SLAB = 2 * LANES
GROUP_WIDTH = HEADS_PER_GROUP * DIL_HEAD_DIM
PROJ_SUB = 512
DIL_ROWS = 2048
DIL_HEADS_PER_STEP = 4
DIL_UNITS_PER_TRIP = 2
VMEM_LIMIT = 56 * 1024 * 1024
FFN_VMEM_LIMIT = 60 * 1024 * 1024

BF16 = jnp.bfloat16
F32 = jnp.float32
_NT = (((1,), (1,)), ((), ()))


def _params(semantics, vmem_limit=VMEM_LIMIT):
    return pltpu.CompilerParams(dimension_semantics=semantics, vmem_limit_bytes=vmem_limit)


def _layer_norm(v, g, b, scale=1.0):
    mu = jnp.mean(v, axis=-1, keepdims=True)
    c = v - mu
    var = jnp.mean(c * c, axis=-1, keepdims=True)
    return c * (scale * lax.rsqrt(scale * scale * var + LN_EPS)) * g + b


def _ffn_ln_kernel(n_chunks, x_ref, wg_ref, wu_ref, wo_ref, g_ref, b_ref, y_ref, xbf_ref):
    k = pl.program_id(1)

    def chunk(xb):
        gate = jnp.dot(xb, wg_ref[...], preferred_element_type=F32)
        up = jnp.dot(xb, wu_ref[...], preferred_element_type=F32)
        act = gate * jax.nn.sigmoid(gate) * up
        return jnp.dot(act.astype(BF16), wo_ref[...], preferred_element_type=F32)

    @pl.when(k == 0)
    def _():
        x = x_ref[...]
        xb = x.astype(BF16)
        xbf_ref[...] = xb
        y_ref[...] = (2.0 * ALPHA) * x + chunk(xb)

    @pl.when(k > 0)
    def _():
        y_ref[...] += chunk(xbf_ref[...])

    @pl.when(k == n_chunks - 1)
    def _():
        y_ref[...] = _layer_norm(y_ref[...], g_ref[...], b_ref[...], scale=0.5)


def _ffn_ln(x, wg, wu, wo, g, b, *, tm=1024):
    rows = x.shape[0]
    n_chunks = D_FF_PAD // FF_CHUNK
    return pl.pallas_call(
        functools.partial(_ffn_ln_kernel, n_chunks),
        grid=(rows // tm, n_chunks),
        in_specs=[
            pl.BlockSpec((tm, D_MODEL), lambda i, k: (i, 0)),
            pl.BlockSpec((D_MODEL, FF_CHUNK), lambda i, k: (0, k)),
            pl.BlockSpec((D_MODEL, FF_CHUNK), lambda i, k: (0, k)),
            pl.BlockSpec((FF_CHUNK, D_MODEL), lambda i, k: (k, 0)),
            pl.BlockSpec((1, D_MODEL), lambda i, k: (0, 0)),
            pl.BlockSpec((1, D_MODEL), lambda i, k: (0, 0)),
        ],
        out_specs=pl.BlockSpec((tm, D_MODEL), lambda i, k: (i, 0)),
        out_shape=jax.ShapeDtypeStruct((rows, D_MODEL), F32),
        scratch_shapes=[pltpu.VMEM((tm, D_MODEL), BF16)],
        compiler_params=_params(("parallel", "arbitrary"), FFN_VMEM_LIMIT),
        name="ffn_ln",
    )(x, wg, wu, wo, g, b)


def _proj_kernel(mode, x_ref, w_ref, *rest):
    tn = w_ref.shape[1]
    starts = range(0, tn, PROJ_SUB)

    def products(x=None):
        x = x_ref[...] if x is None else x
        return [(c, jnp.dot(x, w_ref[:, c:min(c + PROJ_SUB, tn)], preferred_element_type=F32)) for c in starts]

    if mode == "dilated":
        cos_ref, sin_ref, *o_refs, slab_ref = rest
        n_rope_steps = 2 * DIL_WIDTH // tn
        tm = x_ref.shape[0]

        def emit(rope):
            if rope:
                cos = cos_ref[...]
                sin = sin_ref[...]
            for c, z in products():
                group = c // PROJ_SUB
                d = DIL_PATTERNS[group][1]
                o_ref = o_refs[group]
                for h in range(HEADS_PER_GROUP):
                    zh = z[:, h * LANES:(h + 1) * LANES]
                    if rope:
                        zh = zh * cos + pltpu.roll(zh, LANES // 2, 1) * sin
                    if d == 1:
                        o_ref[h, 0] = zh.astype(BF16)
                        continue
                    slab = group * HEADS_PER_GROUP + h
                    slab_ref[slab] = zh
                    for r in range(d):
                        o_ref[h, r] = slab_ref[slab, pl.ds(r, tm // d, stride=d), :].astype(BF16)

        pl.when(pl.program_id(1) < n_rope_steps)(functools.partial(emit, True))
        pl.when(pl.program_id(1) >= n_rope_steps)(functools.partial(emit, False))
    elif mode == "gate":
        b_ref, o_ref = rest
        for c, z in products():
            sl = slice(c, c + z.shape[1])
            o_ref[:, sl] = jax.nn.sigmoid(z + b_ref[:, sl]).astype(o_ref.dtype)
    else:
        o_ref, xbf_ref = rest
        x = x_ref[...].astype(BF16)
        xbf_ref[...] = x
        for c, z in products(x):
            o_ref[:, c:c + z.shape[1]] = z.astype(o_ref.dtype)


def _proj(x, w, out_dtype, *, mode="cast", extras=(), seq=None, tm=1024, tn=512):
    rows = x.shape[0]
    n = w.shape[1]
    tn = min(tn, n)
    assert rows % tm == 0 and n % tn == 0
    in_specs = [
        pl.BlockSpec((tm, D_MODEL), lambda i, j: (i, 0)),
        pl.BlockSpec((D_MODEL, tn), lambda i, j: (0, j)),
    ]
    out_specs = pl.BlockSpec((tm, tn), lambda i, j: (i, j))
    out_shape = jax.ShapeDtypeStruct((rows, n), out_dtype)
    scratch_shapes = []
    if mode == "dilated":
        assert tn == DIL_WIDTH and PROJ_SUB == GROUP_WIDTH
        tiles_per_seq = seq // tm
        in_specs += [pl.BlockSpec((tm, LANES), lambda i, j: (i % tiles_per_seq, 0))] * 2
        out_specs = [
            pl.BlockSpec((HEADS_PER_GROUP, None, d, tm // d, LANES),
                         lambda i, j: (j, i // tiles_per_seq, 0, i % tiles_per_seq, 0))
            for _, d in DIL_PATTERNS]
        out_shape = [jax.ShapeDtypeStruct((DIL_KINDS * HEADS_PER_GROUP, rows // seq, d, seq // d, LANES), out_dtype)
                     for _, d in DIL_PATTERNS]
        scratch_shapes = [pltpu.VMEM((len(DIL_PATTERNS) * HEADS_PER_GROUP, tm, LANES), F32)]
    elif mode == "gate":
        in_specs += [pl.BlockSpec((1, tn), lambda i, j: (0, j))]
    else:
        assert tn == n
        out_specs = [out_specs, pl.BlockSpec((tm, D_MODEL), lambda i, j: (i, 0))]
        out_shape = [out_shape, jax.ShapeDtypeStruct((rows, D_MODEL), BF16)]
    return pl.pallas_call(
        functools.partial(_proj_kernel, mode),
        grid=(rows // tm, n // tn),
        in_specs=in_specs,
        out_specs=out_specs,
        out_shape=out_shape,
        scratch_shapes=scratch_shapes,
        compiler_params=_params(("parallel", "arbitrary")),
        name="proj_" + mode,
    )(x, w, *extras)


def _latent_kernel(lat_ref, qg_ref, kvg_ref, wqn_ref, wqp_ref, wkn_ref, wvt_ref,
                   cos_ref, sin_lo_ref, sin_hi_ref, q_ref, k_ref, vt_ref):
    lat = lat_ref[...]
    c_q = lat[:, :Q_LORA]
    c_kv = lat[:, Q_LORA:Q_LORA + KV_LORA]
    k_rope = lat[:, Q_LORA + KV_LORA:]

    def rms(x, g):
        return (x * lax.rsqrt(jnp.mean(x * x, axis=-1, keepdims=True) + RMS_EPS) * g).astype(BF16)

    cos = cos_ref[...]
    sin_lo = sin_lo_ref[...]
    sin_hi = sin_hi_ref[...]

    def rope64(x):
        quarter = QK_ROPE // 2
        return (x * cos + pltpu.roll(x, LANES - quarter, 1) * sin_lo
                + pltpu.roll(x, quarter, 1) * sin_hi)

    nq = rms(c_q, qg_ref[...])
    nkv = rms(c_kv, kvg_ref[...])
    q_nope = jnp.dot(nq, wqn_ref[...], preferred_element_type=F32)
    q_pe = jnp.dot(nq, wqp_ref[...], preferred_element_type=F32)
    k_nope = jnp.dot(nkv, wkn_ref[...], preferred_element_type=F32)
    vt = lax.dot_general(wvt_ref[...], nkv, _NT, preferred_element_type=F32)
    k_pe = rope64(k_rope).astype(BF16)
    for h in range(MLA_HEADS):
        src = slice(h * LANES, (h + 1) * LANES)
        lo = slice(h * MLA_SLAB, h * MLA_SLAB + LANES)
        hi = slice(h * MLA_SLAB + LANES, (h + 1) * MLA_SLAB)
        q_ref[:, lo] = (q_nope[:, src] * MLA_Q_SCALE).astype(BF16)
        q_ref[:, hi] = (rope64(q_pe[:, src]) * MLA_Q_SCALE).astype(BF16)
        k_ref[:, lo] = k_nope[:, src].astype(BF16)
        k_ref[:, hi] = k_pe
    vt_ref[...] = vt.astype(BF16)


def _latent(lat, qg, kvg, wqn, wqp, wkn, wvt, tables, *, seq, tm=1024):
    rows = lat.shape[0]
    tiles_per_seq = seq // tm
    width = MLA_HEADS * LANES
    row_spec = lambda w: pl.BlockSpec((tm, w), lambda i: (i, 0))
    full = lambda a: pl.BlockSpec(a.shape, lambda i: (0, 0), pipeline_mode=pl.Buffered(1))
    table_spec = pl.BlockSpec((tm, LANES), lambda i: (i % tiles_per_seq, 0))
    return pl.pallas_call(
        _latent_kernel,
        grid=(rows // tm,),
        in_specs=[row_spec(LAT_WIDTH), full(qg), full(kvg), full(wqn), full(wqp), full(wkn), full(wvt),
                  table_spec, table_spec, table_spec],
        out_specs=[row_spec(MLA_HEADS * MLA_SLAB), row_spec(MLA_HEADS * MLA_SLAB),
                   pl.BlockSpec((width, tm), lambda i: (0, i))],
        out_shape=[jax.ShapeDtypeStruct((rows, MLA_HEADS * MLA_SLAB), BF16),
                   jax.ShapeDtypeStruct((rows, MLA_HEADS * MLA_SLAB), BF16),
                   jax.ShapeDtypeStruct((width, rows), BF16)],
        compiler_params=_params(("parallel",)),
        name="latent",
    )(lat, qg, kvg, wqn, wqp, wkn, wvt, *tables)


def _mla_kernel(n_kv, tk, q_ref, k_ref, vt_ref, o_ref, acc_ref, s_ref, mblk_ref):
    n_sub, _, ts = acc_ref.shape
    acc_ref[...] = jnp.zeros_like(acc_ref)

    def scores_into(slot, j):
        k = k_ref[pl.ds(pl.multiple_of(j * tk, tk), tk), :]
        for t in range(n_sub):
            s = lax.dot_general(k, q_ref[pl.ds(t * ts, ts), :], _NT, preferred_element_type=F32)
            s_ref[slot, t] = s
            mblk_ref[slot, t] = jnp.max(s, axis=0, keepdims=True)

    def consume(slot, j, carry):
        vt = vt_ref[:, pl.ds(pl.multiple_of(j * tk, tk), tk)]
        out = []
        for t in range(n_sub):
            m_prev, l_prev = carry[t]
            m_new = jnp.maximum(m_prev, mblk_ref[slot, t])
            alpha = jnp.exp2(m_prev - m_new)
            p = jnp.exp2(s_ref[slot, t] - m_new)
            l_new = alpha * l_prev + jnp.sum(p, axis=0, keepdims=True)
            acc_ref[t] = alpha * acc_ref[t] + jnp.dot(vt, p.astype(BF16), preferred_element_type=F32)
            out.append((m_new, l_new))
        return tuple(out)

    def body(jj, carry):
        j = 2 * jj
        scores_into(1, j + 1)
        carry = consume(0, j, carry)
        scores_into(0, j + 2)
        return consume(1, j + 1, carry)

    scores_into(0, 0)
    init = tuple((jnp.full((1, ts), -jnp.inf, F32), jnp.zeros((1, ts), F32)) for _ in range(n_sub))
    carry = lax.fori_loop(0, n_kv // 2 - 1, body, init)
    scores_into(1, n_kv - 1)
    carry = consume(0, n_kv - 2, carry)
    carry = consume(1, n_kv - 1, carry)
    for t in range(n_sub):
        o_ref[pl.ds(t * ts, ts), :] = (acc_ref[t] / carry[t][1]).T.astype(o_ref.dtype)


def _mla(q, k, vt, *, batch, seq, tq=2048, ts=512, tk=1024):
    assert seq % tq == 0 and tq % ts == 0 and seq % (2 * tk) == 0
    n_q = seq // tq
    return pl.pallas_call(
        functools.partial(_mla_kernel, seq // tk, tk),
        grid=(batch, MLA_HEADS, n_q),
        in_specs=[
            pl.BlockSpec((tq, MLA_SLAB), lambda b, h, i: (b * n_q + i, h)),
            pl.BlockSpec((seq, MLA_SLAB), lambda b, h, i: (b, h)),
            pl.BlockSpec((V_DIM, seq), lambda b, h, i: (h, b)),
        ],
        out_specs=pl.BlockSpec((tq, V_DIM), lambda b, h, i: (b * n_q + i, h)),
        out_shape=jax.ShapeDtypeStruct((batch * seq, MLA_HEADS * V_DIM), BF16),
        scratch_shapes=[pltpu.VMEM((tq // ts, V_DIM, ts), F32), pltpu.VMEM((2, tq // ts, tk, ts), F32),
                        pltpu.VMEM((2, tq // ts, 1, ts), F32)],
        compiler_params=_params(("parallel", "parallel", "arbitrary")),
        name="mla",
    )(q, k, vt)


def _dilated_kernel(d, half, stream_len, q_ref, kp_ref, kc_ref, kn_ref, vp_ref, vc_ref, vn_ref,
                    o_ref, lse_ref, ks_ref, vs_ref):
    n_heads = q_ref.shape[0]
    tl = DIL_ROWS // d
    i = pl.program_id(2)
    for dst, before, cur, after in ((ks_ref, kp_ref, kc_ref, kn_ref), (vs_ref, vp_ref, vc_ref, vn_ref)):
        dst[:, :, 0:half] = before[...]
        dst[:, :, half:half + tl] = cur[...]
        dst[:, :, half + tl:] = after[...]

    chunk = 2 * half
    row = lax.broadcasted_iota(jnp.int32, (chunk, 2 * chunk), 0)
    col = lax.broadcasted_iota(jnp.int32, (chunk, 2 * chunk), 1)
    band = (col >= row) & (col <= row + 2 * half)
    shift = d.bit_length() - 1

    def unit_group(ug, carry):
        jobs = []
        for uu in range(DIL_UNITS_PER_TRIP):
            u = ug * DIL_UNITS_PER_TRIP + uu
            r = u & (d - 1)
            c = u >> shift
            row0 = pl.multiple_of(c * chunk, chunk)
            k_idx = i * tl + c * chunk - half + col
            mask = band & (k_idx >= 0) & (k_idx < stream_len)
            for h in range(n_heads):
                q = q_ref[h, r, pl.ds(row0, chunk), :]
                k = ks_ref[h, r, pl.ds(row0, 2 * chunk), :]
                s = lax.dot_general(q, k, _NT, preferred_element_type=F32) * DIL_SCALE
                jobs.append((h, r, row0, mask, s))
        for h, r, row0, mask, s in jobs:
            s = jnp.where(mask, s, NEG_INF)
            m = jnp.max(s, axis=1, keepdims=True)
            p = jnp.exp(s - m)
            denom = jnp.sum(p, axis=1, keepdims=True)
            pn = (p / denom).astype(BF16)
            o = jnp.dot(pn, vs_ref[h, r, pl.ds(row0, 2 * chunk), :], preferred_element_type=F32)
            start = r + d * row0
            o_ref[h, pl.ds(start, chunk, stride=d), :] = o
            lse_ref[h, pl.ds(start, chunk, stride=d), :] = jnp.broadcast_to(m + jnp.log(denom), (chunk, LANES))
        return carry

    lax.fori_loop(0, DIL_ROWS // chunk // DIL_UNITS_PER_TRIP, unit_group, 0)


def _dilated_group(qkv, *, group, window, dilation, batch, seq):
    d = dilation
    half = window // (2 * d)
    assert seq % DIL_ROWS == 0 and d & (d - 1) == 0 and DIL_ROWS % (2 * half * d) == 0
    assert (DIL_ROWS // (2 * half)) % DIL_UNITS_PER_TRIP == 0
    n_r = seq // DIL_ROWS
    tl = DIL_ROWS // d
    halo_per_step = tl // half
    n_halo = seq // d // half
    steps_per_group = HEADS_PER_GROUP // DIL_HEADS_PER_STEP
    blk = (DIL_HEADS_PER_STEP, None, d, tl, LANES)
    hblk = (DIL_HEADS_PER_STEP, None, d, half, LANES)

    def specs(kind):
        heads = lambda g: kind * steps_per_group + g
        cur = pl.BlockSpec(blk, lambda g, b, i: (heads(g), b, 0, i, 0))
        before = pl.BlockSpec(hblk, lambda g, b, i: (heads(g), b, 0, jnp.maximum(i * halo_per_step - 1, 0), 0))
        after = pl.BlockSpec(hblk, lambda g, b, i: (heads(g), b, 0,
                                                    jnp.minimum((i + 1) * halo_per_step, n_halo - 1), 0))
        return before, cur, after

    out_spec = pl.BlockSpec((DIL_HEADS_PER_STEP, DIL_ROWS, LANES), lambda g, b, i: (g, b * n_r + i, 0))
    out_shape = jax.ShapeDtypeStruct((HEADS_PER_GROUP, batch * seq, LANES), F32)
    return pl.pallas_call(
        functools.partial(_dilated_kernel, d, half, seq // d),
        grid=(steps_per_group, batch, n_r),
        in_specs=[specs(0)[1], *specs(1), *specs(2)],
        out_specs=[out_spec, out_spec],
        out_shape=[out_shape, out_shape],
        scratch_shapes=[pltpu.VMEM((DIL_HEADS_PER_STEP, d, tl + 2 * half, LANES), BF16),
                        pltpu.VMEM((DIL_HEADS_PER_STEP, d, tl + 2 * half, LANES), BF16)],
        compiler_params=_params(("parallel", "parallel", "parallel")),
        name="dilated_g%d" % group,
    )(*([qkv] * 7))


def _merge_kernel(oa_ref, o0_ref, o1_ref, o2_ref, l0_ref, l1_ref, l2_ref, ga_ref, gb_ref, y1_ref,
                  wa_ref, wb_ref, wo_ref, g_ref, b_ref, y_ref):
    heads = []
    for h in range(HEADS_PER_GROUP):
        l0, l1, l2 = l0_ref[h], l1_ref[h], l2_ref[h]
        m = jnp.maximum(jnp.maximum(l0, l1), l2)
        e0, e1, e2 = jnp.exp(l0 - m), jnp.exp(l1 - m), jnp.exp(l2 - m)
        denom = e0 + e1 + e2
        heads.append(((e0 / denom) * o0_ref[h] + (e1 / denom) * o1_ref[h] + (e2 / denom) * o2_ref[h]).astype(BF16))
    out_a = oa_ref[...]
    out_b = jnp.concatenate(heads, axis=1)
    cols = [slice(c, c + PROJ_SUB) for c in range(0, D_MODEL, PROJ_SUB)]
    branches = [(jnp.dot(out_a, wa_ref[:, sl], preferred_element_type=F32),
                 jnp.dot(out_b, wb_ref[:, sl], preferred_element_type=F32)) for sl in cols]
    mix = None
    for sl, (branch_a, branch_b) in zip(cols, branches):
        merged = (ga_ref[:, sl] * branch_a + gb_ref[:, sl] * branch_b).astype(BF16)
        part = jnp.dot(merged, wo_ref[sl, :], preferred_element_type=F32)
        mix = part if mix is None else mix + part
    y_ref[...] = _layer_norm(ALPHA * y1_ref[...] + mix, g_ref[...], b_ref[...])


def _merge(out_a, outs, lses, gates, y1, wa, wb, wo, g, b, *, tm=256):
    rows = y1.shape[0]
    row_spec = lambda w, c=0: pl.BlockSpec((tm, w), lambda i: (i, c))
    full = lambda a: pl.BlockSpec(a.shape, lambda i: (0, 0), pipeline_mode=pl.Buffered(1))
    return pl.pallas_call(
        _merge_kernel,
        grid=(rows // tm,),
        in_specs=[row_spec(MLA_HEADS * V_DIM)]
                 + [pl.BlockSpec((HEADS_PER_GROUP, tm, LANES), lambda i: (0, i, 0))] * 6
                 + [row_spec(D_MODEL, 0), row_spec(D_MODEL, 1), row_spec(D_MODEL)]
                 + [full(wa), full(wb), full(wo), full(g), full(b)],
        out_specs=row_spec(D_MODEL),
        out_shape=jax.ShapeDtypeStruct((rows, D_MODEL), F32),
        compiler_params=_params(("parallel",)),
        name="merge",
    )(out_a, *outs, *lses, gates, gates, y1, wa, wb, wo, g, b)


def _ffn_in_kernel(w_ref, wg_ref, wu_ref):
    w = w_ref[...]
    pad = jnp.zeros((w.shape[0], D_FF_PAD - D_FF), BF16)
    for dst, part in ((wg_ref, w[:, :D_FF]), (wu_ref, w[:, D_FF:])):
        dst[:, :D_FF] = part.astype(BF16)
        dst[:, D_FF:] = pad


def _ffn_out_kernel(n_valid, w_ref, o_ref):
    @pl.when(pl.program_id(0) < n_valid)
    def _():
        o_ref[...] = w_ref[...].astype(BF16)

    @pl.when(pl.program_id(0) >= n_valid)
    def _():
        o_ref[...] = jnp.zeros_like(o_ref)


def _ffn_weights(w_in, w_out, *, tr=64, tc=LANES):
    half = jax.ShapeDtypeStruct((D_MODEL, D_FF_PAD), BF16)
    wg, wu = pl.pallas_call(
        _ffn_in_kernel,
        grid=(D_MODEL // tr,),
        in_specs=[pl.BlockSpec((tr, 2 * D_FF), lambda i: (i, 0))],
        out_specs=[pl.BlockSpec((tr, D_FF_PAD), lambda i: (i, 0))] * 2,
        out_shape=[half, half],
        compiler_params=_params(("parallel",)),
        name="ffn_w_in",
    )(w_in)
    n_valid = D_FF // tc
    wo = pl.pallas_call(
        functools.partial(_ffn_out_kernel, n_valid),
        grid=(D_FF_PAD // tc,),
        in_specs=[pl.BlockSpec((tc, D_MODEL), lambda i: (jnp.minimum(i, n_valid - 1), 0))],
        out_specs=pl.BlockSpec((tc, D_MODEL), lambda i: (i, 0)),
        out_shape=jax.ShapeDtypeStruct((D_FF_PAD, D_MODEL), BF16),
        compiler_params=_params(("parallel",)),
        name="ffn_w_out",
    )(w_out)
    return wg, wu, wo


def _rope_tables(seq):
    pos = jnp.arange(seq, dtype=F32)[:, None]

    def angles(d):
        inv_freq = ROPE_THETA ** (-jnp.arange(0, d, 2, dtype=F32) / d)
        ang = pos * inv_freq[None, :]
        return jnp.cos(ang), jnp.sin(ang)

    cos, sin = angles(DIL_HEAD_DIM)
    dil = (jnp.concatenate([cos, cos], axis=1), jnp.concatenate([-sin, sin], axis=1))
    cos, sin = angles(QK_ROPE)
    zeros = jnp.zeros_like(cos)
    pad = jnp.zeros((seq, LANES - QK_ROPE), F32)
    mla = (jnp.concatenate([cos, cos, pad], axis=1),
           jnp.concatenate([-sin, zeros, pad], axis=1),
           jnp.concatenate([zeros, sin, pad], axis=1))
    return dil, mla


def _prepare(ffn1_w_in, ffn1_w_out, ln1_g, ln1_b, w_in_mix, b_gate, q_norm_g, w_uq, kv_norm_g, w_ukv,
             w_branch_a, w_branch_b, w_out_mix, ln2_g, ln2_b, ffn2_w_in, ffn2_w_out, ln3_g, ln3_b):
    ffn = _ffn_weights
    row = lambda a: a.reshape(1, -1)
    o_lat = Q_LORA + KV_LORA + QK_ROPE
    o_qk = o_lat + 2 * DIL_WIDTH
    o_v = o_qk + DIL_WIDTH
    w_lat = jnp.pad(w_in_mix[:, :o_lat], ((0, 0), (0, LAT_WIDTH - o_lat))).astype(BF16)
    uq = w_uq.reshape(Q_LORA, MLA_HEADS, QK_NOPE + QK_ROPE)
    ukv = w_ukv.reshape(KV_LORA, MLA_HEADS, QK_NOPE + V_DIM)
    flat = lambda a: a.reshape(a.shape[0], -1).astype(BF16)
    return dict(
        ffn1=ffn(ffn1_w_in, ffn1_w_out), ln1=(row(ln1_g), row(ln1_b)),
        ffn2=ffn(ffn2_w_in, ffn2_w_out), ln3=(row(ln3_g), row(ln3_b)),
        w_lat=w_lat,
        w_dil=w_in_mix[:, o_lat:o_v].astype(BF16),
        w_gate=w_in_mix[:, o_v:].astype(BF16),
        b_gate=row(b_gate),
        q_norm_g=row(q_norm_g), kv_norm_g=row(kv_norm_g),
        wqn=flat(uq[:, :, :QK_NOPE]),
        wqp=flat(jnp.pad(uq[:, :, QK_NOPE:], ((0, 0), (0, 0), (0, LANES - QK_ROPE)))),
        wkn=flat(ukv[:, :, :QK_NOPE]),
        wvt=flat(ukv[:, :, QK_NOPE:]).T,
        wa=w_branch_a.astype(BF16), wb=w_branch_b.astype(BF16), wo=w_out_mix.astype(BF16),
        ln2=(row(ln2_g), row(ln2_b)),
    )


def _encoder_layer(x, p, tables):
    batch, seq, _ = x.shape
    x2 = x.reshape(batch * seq, D_MODEL)
    dil_tables, mla_tables = tables

    y1 = _ffn_ln(x2, *p["ffn1"], *p["ln1"])

    lat, y1_bf = _proj(y1, p["w_lat"], F32, tn=LAT_WIDTH)
    qkv_d = _proj(y1_bf, p["w_dil"], BF16, mode="dilated", extras=dil_tables, seq=seq, tn=DIL_WIDTH)
    gates = _proj(y1_bf, p["w_gate"], F32, mode="gate", extras=(p["b_gate"],), tn=D_MODEL)

    q, k, vt = _latent(lat, p["q_norm_g"], p["kv_norm_g"], p["wqn"], p["wqp"], p["wkn"], p["wvt"],
                       mla_tables, seq=seq)
    out_a = _mla(q, k, vt, batch=batch, seq=seq)

    outs, lses = [], []
    for group, (window, dilation) in enumerate(DIL_PATTERNS):
        o, lse = _dilated_group(qkv_d[group], group=group, window=window, dilation=dilation, batch=batch, seq=seq)
        outs.append(o)
        lses.append(lse)

    y2 = _merge(out_a, outs, lses, gates, y1, p["wa"], p["wb"], p["wo"], *p["ln2"])
    y3 = _ffn_ln(y2, *p["ffn2"], *p["ln3"])
    return y3.reshape(batch, seq, D_MODEL)


def kernel(x_prompt, x_sample, ffn1_w_in, ffn1_w_out, ln1_g, ln1_b, w_in_mix, b_gate, q_norm_g, w_uq, kv_norm_g, w_ukv, w_branch_a, w_branch_b, w_out_mix, ln2_g, ln2_b, ffn2_w_in, ffn2_w_out, ln3_g, ln3_b):
    weights = (ffn1_w_in, ffn1_w_out, ln1_g, ln1_b, w_in_mix, b_gate, q_norm_g, w_uq, kv_norm_g, w_ukv,
               w_branch_a, w_branch_b, w_out_mix, ln2_g, ln2_b, ffn2_w_in, ffn2_w_out, ln3_g, ln3_b)
    y_prompt, y_sample = x_prompt, x_sample
    tables = _rope_tables(max(x_prompt.shape[1], x_sample.shape[1]))
    for layer in range(DEPTH):
        p = _prepare(*(w[layer] for w in weights))
        y_prompt = _encoder_layer(y_prompt, p, tables)
        y_sample = _encoder_layer(y_sample, p, tables)
    return (y_prompt, y_sample)
```

```python
import functools

import jax
import jax.numpy as jnp
from jax import lax
from jax.experimental import pallas as pl
from jax.experimental.pallas import tpu as pltpu

D_MODEL = 2048
DEPTH = 1
MLA_HEADS = 8
Q_LORA = 512
KV_LORA = 512
QK_NOPE = 128
QK_ROPE = 64
V_DIM = 128
DIL_PATTERNS = ((128, 1), (512, 4), (2048, 16))
HEADS_PER_GROUP = 4
DIL_HEAD_DIM = 128
DIL_WIDTH = len(DIL_PATTERNS) * HEADS_PER_GROUP * DIL_HEAD_DIM
D_FF = 5504
ROPE_THETA = 10000.0
LN_EPS = 1e-5
RMS_EPS = 1e-6
NEG_INF = -1e30
ALPHA = (2 * DEPTH) ** 0.25
MLA_SCALE = (QK_NOPE + QK_ROPE) ** -0.5
LOG2_E = 1.4426950408889634
MLA_Q_SCALE = MLA_SCALE * LOG2_E
DIL_KINDS = 3
DIL_SCALE = DIL_HEAD_DIM ** -0.5

LANES = 128
D_FF_PAD = 5632
FF_CHUNK = 512
LAT_WIDTH = Q_LORA + KV_LORA + LANES
MLA_SLAB = 2 * LANES
GROUP_WIDTH = HEADS_PER_GROUP * DIL_HEAD_DIM
PROJ_SUB = 512
DIL_ROWS = 2048
DIL_HEADS_PER_STEP = 4
DIL_UNITS_PER_TRIP = 2
VMEM_LIMIT = 56 * 1024 * 1024
FFN_VMEM_LIMIT = 60 * 1024 * 1024

BF16 = jnp.bfloat16
F32 = jnp.float32
_NT = (((1,), (1,)), ((), ()))


def _params(semantics, vmem_limit=VMEM_LIMIT):
    return pltpu.CompilerParams(dimension_semantics=semantics, vmem_limit_bytes=vmem_limit)


def _layer_norm(v, g, b, scale=1.0):
    mu = jnp.mean(v, axis=-1, keepdims=True)
    c = v - mu
    var = jnp.mean(c * c, axis=-1, keepdims=True)
    return c * (scale * lax.rsqrt(scale * scale * var + LN_EPS)) * g + b


def _ffn_ln_kernel(n_chunks, x_ref, wg_ref, wu_ref, wo_ref, g_ref, b_ref, y_ref, xbf_ref):
    k = pl.program_id(1)

    def chunk(xb):
        gate = jnp.dot(xb, wg_ref[...], preferred_element_type=F32)
        up = jnp.dot(xb, wu_ref[...], preferred_element_type=F32)
        act = gate * jax.nn.sigmoid(gate) * up
        return jnp.dot(act.astype(BF16), wo_ref[...], preferred_element_type=F32)

    @pl.when(k == 0)
    def _():
        x = x_ref[...]
        xb = x.astype(BF16)
        xbf_ref[...] = xb
        y_ref[...] = (2.0 * ALPHA) * x + chunk(xb)

    @pl.when(k > 0)
    def _():
        y_ref[...] += chunk(xbf_ref[...])

    @pl.when(k == n_chunks - 1)
    def _():
        y_ref[...] = _layer_norm(y_ref[...], g_ref[...], b_ref[...], scale=0.5)


def _ffn_ln(x, wg, wu, wo, g, b, *, tm=1024):
    rows = x.shape[0]
    n_chunks = D_FF_PAD // FF_CHUNK
    return pl.pallas_call(
        functools.partial(_ffn_ln_kernel, n_chunks),
        grid=(rows // tm, n_chunks),
        in_specs=[
            pl.BlockSpec((tm, D_MODEL), lambda i, k: (i, 0)),
            pl.BlockSpec((D_MODEL, FF_CHUNK), lambda i, k: (0, k)),
            pl.BlockSpec((D_MODEL, FF_CHUNK), lambda i, k: (0, k)),
            pl.BlockSpec((FF_CHUNK, D_MODEL), lambda i, k: (k, 0)),
            pl.BlockSpec((1, D_MODEL), lambda i, k: (0, 0)),
            pl.BlockSpec((1, D_MODEL), lambda i, k: (0, 0)),
        ],
        out_specs=pl.BlockSpec((tm, D_MODEL), lambda i, k: (i, 0)),
        out_shape=jax.ShapeDtypeStruct((rows, D_MODEL), F32),
        scratch_shapes=[pltpu.VMEM((tm, D_MODEL), BF16)],
        compiler_params=_params(("parallel", "arbitrary"), FFN_VMEM_LIMIT),
        name="ffn_ln",
    )(x, wg, wu, wo, g, b)


def _proj_kernel(mode, x_ref, w_ref, *rest):
    tn = w_ref.shape[1]
    starts = range(0, tn, PROJ_SUB)

    def products(x=None):
        x = x_ref[...] if x is None else x
        return [(c, jnp.dot(x, w_ref[:, c:min(c + PROJ_SUB, tn)], preferred_element_type=F32)) for c in starts]

    if mode == "dilated":
        cos_ref, sin_ref, *o_refs, slab_ref = rest
        n_rope_steps = 2 * DIL_WIDTH // tn
        tm = x_ref.shape[0]

        def emit(rope):
            if rope:
                cos = cos_ref[...]
                sin = sin_ref[...]
            for c, z in products():
                group = c // PROJ_SUB
                d = DIL_PATTERNS[group][1]
                o_ref = o_refs[group]
                for h in range(HEADS_PER_GROUP):
                    zh = z[:, h * LANES:(h + 1) * LANES]
                    if rope:
                        zh = zh * cos + pltpu.roll(zh, LANES // 2, 1) * sin
                    if d == 1:
                        o_ref[h, 0] = zh.astype(BF16)
                        continue
                    slab = group * HEADS_PER_GROUP + h
                    slab_ref[slab] = zh
                    for r in range(d):
                        o_ref[h, r] = slab_ref[slab, pl.ds(r, tm // d, stride=d), :].astype(BF16)

        pl.when(pl.program_id(1) < n_rope_steps)(functools.partial(emit, True))
        pl.when(pl.program_id(1) >= n_rope_steps)(functools.partial(emit, False))
    elif mode == "gate":
        b_ref, o_ref = rest
        for c, z in products():
            sl = slice(c, c + z.shape[1])
            o_ref[:, sl] = jax.nn.sigmoid(z + b_ref[:, sl]).astype(o_ref.dtype)
    else:
        o_ref, xbf_ref = rest
        x = x_ref[...].astype(BF16)
        xbf_ref[...] = x
        for c, z in products(x):
            o_ref[:, c:c + z.shape[1]] = z.astype(o_ref.dtype)


def _proj(x, w, out_dtype, *, mode="cast", extras=(), seq=None, tm=1024, tn=512):
    rows = x.shape[0]
    n = w.shape[1]
    tn = min(tn, n)
    assert rows % tm == 0 and n % tn == 0
    in_specs = [
        pl.BlockSpec((tm, D_MODEL), lambda i, j: (i, 0)),
        pl.BlockSpec((D_MODEL, tn), lambda i, j: (0, j)),
    ]
    out_specs = pl.BlockSpec((tm, tn), lambda i, j: (i, j))
    out_shape = jax.ShapeDtypeStruct((rows, n), out_dtype)
    scratch_shapes = []
    if mode == "dilated":
        assert tn == DIL_WIDTH and PROJ_SUB == GROUP_WIDTH
        tiles_per_seq = seq // tm
        in_specs += [pl.BlockSpec((tm, LANES), lambda i, j: (i % tiles_per_seq, 0))] * 2
        out_specs = [
            pl.BlockSpec((HEADS_PER_GROUP, None, d, tm // d, LANES),
                         lambda i, j: (j, i // tiles_per_seq, 0, i % tiles_per_seq, 0))
            for _, d in DIL_PATTERNS]
        out_shape = [jax.ShapeDtypeStruct((DIL_KINDS * HEADS_PER_GROUP, rows // seq, d, seq // d, LANES), out_dtype)
                     for _, d in DIL_PATTERNS]
        scratch_shapes = [pltpu.VMEM((len(DIL_PATTERNS) * HEADS_PER_GROUP, tm, LANES), F32)]
    elif mode == "gate":
        in_specs += [pl.BlockSpec((1, tn), lambda i, j: (0, j))]
    else:
        assert tn == n
        out_specs = [out_specs, pl.BlockSpec((tm, D_MODEL), lambda i, j: (i, 0))]
        out_shape = [out_shape, jax.ShapeDtypeStruct((rows, D_MODEL), BF16)]
    return pl.pallas_call(
        functools.partial(_proj_kernel, mode),
        grid=(rows // tm, n // tn),
        in_specs=in_specs,
        out_specs=out_specs,
        out_shape=out_shape,
        scratch_shapes=scratch_shapes,
        compiler_params=_params(("parallel", "arbitrary")),
        name="proj_" + mode,
    )(x, w, *extras)


def _latent_kernel(lat_ref, qg_ref, kvg_ref, wqn_ref, wqp_ref, wkn_ref, wvt_ref,
                   cos_ref, sin_lo_ref, sin_hi_ref, q_ref, k_ref, vt_ref):
    lat = lat_ref[...]
    c_q = lat[:, :Q_LORA]
    c_kv = lat[:, Q_LORA:Q_LORA + KV_LORA]
    k_rope = lat[:, Q_LORA + KV_LORA:]

    def rms(x, g):
        return (x * lax.rsqrt(jnp.mean(x * x, axis=-1, keepdims=True) + RMS_EPS) * g).astype(BF16)

    cos = cos_ref[...]
    sin_lo = sin_lo_ref[...]
    sin_hi = sin_hi_ref[...]

    def rope64(x):
        quarter = QK_ROPE // 2
        return (x * cos + pltpu.roll(x, LANES - quarter, 1) * sin_lo
                + pltpu.roll(x, quarter, 1) * sin_hi)

    nq = rms(c_q, qg_ref[...])
    nkv = rms(c_kv, kvg_ref[...])
    q_nope = jnp.dot(nq, wqn_ref[...], preferred_element_type=F32)
    q_pe = jnp.dot(nq, wqp_ref[...], preferred_element_type=F32)
    k_nope = jnp.dot(nkv, wkn_ref[...], preferred_element_type=F32)
    vt = lax.dot_general(wvt_ref[...], nkv, _NT, preferred_element_type=F32)
    k_pe = rope64(k_rope).astype(BF16)
    for h in range(MLA_HEADS):
        src = slice(h * LANES, (h + 1) * LANES)
        lo = slice(h * MLA_SLAB, h * MLA_SLAB + LANES)
        hi = slice(h * MLA_SLAB + LANES, (h + 1) * MLA_SLAB)
        q_ref[:, lo] = (q_nope[:, src] * MLA_Q_SCALE).astype(BF16)
        q_ref[:, hi] = (rope64(q_pe[:, src]) * MLA_Q_SCALE).astype(BF16)
        k_ref[:, lo] = k_nope[:, src].astype(BF16)
        k_ref[:, hi] = k_pe
    vt_ref[...] = vt.astype(BF16)


def _latent(lat, qg, kvg, wqn, wqp, wkn, wvt, tables, *, seq, tm=1024):
    rows = lat.shape[0]
    tiles_per_seq = seq // tm
    width = MLA_HEADS * LANES
    row_spec = lambda w: pl.BlockSpec((tm, w), lambda i: (i, 0))
    full = lambda a: pl.BlockSpec(a.shape, lambda i: (0, 0), pipeline_mode=pl.Buffered(1))
    table_spec = pl.BlockSpec((tm, LANES), lambda i: (i % tiles_per_seq, 0))
    return pl.pallas_call(
        _latent_kernel,
        grid=(rows // tm,),
        in_specs=[row_spec(LAT_WIDTH), full(qg), full(kvg), full(wqn), full(wqp), full(wkn), full(wvt),
                  table_spec, table_spec, table_spec],
        out_specs=[row_spec(MLA_HEADS * MLA_SLAB), row_spec(MLA_HEADS * MLA_SLAB),
                   pl.BlockSpec((width, tm), lambda i: (0, i))],
        out_shape=[jax.ShapeDtypeStruct((rows, MLA_HEADS * MLA_SLAB), BF16),
                   jax.ShapeDtypeStruct((rows, MLA_HEADS * MLA_SLAB), BF16),
                   jax.ShapeDtypeStruct((width, rows), BF16)],
        compiler_params=_params(("parallel",)),
        name="latent",
    )(lat, qg, kvg, wqn, wqp, wkn, wvt, *tables)


def _mla_kernel(n_kv, tk, q_ref, k_ref, vt_ref, o_ref, acc_ref, s_ref, mblk_ref):
    n_sub, _, ts = acc_ref.shape
    acc_ref[...] = jnp.zeros_like(acc_ref)

    def scores_into(slot, j):
        k = k_ref[pl.ds(pl.multiple_of(j * tk, tk), tk), :]
        for t in range(n_sub):
            s = lax.dot_general(k, q_ref[pl.ds(t * ts, ts), :], _NT, preferred_element_type=F32)
            s_ref[slot, t] = s
            mblk_ref[slot, t] = jnp.max(s, axis=0, keepdims=True)

    def consume(slot, j, carry):
        vt = vt_ref[:, pl.ds(pl.multiple_of(j * tk, tk), tk)]
        out = []
        for t in range(n_sub):
            m_prev, l_prev = carry[t]
            m_new = jnp.maximum(m_prev, mblk_ref[slot, t])
            alpha = jnp.exp2(m_prev - m_new)
            p = jnp.exp2(s_ref[slot, t] - m_new)
            l_new = alpha * l_prev + jnp.sum(p, axis=0, keepdims=True)
            acc_ref[t] = alpha * acc_ref[t] + jnp.dot(vt, p.astype(BF16), preferred_element_type=F32)
            out.append((m_new, l_new))
        return tuple(out)

    def body(jj, carry):
        j = 2 * jj
        scores_into(1, j + 1)
        carry = consume(0, j, carry)
        scores_into(0, j + 2)
        return consume(1, j + 1, carry)

    scores_into(0, 0)
    init = tuple((jnp.full((1, ts), -jnp.inf, F32), jnp.zeros((1, ts), F32)) for _ in range(n_sub))
    carry = lax.fori_loop(0, n_kv // 2 - 1, body, init)
    scores_into(1, n_kv - 1)
    carry = consume(0, n_kv - 2, carry)
    carry = consume(1, n_kv - 1, carry)
    for t in range(n_sub):
        o_ref[pl.ds(t * ts, ts), :] = (acc_ref[t] / carry[t][1]).T.astype(o_ref.dtype)


def _mla(q, k, vt, *, batch, seq, tq=2048, ts=512, tk=1024):
    assert seq % tq == 0 and tq % ts == 0 and seq % (2 * tk) == 0
    n_q = seq // tq
    return pl.pallas_call(
        functools.partial(_mla_kernel, seq // tk, tk),
        grid=(batch, MLA_HEADS, n_q),
        in_specs=[
            pl.BlockSpec((tq, MLA_SLAB), lambda b, h, i: (b * n_q + i, h)),
            pl.BlockSpec((seq, MLA_SLAB), lambda b, h, i: (b, h)),
            pl.BlockSpec((V_DIM, seq), lambda b, h, i: (h, b)),
        ],
        out_specs=pl.BlockSpec((tq, V_DIM), lambda b, h, i: (b * n_q + i, h)),
        out_shape=jax.ShapeDtypeStruct((batch * seq, MLA_HEADS * V_DIM), BF16),
        scratch_shapes=[pltpu.VMEM((tq // ts, V_DIM, ts), F32), pltpu.VMEM((2, tq // ts, tk, ts), F32),
                        pltpu.VMEM((2, tq // ts, 1, ts), F32)],
        compiler_params=_params(("parallel", "parallel", "arbitrary")),
        name="mla",
    )(q, k, vt)


def _dilated_kernel(d, half, stream_len, q_ref, kp_ref, kc_ref, kn_ref, vp_ref, vc_ref, vn_ref,
                    o_ref, lse_ref, ks_ref, vs_ref):
    n_heads = q_ref.shape[0]
    tl = DIL_ROWS // d
    i = pl.program_id(2)
    for dst, before, cur, after in ((ks_ref, kp_ref, kc_ref, kn_ref), (vs_ref, vp_ref, vc_ref, vn_ref)):
        dst[:, :, 0:half] = before[...]
        dst[:, :, half:half + tl] = cur[...]
        dst[:, :, half + tl:] = after[...]

    chunk = 2 * half
    row = lax.broadcasted_iota(jnp.int32, (chunk, 2 * chunk), 0)
    col = lax.broadcasted_iota(jnp.int32, (chunk, 2 * chunk), 1)
    band = (col >= row) & (col <= row + 2 * half)
    shift = d.bit_length() - 1

    def unit_group(ug, carry):
        jobs = []
        for uu in range(DIL_UNITS_PER_TRIP):
            u = ug * DIL_UNITS_PER_TRIP + uu
            r = u & (d - 1)
            c = u >> shift
            row0 = pl.multiple_of(c * chunk, chunk)
            k_idx = i * tl + c * chunk - half + col
            mask = band & (k_idx >= 0) & (k_idx < stream_len)
            for h in range(n_heads):
                q = q_ref[h, r, pl.ds(row0, chunk), :]
                k = ks_ref[h, r, pl.ds(row0, 2 * chunk), :]
                s = lax.dot_general(q, k, _NT, preferred_element_type=F32) * DIL_SCALE
                jobs.append((h, r, row0, mask, s))
        for h, r, row0, mask, s in jobs:
            s = jnp.where(mask, s, NEG_INF)
            m = jnp.max(s, axis=1, keepdims=True)
            p = jnp.exp(s - m)
            denom = jnp.sum(p, axis=1, keepdims=True)
            pn = (p / denom).astype(BF16)
            o = jnp.dot(pn, vs_ref[h, r, pl.ds(row0, 2 * chunk), :], preferred_element_type=F32)
            start = r + d * row0
            o_ref[h, pl.ds(start, chunk, stride=d), :] = o
            lse_ref[h, pl.ds(start, chunk, stride=d), :] = jnp.broadcast_to(m + jnp.log(denom), (chunk, LANES))
        return carry

    lax.fori_loop(0, DIL_ROWS // chunk // DIL_UNITS_PER_TRIP, unit_group, 0)


def _dilated_group(qkv, *, group, window, dilation, batch, seq):
    d = dilation
    half = window // (2 * d)
    assert seq % DIL_ROWS == 0 and d & (d - 1) == 0 and DIL_ROWS % (2 * half * d) == 0
    assert (DIL_ROWS // (2 * half)) % DIL_UNITS_PER_TRIP == 0
    n_r = seq // DIL_ROWS
    tl = DIL_ROWS // d
    halo_per_step = tl // half
    n_halo = seq // d // half
    steps_per_group = HEADS_PER_GROUP // DIL_HEADS_PER_STEP
    blk = (DIL_HEADS_PER_STEP, None, d, tl, LANES)
    hblk = (DIL_HEADS_PER_STEP, None, d, half, LANES)

    def specs(kind):
        heads = lambda g: kind * steps_per_group + g
        cur = pl.BlockSpec(blk, lambda g, b, i: (heads(g), b, 0, i, 0))
        before = pl.BlockSpec(hblk, lambda g, b, i: (heads(g), b, 0, jnp.maximum(i * halo_per_step - 1, 0), 0))
        after = pl.BlockSpec(hblk, lambda g, b, i: (heads(g), b, 0,
                                                    jnp.minimum((i + 1) * halo_per_step, n_halo - 1), 0))
        return before, cur, after

    out_spec = pl.BlockSpec((DIL_HEADS_PER_STEP, DIL_ROWS, LANES), lambda g, b, i: (g, b * n_r + i, 0))
    out_shape = jax.ShapeDtypeStruct((HEADS_PER_GROUP, batch * seq, LANES), F32)
    return pl.pallas_call(
        functools.partial(_dilated_kernel, d, half, seq // d),
        grid=(steps_per_group, batch, n_r),
        in_specs=[specs(0)[1], *specs(1), *specs(2)],
        out_specs=[out_spec, out_spec],
        out_shape=[out_shape, out_shape],
        scratch_shapes=[pltpu.VMEM((DIL_HEADS_PER_STEP, d, tl + 2 * half, LANES), BF16),
                        pltpu.VMEM((DIL_HEADS_PER_STEP, d, tl + 2 * half, LANES), BF16)],
        compiler_params=_params(("parallel", "parallel", "parallel")),
        name="dilated_g%d" % group,
    )(*([qkv] * 7))


def _merge_kernel(oa_ref, o0_ref, o1_ref, o2_ref, l0_ref, l1_ref, l2_ref, ga_ref, gb_ref, y1_ref,
                  wa_ref, wb_ref, wo_ref, g_ref, b_ref, y_ref):
    heads = []
    for h in range(HEADS_PER_GROUP):
        l0, l1, l2 = l0_ref[h], l1_ref[h], l2_ref[h]
        m = jnp.maximum(jnp.maximum(l0, l1), l2)
        e0, e1, e2 = jnp.exp(l0 - m), jnp.exp(l1 - m), jnp.exp(l2 - m)
        denom = e0 + e1 + e2
        heads.append(((e0 / denom) * o0_ref[h] + (e1 / denom) * o1_ref[h] + (e2 / denom) * o2_ref[h]).astype(BF16))
    out_a = oa_ref[...]
    out_b = jnp.concatenate(heads, axis=1)
    cols = [slice(c, c + PROJ_SUB) for c in range(0, D_MODEL, PROJ_SUB)]
    branches = [(jnp.dot(out_a, wa_ref[:, sl], preferred_element_type=F32),
                 jnp.dot(out_b, wb_ref[:, sl], preferred_element_type=F32)) for sl in cols]
    mix = None
    for sl, (branch_a, branch_b) in zip(cols, branches):
        merged = (ga_ref[:, sl] * branch_a + gb_ref[:, sl] * branch_b).astype(BF16)
        part = jnp.dot(merged, wo_ref[sl, :], preferred_element_type=F32)
        mix = part if mix is None else mix + part
    y_ref[...] = _layer_norm(ALPHA * y1_ref[...] + mix, g_ref[...], b_ref[...])


def _merge(out_a, outs, lses, gates, y1, wa, wb, wo, g, b, *, tm=256):
    rows = y1.shape[0]
    row_spec = lambda w, c=0: pl.BlockSpec((tm, w), lambda i: (i, c))
    full = lambda a: pl.BlockSpec(a.shape, lambda i: (0, 0), pipeline_mode=pl.Buffered(1))
    return pl.pallas_call(
        _merge_kernel,
        grid=(rows // tm,),
        in_specs=[row_spec(MLA_HEADS * V_DIM)]
                 + [pl.BlockSpec((HEADS_PER_GROUP, tm, LANES), lambda i: (0, i, 0))] * 6
                 + [row_spec(D_MODEL, 0), row_spec(D_MODEL, 1), row_spec(D_MODEL)]
                 + [full(wa), full(wb), full(wo), full(g), full(b)],
        out_specs=row_spec(D_MODEL),
        out_shape=jax.ShapeDtypeStruct((rows, D_MODEL), F32),
        compiler_params=_params(("parallel",)),
        name="merge",
    )(out_a, *outs, *lses, gates, gates, y1, wa, wb, wo, g, b)


def _ffn_in_kernel(w_ref, wg_ref, wu_ref):
    w = w_ref[...]
    pad = jnp.zeros((w.shape[0], D_FF_PAD - D_FF), BF16)
    for dst, part in ((wg_ref, w[:, :D_FF]), (wu_ref, w[:, D_FF:])):
        dst[:, :D_FF] = part.astype(BF16)
        dst[:, D_FF:] = pad


def _ffn_out_kernel(n_valid, w_ref, o_ref):
    @pl.when(pl.program_id(0) < n_valid)
    def _():
        o_ref[...] = w_ref[...].astype(BF16)

    @pl.when(pl.program_id(0) >= n_valid)
    def _():
        o_ref[...] = jnp.zeros_like(o_ref)


def _ffn_weights(w_in, w_out, *, tr=64, tc=LANES):
    half = jax.ShapeDtypeStruct((D_MODEL, D_FF_PAD), BF16)
    wg, wu = pl.pallas_call(
        _ffn_in_kernel,
        grid=(D_MODEL // tr,),
        in_specs=[pl.BlockSpec((tr, 2 * D_FF), lambda i: (i, 0))],
        out_specs=[pl.BlockSpec((tr, D_FF_PAD), lambda i: (i, 0))] * 2,
        out_shape=[half, half],
        compiler_params=_params(("parallel",)),
        name="ffn_w_in",
    )(w_in)
    n_valid = D_FF // tc
    wo = pl.pallas_call(
        functools.partial(_ffn_out_kernel, n_valid),
        grid=(D_FF_PAD // tc,),
        in_specs=[pl.BlockSpec((tc, D_MODEL), lambda i: (jnp.minimum(i, n_valid - 1), 0))],
        out_specs=pl.BlockSpec((tc, D_MODEL), lambda i: (i, 0)),
        out_shape=jax.ShapeDtypeStruct((D_FF_PAD, D_MODEL), BF16),
        compiler_params=_params(("parallel",)),
        name="ffn_w_out",
    )(w_out)
    return wg, wu, wo


def _mix_in_kernel(w_ref, lat_ref, dil_ref, gate_ref):
    w = w_ref[...]
    o_lat = Q_LORA + KV_LORA + QK_ROPE
    o_v = o_lat + 3 * DIL_WIDTH
    lat_ref[:, :o_lat] = w[:, :o_lat].astype(BF16)
    lat_ref[:, o_lat:] = jnp.zeros((w.shape[0], LAT_WIDTH - o_lat), BF16)
    dil_ref[...] = w[:, o_lat:o_v].astype(BF16)
    gate_ref[...] = w[:, o_v:].astype(BF16)


def _mix_in_weights(w_in_mix, *, tr=64):
    n = w_in_mix.shape[1]
    widths = (LAT_WIDTH, 3 * DIL_WIDTH, 2 * D_MODEL)
    return pl.pallas_call(
        _mix_in_kernel,
        grid=(D_MODEL // tr,),
        in_specs=[pl.BlockSpec((tr, n), lambda i: (i, 0))],
        out_specs=[pl.BlockSpec((tr, w), lambda i: (i, 0)) for w in widths],
        out_shape=[jax.ShapeDtypeStruct((D_MODEL, w), BF16) for w in widths],
        compiler_params=_params(("parallel",)),
        name="mix_w_in",
    )(w_in_mix)


def _rope_tables(seq):
    pos = jnp.arange(seq, dtype=F32)[:, None]

    def angles(d):
        inv_freq = ROPE_THETA ** (-jnp.arange(0, d, 2, dtype=F32) / d)
        ang = pos * inv_freq[None, :]
        return jnp.cos(ang), jnp.sin(ang)

    cos, sin = angles(DIL_HEAD_DIM)
    dil = (jnp.concatenate([cos, cos], axis=1), jnp.concatenate([-sin, sin], axis=1))
    cos, sin = angles(QK_ROPE)
    zeros = jnp.zeros_like(cos)
    pad = jnp.zeros((seq, LANES - QK_ROPE), F32)
    mla = (jnp.concatenate([cos, cos, pad], axis=1),
           jnp.concatenate([-sin, zeros, pad], axis=1),
           jnp.concatenate([zeros, sin, pad], axis=1))
    return dil, mla


def _prepare(ffn1_w_in, ffn1_w_out, ln1_g, ln1_b, w_in_mix, b_gate, q_norm_g, w_uq, kv_norm_g, w_ukv,
             w_branch_a, w_branch_b, w_out_mix, ln2_g, ln2_b, ffn2_w_in, ffn2_w_out, ln3_g, ln3_b):
    ffn = _ffn_weights
    row = lambda a: a.reshape(1, -1)
    w_lat, w_dil, w_gate = _mix_in_weights(w_in_mix)
    uq = w_uq.reshape(Q_LORA, MLA_HEADS, QK_NOPE + QK_ROPE)
    ukv = w_ukv.reshape(KV_LORA, MLA_HEADS, QK_NOPE + V_DIM)
    flat = lambda a: a.reshape(a.shape[0], -1).astype(BF16)
    return dict(
        ffn1=ffn(ffn1_w_in, ffn1_w_out), ln1=(row(ln1_g), row(ln1_b)),
        ffn2=ffn(ffn2_w_in, ffn2_w_out), ln3=(row(ln3_g), row(ln3_b)),
        w_lat=w_lat, w_dil=w_dil, w_gate=w_gate,
        b_gate=row(b_gate),
        q_norm_g=row(q_norm_g), kv_norm_g=row(kv_norm_g),
        wqn=flat(uq[:, :, :QK_NOPE]),
        wqp=flat(jnp.pad(uq[:, :, QK_NOPE:], ((0, 0), (0, 0), (0, LANES - QK_ROPE)))),
        wkn=flat(ukv[:, :, :QK_NOPE]),
        wvt=flat(ukv[:, :, QK_NOPE:]).T,
        wa=w_branch_a.astype(BF16), wb=w_branch_b.astype(BF16), wo=w_out_mix.astype(BF16),
        ln2=(row(ln2_g), row(ln2_b)),
    )


def _encoder_layer(x, p, tables):
    batch, seq, _ = x.shape
    x2 = x.reshape(batch * seq, D_MODEL)
    dil_tables, mla_tables = tables

    y1 = _ffn_ln(x2, *p["ffn1"], *p["ln1"])

    lat, y1_bf = _proj(y1, p["w_lat"], F32, tn=LAT_WIDTH)
    qkv_d = _proj(y1_bf, p["w_dil"], BF16, mode="dilated", extras=dil_tables, seq=seq, tn=DIL_WIDTH)
    gates = _proj(y1_bf, p["w_gate"], F32, mode="gate", extras=(p["b_gate"],), tn=D_MODEL)

    q, k, vt = _latent(lat, p["q_norm_g"], p["kv_norm_g"], p["wqn"], p["wqp"], p["wkn"], p["wvt"],
                       mla_tables, seq=seq)
    out_a = _mla(q, k, vt, batch=batch, seq=seq)

    outs, lses = [], []
    for group, (window, dilation) in enumerate(DIL_PATTERNS):
        o, lse = _dilated_group(qkv_d[group], group=group, window=window, dilation=dilation, batch=batch, seq=seq)
        outs.append(o)
        lses.append(lse)

    y2 = _merge(out_a, outs, lses, gates, y1, p["wa"], p["wb"], p["wo"], *p["ln2"])
    y3 = _ffn_ln(y2, *p["ffn2"], *p["ln3"])
    return y3.reshape(batch, seq, D_MODEL)


def kernel(x_prompt, x_sample, ffn1_w_in, ffn1_w_out, ln1_g, ln1_b, w_in_mix, b_gate, q_norm_g, w_uq, kv_norm_g, w_ukv, w_branch_a, w_branch_b, w_out_mix, ln2_g, ln2_b, ffn2_w_in, ffn2_w_out, ln3_g, ln3_b):
    weights = (ffn1_w_in, ffn1_w_out, ln1_g, ln1_b, w_in_mix, b_gate, q_norm_g, w_uq, kv_norm_g, w_ukv,
               w_branch_a, w_branch_b, w_out_mix, ln2_g, ln2_b, ffn2_w_in, ffn2_w_out, ln3_g, ln3_b)
    y_prompt, y_sample = x_prompt, x_sample
    tables = _rope_tables(max(x_prompt.shape[1], x_sample.shape[1]))
    for layer in range(DEPTH):
        p = _prepare(*(w[layer] for w in weights))
        y_prompt = _encoder_layer(y_prompt, p, tables)
        y_sample = _encoder_layer(y_sample, p, tables)
    return (y_prompt, y_sample)
```

```python
import functools

import jax
import jax.numpy as jnp
from jax import lax
from jax.experimental import pallas as pl
from jax.experimental.pallas import tpu as pltpu

D_MODEL = 2048
DEPTH = 1
MLA_HEADS = 8
Q_LORA = 512
KV_LORA = 512
QK_NOPE = 128
QK_ROPE = 64
V_DIM = 128
DIL_PATTERNS = ((128, 1), (512, 4), (2048, 16))
HEADS_PER_GROUP = 4
DIL_HEAD_DIM = 128
DIL_WIDTH = len(DIL_PATTERNS) * HEADS_PER_GROUP * DIL_HEAD_DIM
D_FF = 5504
ROPE_THETA = 10000.0
LN_EPS = 1e-5
RMS_EPS = 1e-6
NEG_INF = -1e30
ALPHA = (2 * DEPTH) ** 0.25
MLA_SCALE = (QK_NOPE + QK_ROPE) ** -0.5
LOG2_E = 1.4426950408889634
MLA_Q_SCALE = MLA_SCALE * LOG2_E
DIL_KINDS = 3
DIL_Q_SCALE = DIL_HEAD_DIM ** -0.5 * LOG2_E
LN_2 = 1.0 / LOG2_E

LANES = 128
D_FF_PAD = 5632
FF_CHUNK = 512
LAT_WIDTH = Q_LORA + KV_LORA + LANES
MLA_SLAB = 2 * LANES
GROUP_WIDTH = HEADS_PER_GROUP * DIL_HEAD_DIM
PROJ_SUB = 512
DIL_ROWS = 2048
DIL_HEADS_PER_STEP = 4
DIL_UNITS_PER_TRIP = 2
VMEM_LIMIT = 56 * 1024 * 1024
FFN_VMEM_LIMIT = 60 * 1024 * 1024

BF16 = jnp.bfloat16
F32 = jnp.float32
_NT = (((1,), (1,)), ((), ()))


def _params(semantics, vmem_limit=VMEM_LIMIT):
    return pltpu.CompilerParams(dimension_semantics=semantics, vmem_limit_bytes=vmem_limit)


def _layer_norm(v, g, b, scale=1.0):
    mu = jnp.mean(v, axis=-1, keepdims=True)
    c = v - mu
    var = jnp.mean(c * c, axis=-1, keepdims=True)
    return c * (scale * lax.rsqrt(scale * scale * var + LN_EPS)) * g + b


def _ffn_ln_kernel(n_chunks, x_ref, wg_ref, wu_ref, wo_ref, g_ref, b_ref, y_ref, xbf_ref):
    k = pl.program_id(1)

    def chunk(xb):
        gate = jnp.dot(xb, wg_ref[...], preferred_element_type=F32)
        up = jnp.dot(xb, wu_ref[...], preferred_element_type=F32)
        act = gate * jax.nn.sigmoid(gate) * up
        return jnp.dot(act.astype(BF16), wo_ref[...], preferred_element_type=F32)

    @pl.when(k == 0)
    def _():
        x = x_ref[...]
        xb = x.astype(BF16)
        xbf_ref[...] = xb
        y_ref[...] = (2.0 * ALPHA) * x + chunk(xb)

    @pl.when(k > 0)
    def _():
        y_ref[...] += chunk(xbf_ref[...])

    @pl.when(k == n_chunks - 1)
    def _():
        y_ref[...] = _layer_norm(y_ref[...], g_ref[...], b_ref[...], scale=0.5)


def _ffn_ln(x, wg, wu, wo, g, b, *, tm=1024):
    rows = x.shape[0]
    n_chunks = D_FF_PAD // FF_CHUNK
    return pl.pallas_call(
        functools.partial(_ffn_ln_kernel, n_chunks),
        grid=(rows // tm, n_chunks),
        in_specs=[
            pl.BlockSpec((tm, D_MODEL), lambda i, k: (i, 0)),
            pl.BlockSpec((D_MODEL, FF_CHUNK), lambda i, k: (0, k)),
            pl.BlockSpec((D_MODEL, FF_CHUNK), lambda i, k: (0, k)),
            pl.BlockSpec((FF_CHUNK, D_MODEL), lambda i, k: (k, 0)),
            pl.BlockSpec((1, D_MODEL), lambda i, k: (0, 0)),
            pl.BlockSpec((1, D_MODEL), lambda i, k: (0, 0)),
        ],
        out_specs=pl.BlockSpec((tm, D_MODEL), lambda i, k: (i, 0)),
        out_shape=jax.ShapeDtypeStruct((rows, D_MODEL), F32),
        scratch_shapes=[pltpu.VMEM((tm, D_MODEL), BF16)],
        compiler_params=_params(("parallel", "arbitrary"), FFN_VMEM_LIMIT),
        name="ffn_ln",
    )(x, wg, wu, wo, g, b)


def _proj_kernel(mode, x_ref, w_ref, *rest):
    tn = w_ref.shape[1]
    starts = range(0, tn, PROJ_SUB)

    def products(x=None):
        x = x_ref[...] if x is None else x
        return [(c, jnp.dot(x, w_ref[:, c:min(c + PROJ_SUB, tn)], preferred_element_type=F32)) for c in starts]

    if mode == "dilated":
        cos_ref, sin_ref, *o_refs, slab_ref = rest
        n_rope_steps = 2 * DIL_WIDTH // tn
        tm = x_ref.shape[0]

        def emit(rope):
            if rope:
                scale = jnp.where(pl.program_id(1) == 0, DIL_Q_SCALE, 1.0)
                cos = cos_ref[...] * scale
                sin = sin_ref[...] * scale
            for c, z in products():
                group = c // PROJ_SUB
                d = DIL_PATTERNS[group][1]
                o_ref = o_refs[group]
                for h in range(HEADS_PER_GROUP):
                    zh = z[:, h * LANES:(h + 1) * LANES]
                    if rope:
                        zh = zh * cos + pltpu.roll(zh, LANES // 2, 1) * sin
                    if d == 1:
                        o_ref[h, 0] = zh.astype(BF16)
                        continue
                    slab = group * HEADS_PER_GROUP + h
                    slab_ref[slab] = zh
                    for r in range(d):
                        o_ref[h, r] = slab_ref[slab, pl.ds(r, tm // d, stride=d), :].astype(BF16)

        pl.when(pl.program_id(1) < n_rope_steps)(functools.partial(emit, True))
        pl.when(pl.program_id(1) >= n_rope_steps)(functools.partial(emit, False))
    elif mode == "gate":
        b_ref, o_ref = rest
        for c, z in products():
            sl = slice(c, c + z.shape[1])
            o_ref[:, sl] = jax.nn.sigmoid(z + b_ref[:, sl]).astype(o_ref.dtype)
    else:
        o_ref, xbf_ref = rest
        x = x_ref[...].astype(BF16)
        xbf_ref[...] = x
        for c, z in products(x):
            o_ref[:, c:c + z.shape[1]] = z.astype(o_ref.dtype)


def _proj(x, w, out_dtype, *, mode="cast", extras=(), seq=None, tm=1024, tn=512):
    rows = x.shape[0]
    n = w.shape[1]
    tn = min(tn, n)
    assert rows % tm == 0 and n % tn == 0
    in_specs = [
        pl.BlockSpec((tm, D_MODEL), lambda i, j: (i, 0)),
        pl.BlockSpec((D_MODEL, tn), lambda i, j: (0, j)),
    ]
    out_specs = pl.BlockSpec((tm, tn), lambda i, j: (i, j))
    out_shape = jax.ShapeDtypeStruct((rows, n), out_dtype)
    scratch_shapes = []
    if mode == "dilated":
        assert tn == DIL_WIDTH and PROJ_SUB == GROUP_WIDTH
        tiles_per_seq = seq // tm
        in_specs += [pl.BlockSpec((tm, LANES), lambda i, j: (i % tiles_per_seq, 0))] * 2
        out_specs = [
            pl.BlockSpec((HEADS_PER_GROUP, None, d, tm // d, LANES),
                         lambda i, j: (j, i // tiles_per_seq, 0, i % tiles_per_seq, 0))
            for _, d in DIL_PATTERNS]
        out_shape = [jax.ShapeDtypeStruct((DIL_KINDS * HEADS_PER_GROUP, rows // seq, d, seq // d, LANES), out_dtype)
                     for _, d in DIL_PATTERNS]
        scratch_shapes = [pltpu.VMEM((len(DIL_PATTERNS) * HEADS_PER_GROUP, tm, LANES), F32)]
    elif mode == "gate":
        in_specs += [pl.BlockSpec((1, tn), lambda i, j: (0, j))]
    else:
        assert tn == n
        out_specs = [out_specs, pl.BlockSpec((tm, D_MODEL), lambda i, j: (i, 0))]
        out_shape = [out_shape, jax.ShapeDtypeStruct((rows, D_MODEL), BF16)]
    return pl.pallas_call(
        functools.partial(_proj_kernel, mode),
        grid=(rows // tm, n // tn),
        in_specs=in_specs,
        out_specs=out_specs,
        out_shape=out_shape,
        scratch_shapes=scratch_shapes,
        compiler_params=_params(("parallel", "arbitrary")),
        name="proj_" + mode,
    )(x, w, *extras)


def _latent_kernel(lat_ref, qg_ref, kvg_ref, wqn_ref, wqp_ref, wkn_ref, wvt_ref,
                   cos_ref, sin_lo_ref, sin_hi_ref, q_ref, k_ref, vt_ref):
    lat = lat_ref[...]
    c_q = lat[:, :Q_LORA]
    c_kv = lat[:, Q_LORA:Q_LORA + KV_LORA]
    k_rope = lat[:, Q_LORA + KV_LORA:]

    def rms(x, g):
        return (x * lax.rsqrt(jnp.mean(x * x, axis=-1, keepdims=True) + RMS_EPS) * g).astype(BF16)

    cos = cos_ref[...]
    sin_lo = sin_lo_ref[...]
    sin_hi = sin_hi_ref[...]

    def rope64(x):
        quarter = QK_ROPE // 2
        return (x * cos + pltpu.roll(x, LANES - quarter, 1) * sin_lo
                + pltpu.roll(x, quarter, 1) * sin_hi)

    nq = rms(c_q, qg_ref[...])
    nkv = rms(c_kv, kvg_ref[...])
    q_nope = jnp.dot(nq, wqn_ref[...], preferred_element_type=F32)
    q_pe = jnp.dot(nq, wqp_ref[...], preferred_element_type=F32)
    k_nope = jnp.dot(nkv, wkn_ref[...], preferred_element_type=F32)
    vt = lax.dot_general(wvt_ref[...], nkv, _NT, preferred_element_type=F32)
    k_pe = rope64(k_rope).astype(BF16)
    for h in range(MLA_HEADS):
        src = slice(h * LANES, (h + 1) * LANES)
        lo = slice(h * MLA_SLAB, h * MLA_SLAB + LANES)
        hi = slice(h * MLA_SLAB + LANES, (h + 1) * MLA_SLAB)
        q_ref[:, lo] = (q_nope[:, src] * MLA_Q_SCALE).astype(BF16)
        q_ref[:, hi] = (rope64(q_pe[:, src]) * MLA_Q_SCALE).astype(BF16)
        k_ref[:, lo] = k_nope[:, src].astype(BF16)
        k_ref[:, hi] = k_pe
    vt_ref[...] = vt.astype(BF16)


def _latent(lat, qg, kvg, wqn, wqp, wkn, wvt, tables, *, seq, tm=1024):
    rows = lat.shape[0]
    tiles_per_seq = seq // tm
    width = MLA_HEADS * LANES
    row_spec = lambda w: pl.BlockSpec((tm, w), lambda i: (i, 0))
    full = lambda a: pl.BlockSpec(a.shape, lambda i: (0, 0), pipeline_mode=pl.Buffered(1))
    table_spec = pl.BlockSpec((tm, LANES), lambda i: (i % tiles_per_seq, 0))
    return pl.pallas_call(
        _latent_kernel,
        grid=(rows // tm,),
        in_specs=[row_spec(LAT_WIDTH), full(qg), full(kvg), full(wqn), full(wqp), full(wkn), full(wvt),
                  table_spec, table_spec, table_spec],
        out_specs=[row_spec(MLA_HEADS * MLA_SLAB), row_spec(MLA_HEADS * MLA_SLAB),
                   pl.BlockSpec((width, tm), lambda i: (0, i))],
        out_shape=[jax.ShapeDtypeStruct((rows, MLA_HEADS * MLA_SLAB), BF16),
                   jax.ShapeDtypeStruct((rows, MLA_HEADS * MLA_SLAB), BF16),
                   jax.ShapeDtypeStruct((width, rows), BF16)],
        compiler_params=_params(("parallel",)),
        name="latent",
    )(lat, qg, kvg, wqn, wqp, wkn, wvt, *tables)


def _mla_kernel(n_kv, tk, q_ref, k_ref, vt_ref, o_ref, acc_ref, s_ref, mblk_ref):
    n_sub, _, ts = acc_ref.shape
    acc_ref[...] = jnp.zeros_like(acc_ref)

    def scores_into(slot, j):
        k = k_ref[pl.ds(pl.multiple_of(j * tk, tk), tk), :]
        for t in range(n_sub):
            s = lax.dot_general(k, q_ref[pl.ds(t * ts, ts), :], _NT, preferred_element_type=F32)
            s_ref[slot, t] = s
            mblk_ref[slot, t] = jnp.max(s, axis=0, keepdims=True)

    def consume(slot, j, carry):
        vt = vt_ref[:, pl.ds(pl.multiple_of(j * tk, tk), tk)]
        out = []
        for t in range(n_sub):
            m_prev, l_prev = carry[t]
            m_new = jnp.maximum(m_prev, mblk_ref[slot, t])
            alpha = jnp.exp2(m_prev - m_new)
            p = jnp.exp2(s_ref[slot, t] - m_new)
            l_new = alpha * l_prev + jnp.sum(p, axis=0, keepdims=True)
            acc_ref[t] = alpha * acc_ref[t] + jnp.dot(vt, p.astype(BF16), preferred_element_type=F32)
            out.append((m_new, l_new))
        return tuple(out)

    def body(jj, carry):
        j = 2 * jj
        scores_into(1, j + 1)
        carry = consume(0, j, carry)
        scores_into(0, j + 2)
        return consume(1, j + 1, carry)

    scores_into(0, 0)
    init = tuple((jnp.full((1, ts), -jnp.inf, F32), jnp.zeros((1, ts), F32)) for _ in range(n_sub))
    carry = lax.fori_loop(0, n_kv // 2 - 1, body, init)
    scores_into(1, n_kv - 1)
    carry = consume(0, n_kv - 2, carry)
    carry = consume(1, n_kv - 1, carry)
    for t in range(n_sub):
        o_ref[pl.ds(t * ts, ts), :] = (acc_ref[t] / carry[t][1]).T.astype(o_ref.dtype)


def _mla(q, k, vt, *, batch, seq, tq=2048, ts=512, tk=1024):
    assert seq % tq == 0 and tq % ts == 0 and seq % (2 * tk) == 0
    n_q = seq // tq
    return pl.pallas_call(
        functools.partial(_mla_kernel, seq // tk, tk),
        grid=(batch, MLA_HEADS, n_q),
        in_specs=[
            pl.BlockSpec((tq, MLA_SLAB), lambda b, h, i: (b * n_q + i, h)),
            pl.BlockSpec((seq, MLA_SLAB), lambda b, h, i: (b, h)),
            pl.BlockSpec((V_DIM, seq), lambda b, h, i: (h, b)),
        ],
        out_specs=pl.BlockSpec((tq, V_DIM), lambda b, h, i: (b * n_q + i, h)),
        out_shape=jax.ShapeDtypeStruct((batch * seq, MLA_HEADS * V_DIM), BF16),
        scratch_shapes=[pltpu.VMEM((tq // ts, V_DIM, ts), F32), pltpu.VMEM((2, tq // ts, tk, ts), F32),
                        pltpu.VMEM((2, tq // ts, 1, ts), F32)],
        compiler_params=_params(("parallel", "parallel", "arbitrary")),
        name="mla",
    )(q, k, vt)


def _dilated_kernel(d, half, stream_len, q_ref, kp_ref, kc_ref, kn_ref, vp_ref, vc_ref, vn_ref,
                    o_ref, lse_ref, ks_ref, vs_ref):
    n_heads = q_ref.shape[0]
    tl = DIL_ROWS // d
    i = pl.program_id(2)
    for dst, before, cur, after in ((ks_ref, kp_ref, kc_ref, kn_ref), (vs_ref, vp_ref, vc_ref, vn_ref)):
        dst[:, :, 0:half] = before[...]
        dst[:, :, half:half + tl] = cur[...]
        dst[:, :, half + tl:] = after[...]

    chunk = 2 * half
    row = lax.broadcasted_iota(jnp.int32, (chunk, 2 * chunk), 0)
    col = lax.broadcasted_iota(jnp.int32, (chunk, 2 * chunk), 1)
    band = (col >= row) & (col <= row + 2 * half)
    shift = d.bit_length() - 1

    def unit_group(ug, carry):
        jobs = []
        for uu in range(DIL_UNITS_PER_TRIP):
            u = ug * DIL_UNITS_PER_TRIP + uu
            r = u & (d - 1)
            c = u >> shift
            row0 = pl.multiple_of(c * chunk, chunk)
            k_idx = i * tl + c * chunk - half + col
            mask = band & (k_idx >= 0) & (k_idx < stream_len)
            for h in range(n_heads):
                q = q_ref[h, r, pl.ds(row0, chunk), :]
                k = ks_ref[h, r, pl.ds(row0, 2 * chunk), :]
                s = lax.dot_general(q, k, _NT, preferred_element_type=F32)
                jobs.append((h, r, row0, mask, s))
        for h, r, row0, mask, s in jobs:
            s = jnp.where(mask, s, NEG_INF)
            m = jnp.max(s, axis=1, keepdims=True)
            p = jnp.exp2(s - m)
            denom = jnp.sum(p, axis=1, keepdims=True)
            o = jnp.dot(p.astype(BF16), vs_ref[h, r, pl.ds(row0, 2 * chunk), :], preferred_element_type=F32) / denom
            start = r + d * row0
            o_ref[h, pl.ds(start, chunk, stride=d), :] = o
            lse = m * LN_2 + jnp.log(denom)
            lse_ref[h, pl.ds(start, chunk, stride=d), :] = jnp.broadcast_to(lse, (chunk, LANES))
        return carry

    lax.fori_loop(0, DIL_ROWS // chunk // DIL_UNITS_PER_TRIP, unit_group, 0)


def _dilated_group(qkv, *, group, window, dilation, batch, seq):
    d = dilation
    half = window // (2 * d)
    assert seq % DIL_ROWS == 0 and d & (d - 1) == 0 and DIL_ROWS % (2 * half * d) == 0
    assert (DIL_ROWS // (2 * half)) % DIL_UNITS_PER_TRIP == 0
    n_r = seq // DIL_ROWS
    tl = DIL_ROWS // d
    halo_per_step = tl // half
    n_halo = seq // d // half
    steps_per_group = HEADS_PER_GROUP // DIL_HEADS_PER_STEP
    blk = (DIL_HEADS_PER_STEP, None, d, tl, LANES)
    hblk = (DIL_HEADS_PER_STEP, None, d, half, LANES)

    def specs(kind):
        heads = lambda g: kind * steps_per_group + g
        cur = pl.BlockSpec(blk, lambda g, b, i: (heads(g), b, 0, i, 0))
        before = pl.BlockSpec(hblk, lambda g, b, i: (heads(g), b, 0, jnp.maximum(i * halo_per_step - 1, 0), 0))
        after = pl.BlockSpec(hblk, lambda g, b, i: (heads(g), b, 0,
                                                    jnp.minimum((i + 1) * halo_per_step, n_halo - 1), 0))
        return before, cur, after

    out_spec = pl.BlockSpec((DIL_HEADS_PER_STEP, DIL_ROWS, LANES), lambda g, b, i: (g, b * n_r + i, 0))
    out_shape = jax.ShapeDtypeStruct((HEADS_PER_GROUP, batch * seq, LANES), F32)
    return pl.pallas_call(
        functools.partial(_dilated_kernel, d, half, seq // d),
        grid=(steps_per_group, batch, n_r),
        in_specs=[specs(0)[1], *specs(1), *specs(2)],
        out_specs=[out_spec, out_spec],
        out_shape=[out_shape, out_shape],
        scratch_shapes=[pltpu.VMEM((DIL_HEADS_PER_STEP, d, tl + 2 * half, LANES), BF16),
                        pltpu.VMEM((DIL_HEADS_PER_STEP, d, tl + 2 * half, LANES), BF16)],
        compiler_params=_params(("parallel", "parallel", "parallel")),
        name="dilated_g%d" % group,
    )(*([qkv] * 7))


def _merge_kernel(oa_ref, o0_ref, o1_ref, o2_ref, l0_ref, l1_ref, l2_ref, ga_ref, gb_ref, y1_ref,
                  wa_ref, wb_ref, wo_ref, g_ref, b_ref, y_ref):
    heads = []
    for h in range(HEADS_PER_GROUP):
        l0, l1, l2 = l0_ref[h], l1_ref[h], l2_ref[h]
        m = jnp.maximum(jnp.maximum(l0, l1), l2)
        e0, e1, e2 = jnp.exp(l0 - m), jnp.exp(l1 - m), jnp.exp(l2 - m)
        denom = e0 + e1 + e2
        heads.append(((e0 / denom) * o0_ref[h] + (e1 / denom) * o1_ref[h] + (e2 / denom) * o2_ref[h]).astype(BF16))
    out_a = oa_ref[...]
    out_b = jnp.concatenate(heads, axis=1)
    cols = [slice(c, c + PROJ_SUB) for c in range(0, D_MODEL, PROJ_SUB)]
    branches = [(jnp.dot(out_a, wa_ref[:, sl], preferred_element_type=F32),
                 jnp.dot(out_b, wb_ref[:, sl], preferred_element_type=F32)) for sl in cols]
    mix = None
    for sl, (branch_a, branch_b) in zip(cols, branches):
        merged = (ga_ref[:, sl] * branch_a + gb_ref[:, sl] * branch_b).astype(BF16)
        part = jnp.dot(merged, wo_ref[sl, :], preferred_element_type=F32)
        mix = part if mix is None else mix + part
    y_ref[...] = _layer_norm(ALPHA * y1_ref[...] + mix, g_ref[...], b_ref[...])


def _merge(out_a, outs, lses, gates, y1, wa, wb, wo, g, b, *, tm=256):
    rows = y1.shape[0]
    row_spec = lambda w, c=0: pl.BlockSpec((tm, w), lambda i: (i, c))
    full = lambda a: pl.BlockSpec(a.shape, lambda i: (0, 0), pipeline_mode=pl.Buffered(1))
    return pl.pallas_call(
        _merge_kernel,
        grid=(rows // tm,),
        in_specs=[row_spec(MLA_HEADS * V_DIM)]
                 + [pl.BlockSpec((HEADS_PER_GROUP, tm, LANES), lambda i: (0, i, 0))] * 6
                 + [row_spec(D_MODEL, 0), row_spec(D_MODEL, 1), row_spec(D_MODEL)]
                 + [full(wa), full(wb), full(wo), full(g), full(b)],
        out_specs=row_spec(D_MODEL),
        out_shape=jax.ShapeDtypeStruct((rows, D_MODEL), F32),
        compiler_params=_params(("parallel",)),
        name="merge",
    )(out_a, *outs, *lses, gates, gates, y1, wa, wb, wo, g, b)


def _ffn_in_kernel(w_ref, wg_ref, wu_ref):
    w = w_ref[...]
    pad = jnp.zeros((w.shape[0], D_FF_PAD - D_FF), BF16)
    for dst, part in ((wg_ref, w[:, :D_FF]), (wu_ref, w[:, D_FF:])):
        dst[:, :D_FF] = part.astype(BF16)
        dst[:, D_FF:] = pad


def _ffn_out_kernel(n_valid, w_ref, o_ref):
    @pl.when(pl.program_id(0) < n_valid)
    def _():
        o_ref[...] = w_ref[...].astype(BF16)

    @pl.when(pl.program_id(0) >= n_valid)
    def _():
        o_ref[...] = jnp.zeros_like(o_ref)


def _ffn_weights(w_in, w_out, *, tr=64, tc=LANES):
    half = jax.ShapeDtypeStruct((D_MODEL, D_FF_PAD), BF16)
    wg, wu = pl.pallas_call(
        _ffn_in_kernel,
        grid=(D_MODEL // tr,),
        in_specs=[pl.BlockSpec((tr, 2 * D_FF), lambda i: (i, 0))],
        out_specs=[pl.BlockSpec((tr, D_FF_PAD), lambda i: (i, 0))] * 2,
        out_shape=[half, half],
        compiler_params=_params(("parallel",)),
        name="ffn_w_in",
    )(w_in)
    n_valid = D_FF // tc
    wo = pl.pallas_call(
        functools.partial(_ffn_out_kernel, n_valid),
        grid=(D_FF_PAD // tc,),
        in_specs=[pl.BlockSpec((tc, D_MODEL), lambda i: (jnp.minimum(i, n_valid - 1), 0))],
        out_specs=pl.BlockSpec((tc, D_MODEL), lambda i: (i, 0)),
        out_shape=jax.ShapeDtypeStruct((D_FF_PAD, D_MODEL), BF16),
        compiler_params=_params(("parallel",)),
        name="ffn_w_out",
    )(w_out)
    return wg, wu, wo


def _mix_in_kernel(w_ref, lat_ref, dil_ref, gate_ref):
    w = w_ref[...]
    o_lat = Q_LORA + KV_LORA + QK_ROPE
    o_v = o_lat + 3 * DIL_WIDTH
    lat_ref[:, :o_lat] = w[:, :o_lat].astype(BF16)
    lat_ref[:, o_lat:] = jnp.zeros((w.shape[0], LAT_WIDTH - o_lat), BF16)
    dil_ref[...] = w[:, o_lat:o_v].astype(BF16)
    gate_ref[...] = w[:, o_v:].astype(BF16)


def _mix_in_weights(w_in_mix, *, tr=64):
    n = w_in_mix.shape[1]
    widths = (LAT_WIDTH, 3 * DIL_WIDTH, 2 * D_MODEL)
    return pl.pallas_call(
        _mix_in_kernel,
        grid=(D_MODEL // tr,),
        in_specs=[pl.BlockSpec((tr, n), lambda i: (i, 0))],
        out_specs=[pl.BlockSpec((tr, w), lambda i: (i, 0)) for w in widths],
        out_shape=[jax.ShapeDtypeStruct((D_MODEL, w), BF16) for w in widths],
        compiler_params=_params(("parallel",)),
        name="mix_w_in",
    )(w_in_mix)


def _rope_tables(seq):
    pos = jnp.arange(seq, dtype=F32)[:, None]

    def angles(d):
        inv_freq = ROPE_THETA ** (-jnp.arange(0, d, 2, dtype=F32) / d)
        ang = pos * inv_freq[None, :]
        return jnp.cos(ang), jnp.sin(ang)

    cos, sin = angles(DIL_HEAD_DIM)
    dil = (jnp.concatenate([cos, cos], axis=1), jnp.concatenate([-sin, sin], axis=1))
    cos, sin = angles(QK_ROPE)
    zeros = jnp.zeros_like(cos)
    pad = jnp.zeros((seq, LANES - QK_ROPE), F32)
    mla = (jnp.concatenate([cos, cos, pad], axis=1),
           jnp.concatenate([-sin, zeros, pad], axis=1),
           jnp.concatenate([zeros, sin, pad], axis=1))
    return dil, mla


def _prepare(ffn1_w_in, ffn1_w_out, ln1_g, ln1_b, w_in_mix, b_gate, q_norm_g, w_uq, kv_norm_g, w_ukv,
             w_branch_a, w_branch_b, w_out_mix, ln2_g, ln2_b, ffn2_w_in, ffn2_w_out, ln3_g, ln3_b):
    ffn = _ffn_weights
    row = lambda a: a.reshape(1, -1)
    w_lat, w_dil, w_gate = _mix_in_weights(w_in_mix)
    uq = w_uq.reshape(Q_LORA, MLA_HEADS, QK_NOPE + QK_ROPE)
    ukv = w_ukv.reshape(KV_LORA, MLA_HEADS, QK_NOPE + V_DIM)
    flat = lambda a: a.reshape(a.shape[0], -1).astype(BF16)
    return dict(
        ffn1=ffn(ffn1_w_in, ffn1_w_out), ln1=(row(ln1_g), row(ln1_b)),
        ffn2=ffn(ffn2_w_in, ffn2_w_out), ln3=(row(ln3_g), row(ln3_b)),
        w_lat=w_lat, w_dil=w_dil, w_gate=w_gate,
        b_gate=row(b_gate),
        q_norm_g=row(q_norm_g), kv_norm_g=row(kv_norm_g),
        wqn=flat(uq[:, :, :QK_NOPE]),
        wqp=flat(jnp.pad(uq[:, :, QK_NOPE:], ((0, 0), (0, 0), (0, LANES - QK_ROPE)))),
        wkn=flat(ukv[:, :, :QK_NOPE]),
        wvt=flat(ukv[:, :, QK_NOPE:]).T,
        wa=w_branch_a.astype(BF16), wb=w_branch_b.astype(BF16), wo=w_out_mix.astype(BF16),
        ln2=(row(ln2_g), row(ln2_b)),
    )


def _encoder_layer(x, p, tables):
    batch, seq, _ = x.shape
    x2 = x.reshape(batch * seq, D_MODEL)
    dil_tables, mla_tables = tables

    y1 = _ffn_ln(x2, *p["ffn1"], *p["ln1"])

    lat, y1_bf = _proj(y1, p["w_lat"], F32, tn=LAT_WIDTH)
    qkv_d = _proj(y1_bf, p["w_dil"], BF16, mode="dilated", extras=dil_tables, seq=seq, tn=DIL_WIDTH)
    gates = _proj(y1_bf, p["w_gate"], F32, mode="gate", extras=(p["b_gate"],), tn=D_MODEL)

    q, k, vt = _latent(lat, p["q_norm_g"], p["kv_norm_g"], p["wqn"], p["wqp"], p["wkn"], p["wvt"],
                       mla_tables, seq=seq)
    out_a = _mla(q, k, vt, batch=batch, seq=seq)

    outs, lses = [], []
    for group, (window, dilation) in enumerate(DIL_PATTERNS):
        o, lse = _dilated_group(qkv_d[group], group=group, window=window, dilation=dilation, batch=batch, seq=seq)
        outs.append(o)
        lses.append(lse)

    y2 = _merge(out_a, outs, lses, gates, y1, p["wa"], p["wb"], p["wo"], *p["ln2"])
    y3 = _ffn_ln(y2, *p["ffn2"], *p["ln3"])
    return y3.reshape(batch, seq, D_MODEL)


def kernel(x_prompt, x_sample, ffn1_w_in, ffn1_w_out, ln1_g, ln1_b, w_in_mix, b_gate, q_norm_g, w_uq, kv_norm_g, w_ukv, w_branch_a, w_branch_b, w_out_mix, ln2_g, ln2_b, ffn2_w_in, ffn2_w_out, ln3_g, ln3_b):
    weights = (ffn1_w_in, ffn1_w_out, ln1_g, ln1_b, w_in_mix, b_gate, q_norm_g, w_uq, kv_norm_g, w_ukv,
               w_branch_a, w_branch_b, w_out_mix, ln2_g, ln2_b, ffn2_w_in, ffn2_w_out, ln3_g, ln3_b)
    y_prompt, y_sample = x_prompt, x_sample
    tables = _rope_tables(max(x_prompt.shape[1], x_sample.shape[1]))
    for layer in range(DEPTH):
        p = _prepare(*(w[layer] for w in weights))
        y_prompt = _encoder_layer(y_prompt, p, tables)
        y_sample = _encoder_layer(y_sample, p, tables)
    return (y_prompt, y_sample)
```

```python
import functools

import jax
import jax.numpy as jnp
from jax import lax
from jax.experimental import pallas as pl
from jax.experimental.pallas import tpu as pltpu

D_MODEL = 2048
DEPTH = 1
MLA_HEADS = 8
Q_LORA = 512
KV_LORA = 512
QK_NOPE = 128
QK_ROPE = 64
V_DIM = 128
DIL_PATTERNS = ((128, 1), (512, 4), (2048, 16))
HEADS_PER_GROUP = 4
DIL_HEAD_DIM = 128
DIL_WIDTH = len(DIL_PATTERNS) * HEADS_PER_GROUP * DIL_HEAD_DIM
D_FF = 5504
ROPE_THETA = 10000.0
LN_EPS = 1e-5
RMS_EPS = 1e-6
NEG_INF = -1e30
ALPHA = (2 * DEPTH) ** 0.25
MLA_SCALE = (QK_NOPE + QK_ROPE) ** -0.5
LOG2_E = 1.4426950408889634
MLA_Q_SCALE = MLA_SCALE * LOG2_E
DIL_KINDS = 3
DIL_Q_SCALE = DIL_HEAD_DIM ** -0.5 * LOG2_E
LN_2 = 1.0 / LOG2_E

LANES = 128
D_FF_PAD = 5632
FF_CHUNK = 512
LAT_WIDTH = Q_LORA + KV_LORA + LANES
MLA_SLAB = 2 * LANES
GROUP_WIDTH = HEADS_PER_GROUP * DIL_HEAD_DIM
LSE_LANES = LANES // HEADS_PER_GROUP
PROJ_SUB = 512
DIL_ROWS = 2048
DIL_HEADS_PER_STEP = 4
DIL_UNITS_PER_TRIP = 2
VMEM_LIMIT = 56 * 1024 * 1024
FFN_VMEM_LIMIT = 60 * 1024 * 1024

BF16 = jnp.bfloat16
F32 = jnp.float32
_NT = (((1,), (1,)), ((), ()))


def _params(semantics, vmem_limit=VMEM_LIMIT):
    return pltpu.CompilerParams(dimension_semantics=semantics, vmem_limit_bytes=vmem_limit)


def _layer_norm(v, g, b, scale=1.0):
    mu = jnp.mean(v, axis=-1, keepdims=True)
    c = v - mu
    var = jnp.mean(c * c, axis=-1, keepdims=True)
    return c * (scale * lax.rsqrt(scale * scale * var + LN_EPS)) * g + b


def _ffn_ln_kernel(n_chunks, x_ref, wg_ref, wu_ref, wo_ref, g_ref, b_ref, y_ref, xbf_ref):
    k = pl.program_id(1)

    def chunk(xb):
        gate = jnp.dot(xb, wg_ref[...], preferred_element_type=F32)
        up = jnp.dot(xb, wu_ref[...], preferred_element_type=F32)
        act = gate * jax.nn.sigmoid(gate) * up
        return jnp.dot(act.astype(BF16), wo_ref[...], preferred_element_type=F32)

    @pl.when(k == 0)
    def _():
        x = x_ref[...]
        xb = x.astype(BF16)
        xbf_ref[...] = xb
        y_ref[...] = (2.0 * ALPHA) * x + chunk(xb)

    @pl.when(k > 0)
    def _():
        y_ref[...] += chunk(xbf_ref[...])

    @pl.when(k == n_chunks - 1)
    def _():
        y_ref[...] = _layer_norm(y_ref[...], g_ref[...], b_ref[...], scale=0.5)


def _ffn_ln(x, wg, wu, wo, g, b, *, tm=1024):
    rows = x.shape[0]
    n_chunks = D_FF_PAD // FF_CHUNK
    return pl.pallas_call(
        functools.partial(_ffn_ln_kernel, n_chunks),
        grid=(rows // tm, n_chunks),
        in_specs=[
            pl.BlockSpec((tm, D_MODEL), lambda i, k: (i, 0)),
            pl.BlockSpec((D_MODEL, FF_CHUNK), lambda i, k: (0, k)),
            pl.BlockSpec((D_MODEL, FF_CHUNK), lambda i, k: (0, k)),
            pl.BlockSpec((FF_CHUNK, D_MODEL), lambda i, k: (k, 0)),
            pl.BlockSpec((1, D_MODEL), lambda i, k: (0, 0)),
            pl.BlockSpec((1, D_MODEL), lambda i, k: (0, 0)),
        ],
        out_specs=pl.BlockSpec((tm, D_MODEL), lambda i, k: (i, 0)),
        out_shape=jax.ShapeDtypeStruct((rows, D_MODEL), F32),
        scratch_shapes=[pltpu.VMEM((tm, D_MODEL), BF16)],
        compiler_params=_params(("parallel", "arbitrary"), FFN_VMEM_LIMIT),
        name="ffn_ln",
    )(x, wg, wu, wo, g, b)


def _proj_kernel(mode, x_ref, w_ref, *rest):
    tn = w_ref.shape[1]
    starts = range(0, tn, PROJ_SUB)

    def products(x=None):
        x = x_ref[...] if x is None else x
        return [(c, jnp.dot(x, w_ref[:, c:min(c + PROJ_SUB, tn)], preferred_element_type=F32)) for c in starts]

    if mode == "dilated":
        cos_ref, sin_ref, *o_refs, slab_ref = rest
        n_rope_steps = 2 * DIL_WIDTH // tn
        tm = x_ref.shape[0]

        def emit(rope):
            if rope:
                scale = jnp.where(pl.program_id(1) == 0, DIL_Q_SCALE, 1.0)
                cos = cos_ref[...] * scale
                sin = sin_ref[...] * scale
            for c, z in products():
                group = c // PROJ_SUB
                d = DIL_PATTERNS[group][1]
                o_ref = o_refs[group]
                for h in range(HEADS_PER_GROUP):
                    zh = z[:, h * LANES:(h + 1) * LANES]
                    if rope:
                        zh = zh * cos + pltpu.roll(zh, LANES // 2, 1) * sin
                    if d == 1:
                        o_ref[h, 0] = zh.astype(BF16)
                        continue
                    slab = group * HEADS_PER_GROUP + h
                    slab_ref[slab] = zh
                    for r in range(d):
                        o_ref[h, r] = slab_ref[slab, pl.ds(r, tm // d, stride=d), :].astype(BF16)

        pl.when(pl.program_id(1) < n_rope_steps)(functools.partial(emit, True))
        pl.when(pl.program_id(1) >= n_rope_steps)(functools.partial(emit, False))
    elif mode == "gate":
        b_ref, o_ref = rest
        for c, z in products():
            sl = slice(c, c + z.shape[1])
            o_ref[:, sl] = jax.nn.sigmoid(z + b_ref[:, sl]).astype(o_ref.dtype)
    else:
        o_ref, xbf_ref = rest
        x = x_ref[...].astype(BF16)
        xbf_ref[...] = x
        for c, z in products(x):
            o_ref[:, c:c + z.shape[1]] = z.astype(o_ref.dtype)


def _proj(x, w, out_dtype, *, mode="cast", extras=(), seq=None, tm=1024, tn=512):
    rows = x.shape[0]
    n = w.shape[1]
    tn = min(tn, n)
    assert rows % tm == 0 and n % tn == 0
    in_specs = [
        pl.BlockSpec((tm, D_MODEL), lambda i, j: (i, 0)),
        pl.BlockSpec((D_MODEL, tn), lambda i, j: (0, j)),
    ]
    out_specs = pl.BlockSpec((tm, tn), lambda i, j: (i, j))
    out_shape = jax.ShapeDtypeStruct((rows, n), out_dtype)
    scratch_shapes = []
    if mode == "dilated":
        assert tn == DIL_WIDTH and PROJ_SUB == GROUP_WIDTH
        tiles_per_seq = seq // tm
        in_specs += [pl.BlockSpec((tm, LANES), lambda i, j: (i % tiles_per_seq, 0))] * 2
        out_specs = [
            pl.BlockSpec((HEADS_PER_GROUP, None, d, tm // d, LANES),
                         lambda i, j: (j, i // tiles_per_seq, 0, i % tiles_per_seq, 0))
            for _, d in DIL_PATTERNS]
        out_shape = [jax.ShapeDtypeStruct((DIL_KINDS * HEADS_PER_GROUP, rows // seq, d, seq // d, LANES), out_dtype)
                     for _, d in DIL_PATTERNS]
        scratch_shapes = [pltpu.VMEM((len(DIL_PATTERNS) * HEADS_PER_GROUP, tm, LANES), F32)]
    elif mode == "gate":
        in_specs += [pl.BlockSpec((1, tn), lambda i, j: (0, j))]
    else:
        assert tn == n
        out_specs = [out_specs, pl.BlockSpec((tm, D_MODEL), lambda i, j: (i, 0))]
        out_shape = [out_shape, jax.ShapeDtypeStruct((rows, D_MODEL), BF16)]
    return pl.pallas_call(
        functools.partial(_proj_kernel, mode),
        grid=(rows // tm, n // tn),
        in_specs=in_specs,
        out_specs=out_specs,
        out_shape=out_shape,
        scratch_shapes=scratch_shapes,
        compiler_params=_params(("parallel", "arbitrary")),
        name="proj_" + mode,
    )(x, w, *extras)


def _latent_kernel(lat_ref, qg_ref, kvg_ref, wqn_ref, wqp_ref, wkn_ref, wvt_ref,
                   cos_ref, sin_lo_ref, sin_hi_ref, q_ref, k_ref, vt_ref):
    lat = lat_ref[...]
    c_q = lat[:, :Q_LORA]
    c_kv = lat[:, Q_LORA:Q_LORA + KV_LORA]
    k_rope = lat[:, Q_LORA + KV_LORA:]

    def rms(x, g):
        return (x * lax.rsqrt(jnp.mean(x * x, axis=-1, keepdims=True) + RMS_EPS) * g).astype(BF16)

    cos = cos_ref[...]
    sin_lo = sin_lo_ref[...]
    sin_hi = sin_hi_ref[...]

    def rope64(x):
        quarter = QK_ROPE // 2
        return (x * cos + pltpu.roll(x, LANES - quarter, 1) * sin_lo
                + pltpu.roll(x, quarter, 1) * sin_hi)

    nq = rms(c_q, qg_ref[...])
    nkv = rms(c_kv, kvg_ref[...])
    q_nope = jnp.dot(nq, wqn_ref[...], preferred_element_type=F32)
    q_pe = jnp.dot(nq, wqp_ref[...], preferred_element_type=F32)
    k_nope = jnp.dot(nkv, wkn_ref[...], preferred_element_type=F32)
    vt = lax.dot_general(wvt_ref[...], nkv, _NT, preferred_element_type=F32)
    k_pe = rope64(k_rope).astype(BF16)
    for h in range(MLA_HEADS):
        src = slice(h * LANES, (h + 1) * LANES)
        lo = slice(h * MLA_SLAB, h * MLA_SLAB + LANES)
        hi = slice(h * MLA_SLAB + LANES, (h + 1) * MLA_SLAB)
        q_ref[:, lo] = (q_nope[:, src] * MLA_Q_SCALE).astype(BF16)
        q_ref[:, hi] = (rope64(q_pe[:, src]) * MLA_Q_SCALE).astype(BF16)
        k_ref[:, lo] = k_nope[:, src].astype(BF16)
        k_ref[:, hi] = k_pe
    vt_ref[...] = vt.astype(BF16)


def _latent(lat, qg, kvg, wqn, wqp, wkn, wvt, tables, *, seq, tm=1024):
    rows = lat.shape[0]
    tiles_per_seq = seq // tm
    width = MLA_HEADS * LANES
    row_spec = lambda w: pl.BlockSpec((tm, w), lambda i: (i, 0))
    full = lambda a: pl.BlockSpec(a.shape, lambda i: (0, 0), pipeline_mode=pl.Buffered(1))
    table_spec = pl.BlockSpec((tm, LANES), lambda i: (i % tiles_per_seq, 0))
    return pl.pallas_call(
        _latent_kernel,
        grid=(rows // tm,),
        in_specs=[row_spec(LAT_WIDTH), full(qg), full(kvg), full(wqn), full(wqp), full(wkn), full(wvt),
                  table_spec, table_spec, table_spec],
        out_specs=[row_spec(MLA_HEADS * MLA_SLAB), row_spec(MLA_HEADS * MLA_SLAB),
                   pl.BlockSpec((width, tm), lambda i: (0, i))],
        out_shape=[jax.ShapeDtypeStruct((rows, MLA_HEADS * MLA_SLAB), BF16),
                   jax.ShapeDtypeStruct((rows, MLA_HEADS * MLA_SLAB), BF16),
                   jax.ShapeDtypeStruct((width, rows), BF16)],
        compiler_params=_params(("parallel",)),
        name="latent",
    )(lat, qg, kvg, wqn, wqp, wkn, wvt, *tables)


def _mla_kernel(n_kv, tk, q_ref, k_ref, vt_ref, o_ref, acc_ref, s_ref, mblk_ref):
    n_sub, _, ts = acc_ref.shape
    acc_ref[...] = jnp.zeros_like(acc_ref)

    def scores_into(slot, j):
        k = k_ref[pl.ds(pl.multiple_of(j * tk, tk), tk), :]
        for t in range(n_sub):
            s = lax.dot_general(k, q_ref[pl.ds(t * ts, ts), :], _NT, preferred_element_type=F32)
            s_ref[slot, t] = s
            mblk_ref[slot, t] = jnp.max(s, axis=0, keepdims=True)

    def consume(slot, j, carry):
        vt = vt_ref[:, pl.ds(pl.multiple_of(j * tk, tk), tk)]
        out = []
        for t in range(n_sub):
            m_prev, l_prev = carry[t]
            m_new = jnp.maximum(m_prev, mblk_ref[slot, t])
            alpha = jnp.exp2(m_prev - m_new)
            p = jnp.exp2(s_ref[slot, t] - m_new)
            l_new = alpha * l_prev + jnp.sum(p, axis=0, keepdims=True)
            acc_ref[t] = alpha * acc_ref[t] + jnp.dot(vt, p.astype(BF16), preferred_element_type=F32)
            out.append((m_new, l_new))
        return tuple(out)

    def body(jj, carry):
        j = 2 * jj
        scores_into(1, j + 1)
        carry = consume(0, j, carry)
        scores_into(0, j + 2)
        return consume(1, j + 1, carry)

    scores_into(0, 0)
    init = tuple((jnp.full((1, ts), -jnp.inf, F32), jnp.zeros((1, ts), F32)) for _ in range(n_sub))
    carry = lax.fori_loop(0, n_kv // 2 - 1, body, init)
    scores_into(1, n_kv - 1)
    carry = consume(0, n_kv - 2, carry)
    carry = consume(1, n_kv - 1, carry)
    for t in range(n_sub):
        o_ref[pl.ds(t * ts, ts), :] = (acc_ref[t] / carry[t][1]).T.astype(o_ref.dtype)


def _mla(q, k, vt, *, batch, seq, tq=2048, ts=512, tk=1024):
    assert seq % tq == 0 and tq % ts == 0 and seq % (2 * tk) == 0
    n_q = seq // tq
    return pl.pallas_call(
        functools.partial(_mla_kernel, seq // tk, tk),
        grid=(batch, MLA_HEADS, n_q),
        in_specs=[
            pl.BlockSpec((tq, MLA_SLAB), lambda b, h, i: (b * n_q + i, h)),
            pl.BlockSpec((seq, MLA_SLAB), lambda b, h, i: (b, h)),
            pl.BlockSpec((V_DIM, seq), lambda b, h, i: (h, b)),
        ],
        out_specs=pl.BlockSpec((tq, V_DIM), lambda b, h, i: (b * n_q + i, h)),
        out_shape=jax.ShapeDtypeStruct((batch * seq, MLA_HEADS * V_DIM), BF16),
        scratch_shapes=[pltpu.VMEM((tq // ts, V_DIM, ts), F32), pltpu.VMEM((2, tq // ts, tk, ts), F32),
                        pltpu.VMEM((2, tq // ts, 1, ts), F32)],
        compiler_params=_params(("parallel", "parallel", "arbitrary")),
        name="mla",
    )(q, k, vt)


def _dilated_kernel(d, half, stream_len, q_ref, kp_ref, kc_ref, kn_ref, vp_ref, vc_ref, vn_ref,
                    o_ref, lse_ref, ks_ref, vs_ref):
    n_heads = q_ref.shape[0]
    tl = DIL_ROWS // d
    i = pl.program_id(2)
    for dst, before, cur, after in ((ks_ref, kp_ref, kc_ref, kn_ref), (vs_ref, vp_ref, vc_ref, vn_ref)):
        dst[:, :, 0:half] = before[...]
        dst[:, :, half:half + tl] = cur[...]
        dst[:, :, half + tl:] = after[...]

    chunk = 2 * half
    row = lax.broadcasted_iota(jnp.int32, (chunk, 2 * chunk), 0)
    col = lax.broadcasted_iota(jnp.int32, (chunk, 2 * chunk), 1)
    band = (col >= row) & (col <= row + 2 * half)
    shift = d.bit_length() - 1
    lane_head = lax.broadcasted_iota(jnp.int32, (chunk, LANES), 1) // LSE_LANES

    def unit_group(ug, carry):
        jobs = []
        for uu in range(DIL_UNITS_PER_TRIP):
            u = ug * DIL_UNITS_PER_TRIP + uu
            r = u & (d - 1)
            c = u >> shift
            row0 = pl.multiple_of(c * chunk, chunk)
            k_idx = i * tl + c * chunk - half + col
            mask = band & (k_idx >= 0) & (k_idx < stream_len)
            for h in range(n_heads):
                q = q_ref[h, r, pl.ds(row0, chunk), :]
                k = ks_ref[h, r, pl.ds(row0, 2 * chunk), :]
                s = lax.dot_general(q, k, _NT, preferred_element_type=F32)
                jobs.append((h, r, row0, mask, s))
        packed = jnp.zeros((chunk, LANES), F32)
        for h, r, row0, mask, s in jobs:
            s = jnp.where(mask, s, NEG_INF)
            m = jnp.max(s, axis=1, keepdims=True)
            p = jnp.exp2(s - m)
            denom = jnp.sum(p, axis=1, keepdims=True)
            o = jnp.dot(p.astype(BF16), vs_ref[h, r, pl.ds(row0, 2 * chunk), :], preferred_element_type=F32) / denom
            start = r + d * row0
            o_ref[h, pl.ds(start, chunk, stride=d), :] = o
            lse = m * LN_2 + jnp.log(denom)
            packed = jnp.where(lane_head == h, lse, packed)
            if h == n_heads - 1:
                lse_ref[pl.ds(start, chunk, stride=d), :] = packed
        return carry

    lax.fori_loop(0, DIL_ROWS // chunk // DIL_UNITS_PER_TRIP, unit_group, 0)


def _dilated_group(qkv, *, group, window, dilation, batch, seq):
    d = dilation
    half = window // (2 * d)
    assert seq % DIL_ROWS == 0 and d & (d - 1) == 0 and DIL_ROWS % (2 * half * d) == 0
    assert (DIL_ROWS // (2 * half)) % DIL_UNITS_PER_TRIP == 0
    n_r = seq // DIL_ROWS
    tl = DIL_ROWS // d
    halo_per_step = tl // half
    n_halo = seq // d // half
    steps_per_group = HEADS_PER_GROUP // DIL_HEADS_PER_STEP
    blk = (DIL_HEADS_PER_STEP, None, d, tl, LANES)
    hblk = (DIL_HEADS_PER_STEP, None, d, half, LANES)

    def specs(kind):
        heads = lambda g: kind * steps_per_group + g
        cur = pl.BlockSpec(blk, lambda g, b, i: (heads(g), b, 0, i, 0))
        before = pl.BlockSpec(hblk, lambda g, b, i: (heads(g), b, 0, jnp.maximum(i * halo_per_step - 1, 0), 0))
        after = pl.BlockSpec(hblk, lambda g, b, i: (heads(g), b, 0,
                                                    jnp.minimum((i + 1) * halo_per_step, n_halo - 1), 0))
        return before, cur, after

    assert DIL_HEADS_PER_STEP == HEADS_PER_GROUP
    return pl.pallas_call(
        functools.partial(_dilated_kernel, d, half, seq // d),
        grid=(steps_per_group, batch, n_r),
        in_specs=[specs(0)[1], *specs(1), *specs(2)],
        out_specs=[pl.BlockSpec((HEADS_PER_GROUP, DIL_ROWS, LANES), lambda g, b, i: (0, b * n_r + i, 0)),
                   pl.BlockSpec((DIL_ROWS, LANES), lambda g, b, i: (b * n_r + i, 0))],
        out_shape=[jax.ShapeDtypeStruct((HEADS_PER_GROUP, batch * seq, LANES), F32),
                   jax.ShapeDtypeStruct((batch * seq, LANES), F32)],
        scratch_shapes=[pltpu.VMEM((DIL_HEADS_PER_STEP, d, tl + 2 * half, LANES), BF16),
                        pltpu.VMEM((DIL_HEADS_PER_STEP, d, tl + 2 * half, LANES), BF16)],
        compiler_params=_params(("parallel", "parallel", "parallel")),
        name="dilated_g%d" % group,
    )(*([qkv] * 7))


def _merge_kernel(oa_ref, o0_ref, o1_ref, o2_ref, l0_ref, l1_ref, l2_ref, ga_ref, gb_ref, y1_ref,
                  wa_ref, wb_ref, wo_ref, g_ref, b_ref, y_ref):
    l0, l1, l2 = l0_ref[...], l1_ref[...], l2_ref[...]
    m = jnp.maximum(jnp.maximum(l0, l1), l2)
    e0, e1, e2 = jnp.exp(l0 - m), jnp.exp(l1 - m), jnp.exp(l2 - m)
    denom = e0 + e1 + e2
    w0, w1, w2 = e0 / denom, e1 / denom, e2 / denom
    heads = []
    for h in range(HEADS_PER_GROUP):
        one = slice(h * LSE_LANES, h * LSE_LANES + 1)
        heads.append((w0[:, one] * o0_ref[h] + w1[:, one] * o1_ref[h] + w2[:, one] * o2_ref[h]).astype(BF16))
    out_a = oa_ref[...]
    out_b = jnp.concatenate(heads, axis=1)
    cols = [slice(c, c + PROJ_SUB) for c in range(0, D_MODEL, PROJ_SUB)]
    branches = [(jnp.dot(out_a, wa_ref[:, sl], preferred_element_type=F32),
                 jnp.dot(out_b, wb_ref[:, sl], preferred_element_type=F32)) for sl in cols]
    mix = None
    for sl, (branch_a, branch_b) in zip(cols, branches):
        merged = (ga_ref[:, sl] * branch_a + gb_ref[:, sl] * branch_b).astype(BF16)
        part = jnp.dot(merged, wo_ref[sl, :], preferred_element_type=F32)
        mix = part if mix is None else mix + part
    y_ref[...] = _layer_norm(ALPHA * y1_ref[...] + mix, g_ref[...], b_ref[...])


def _merge(out_a, outs, lses, gates, y1, wa, wb, wo, g, b, *, tm=256):
    rows = y1.shape[0]
    row_spec = lambda w, c=0: pl.BlockSpec((tm, w), lambda i: (i, c))
    full = lambda a: pl.BlockSpec(a.shape, lambda i: (0, 0), pipeline_mode=pl.Buffered(1))
    return pl.pallas_call(
        _merge_kernel,
        grid=(rows // tm,),
        in_specs=[row_spec(MLA_HEADS * V_DIM)]
                 + [pl.BlockSpec((HEADS_PER_GROUP, tm, LANES), lambda i: (0, i, 0))] * 3 + [row_spec(LANES)] * 3
                 + [row_spec(D_MODEL, 0), row_spec(D_MODEL, 1), row_spec(D_MODEL)]
                 + [full(wa), full(wb), full(wo), full(g), full(b)],
        out_specs=row_spec(D_MODEL),
        out_shape=jax.ShapeDtypeStruct((rows, D_MODEL), F32),
        compiler_params=_params(("parallel",)),
        name="merge",
    )(out_a, *outs, *lses, gates, gates, y1, wa, wb, wo, g, b)


def _ffn_in_kernel(w_ref, wg_ref, wu_ref):
    w = w_ref[...]
    pad = jnp.zeros((w.shape[0], D_FF_PAD - D_FF), BF16)
    for dst, part in ((wg_ref, w[:, :D_FF]), (wu_ref, w[:, D_FF:])):
        dst[:, :D_FF] = part.astype(BF16)
        dst[:, D_FF:] = pad


def _ffn_out_kernel(n_valid, w_ref, o_ref):
    @pl.when(pl.program_id(0) < n_valid)
    def _():
        o_ref[...] = w_ref[...].astype(BF16)

    @pl.when(pl.program_id(0) >= n_valid)
    def _():
        o_ref[...] = jnp.zeros_like(o_ref)


def _ffn_weights(w_in, w_out, *, tr=64, tc=LANES):
    half = jax.ShapeDtypeStruct((D_MODEL, D_FF_PAD), BF16)
    wg, wu = pl.pallas_call(
        _ffn_in_kernel,
        grid=(D_MODEL // tr,),
        in_specs=[pl.BlockSpec((tr, 2 * D_FF), lambda i: (i, 0))],
        out_specs=[pl.BlockSpec((tr, D_FF_PAD), lambda i: (i, 0))] * 2,
        out_shape=[half, half],
        compiler_params=_params(("parallel",)),
        name="ffn_w_in",
    )(w_in)
    n_valid = D_FF // tc
    wo = pl.pallas_call(
        functools.partial(_ffn_out_kernel, n_valid),
        grid=(D_FF_PAD // tc,),
        in_specs=[pl.BlockSpec((tc, D_MODEL), lambda i: (jnp.minimum(i, n_valid - 1), 0))],
        out_specs=pl.BlockSpec((tc, D_MODEL), lambda i: (i, 0)),
        out_shape=jax.ShapeDtypeStruct((D_FF_PAD, D_MODEL), BF16),
        compiler_params=_params(("parallel",)),
        name="ffn_w_out",
    )(w_out)
    return wg, wu, wo


def _mix_in_kernel(w_ref, lat_ref, dil_ref, gate_ref):
    w = w_ref[...]
    o_lat = Q_LORA + KV_LORA + QK_ROPE
    o_v = o_lat + 3 * DIL_WIDTH
    lat_ref[:, :o_lat] = w[:, :o_lat].astype(BF16)
    lat_ref[:, o_lat:] = jnp.zeros((w.shape[0], LAT_WIDTH - o_lat), BF16)
    dil_ref[...] = w[:, o_lat:o_v].astype(BF16)
    gate_ref[...] = w[:, o_v:].astype(BF16)


def _mix_in_weights(w_in_mix, *, tr=64):
    n = w_in_mix.shape[1]
    widths = (LAT_WIDTH, 3 * DIL_WIDTH, 2 * D_MODEL)
    return pl.pallas_call(
        _mix_in_kernel,
        grid=(D_MODEL // tr,),
        in_specs=[pl.BlockSpec((tr, n), lambda i: (i, 0))],
        out_specs=[pl.BlockSpec((tr, w), lambda i: (i, 0)) for w in widths],
        out_shape=[jax.ShapeDtypeStruct((D_MODEL, w), BF16) for w in widths],
        compiler_params=_params(("parallel",)),
        name="mix_w_in",
    )(w_in_mix)


def _rope_tables(seq):
    pos = jnp.arange(seq, dtype=F32)[:, None]

    def angles(d):
        inv_freq = ROPE_THETA ** (-jnp.arange(0, d, 2, dtype=F32) / d)
        ang = pos * inv_freq[None, :]
        return jnp.cos(ang), jnp.sin(ang)

    cos, sin = angles(DIL_HEAD_DIM)
    dil = (jnp.concatenate([cos, cos], axis=1), jnp.concatenate([-sin, sin], axis=1))
    cos, sin = angles(QK_ROPE)
    zeros = jnp.zeros_like(cos)
    pad = jnp.zeros((seq, LANES - QK_ROPE), F32)
    mla = (jnp.concatenate([cos, cos, pad], axis=1),
           jnp.concatenate([-sin, zeros, pad], axis=1),
           jnp.concatenate([zeros, sin, pad], axis=1))
    return dil, mla


def _prepare(ffn1_w_in, ffn1_w_out, ln1_g, ln1_b, w_in_mix, b_gate, q_norm_g, w_uq, kv_norm_g, w_ukv,
             w_branch_a, w_branch_b, w_out_mix, ln2_g, ln2_b, ffn2_w_in, ffn2_w_out, ln3_g, ln3_b):
    ffn = _ffn_weights
    row = lambda a: a.reshape(1, -1)
    w_lat, w_dil, w_gate = _mix_in_weights(w_in_mix)
    uq = w_uq.reshape(Q_LORA, MLA_HEADS, QK_NOPE + QK_ROPE)
    ukv = w_ukv.reshape(KV_LORA, MLA_HEADS, QK_NOPE + V_DIM)
    flat = lambda a: a.reshape(a.shape[0], -1).astype(BF16)
    return dict(
        ffn1=ffn(ffn1_w_in, ffn1_w_out), ln1=(row(ln1_g), row(ln1_b)),
        ffn2=ffn(ffn2_w_in, ffn2_w_out), ln3=(row(ln3_g), row(ln3_b)),
        w_lat=w_lat, w_dil=w_dil, w_gate=w_gate,
        b_gate=row(b_gate),
        q_norm_g=row(q_norm_g), kv_norm_g=row(kv_norm_g),
        wqn=flat(uq[:, :, :QK_NOPE]),
        wqp=flat(jnp.pad(uq[:, :, QK_NOPE:], ((0, 0), (0, 0), (0, LANES - QK_ROPE)))),
        wkn=flat(ukv[:, :, :QK_NOPE]),
        wvt=flat(ukv[:, :, QK_NOPE:]).T,
        wa=w_branch_a.astype(BF16), wb=w_branch_b.astype(BF16), wo=w_out_mix.astype(BF16),
        ln2=(row(ln2_g), row(ln2_b)),
    )


def _encoder_layer(x, p, tables):
    batch, seq, _ = x.shape
    x2 = x.reshape(batch * seq, D_MODEL)
    dil_tables, mla_tables = tables

    y1 = _ffn_ln(x2, *p["ffn1"], *p["ln1"])

    lat, y1_bf = _proj(y1, p["w_lat"], F32, tn=LAT_WIDTH)
    qkv_d = _proj(y1_bf, p["w_dil"], BF16, mode="dilated", extras=dil_tables, seq=seq, tn=DIL_WIDTH)
    gates = _proj(y1_bf, p["w_gate"], F32, mode="gate", extras=(p["b_gate"],), tn=D_MODEL)

    q, k, vt = _latent(lat, p["q_norm_g"], p["kv_norm_g"], p["wqn"], p["wqp"], p["wkn"], p["wvt"],
                       mla_tables, seq=seq)
    out_a = _mla(q, k, vt, batch=batch, seq=seq)

    outs, lses = [], []
    for group, (window, dilation) in enumerate(DIL_PATTERNS):
        o, lse = _dilated_group(qkv_d[group], group=group, window=window, dilation=dilation, batch=batch, seq=seq)
        outs.append(o)
        lses.append(lse)

    y2 = _merge(out_a, outs, lses, gates, y1, p["wa"], p["wb"], p["wo"], *p["ln2"])
    y3 = _ffn_ln(y2, *p["ffn2"], *p["ln3"])
    return y3.reshape(batch, seq, D_MODEL)


def kernel(x_prompt, x_sample, ffn1_w_in, ffn1_w_out, ln1_g, ln1_b, w_in_mix, b_gate, q_norm_g, w_uq, kv_norm_g, w_ukv, w_branch_a, w_branch_b, w_out_mix, ln2_g, ln2_b, ffn2_w_in, ffn2_w_out, ln3_g, ln3_b):
    weights = (ffn1_w_in, ffn1_w_out, ln1_g, ln1_b, w_in_mix, b_gate, q_norm_g, w_uq, kv_norm_g, w_ukv,
               w_branch_a, w_branch_b, w_out_mix, ln2_g, ln2_b, ffn2_w_in, ffn2_w_out, ln3_g, ln3_b)
    y_prompt, y_sample = x_prompt, x_sample
    tables = _rope_tables(max(x_prompt.shape[1], x_sample.shape[1]))
    for layer in range(DEPTH):
        p = _prepare(*(w[layer] for w in weights))
        y_prompt = _encoder_layer(y_prompt, p, tables)
        y_sample = _encoder_layer(y_sample, p, tables)
    return (y_prompt, y_sample)
```

```python
import functools

import jax
import jax.numpy as jnp
from jax import lax
from jax.experimental import pallas as pl
from jax.experimental.pallas import tpu as pltpu

D_MODEL = 2048
DEPTH = 1
MLA_HEADS = 8
Q_LORA = 512
KV_LORA = 512
QK_NOPE = 128
QK_ROPE = 64
V_DIM = 128
DIL_PATTERNS = ((128, 1), (512, 4), (2048, 16))
HEADS_PER_GROUP = 4
DIL_HEAD_DIM = 128
DIL_WIDTH = len(DIL_PATTERNS) * HEADS_PER_GROUP * DIL_HEAD_DIM
D_FF = 5504
ROPE_THETA = 10000.0
LN_EPS = 1e-5
RMS_EPS = 1e-6
NEG_INF = -1e30
ALPHA = (2 * DEPTH) ** 0.25
MLA_SCALE = (QK_NOPE + QK_ROPE) ** -0.5
LOG2_E = 1.4426950408889634
MLA_Q_SCALE = MLA_SCALE * LOG2_E
DIL_KINDS = 3
DIL_Q_SCALE = DIL_HEAD_DIM ** -0.5 * LOG2_E
LN_2 = 1.0 / LOG2_E

LANES = 128
D_FF_PAD = 5632
FF_CHUNK = 512
LAT_WIDTH = Q_LORA + KV_LORA + LANES
MLA_SLAB = 2 * LANES
GROUP_WIDTH = HEADS_PER_GROUP * DIL_HEAD_DIM
LSE_LANES = LANES // HEADS_PER_GROUP
PROJ_SUB = 512
DIL_ROWS = 2048
DIL_HEADS_PER_STEP = 4
DIL_UNITS_PER_TRIP = 2
VMEM_LIMIT = 56 * 1024 * 1024
FFN_VMEM_LIMIT = 60 * 1024 * 1024

BF16 = jnp.bfloat16
F32 = jnp.float32
_NT = (((1,), (1,)), ((), ()))


def _params(semantics, vmem_limit=VMEM_LIMIT):
    return pltpu.CompilerParams(dimension_semantics=semantics, vmem_limit_bytes=vmem_limit)


def _layer_norm(v, g, b, scale=1.0):
    mu = jnp.mean(v, axis=-1, keepdims=True)
    c = v - mu
    var = jnp.mean(c * c, axis=-1, keepdims=True)
    return c * (scale * lax.rsqrt(scale * scale * var + LN_EPS)) * g + b


def _ffn_ln_kernel(n_chunks, x_ref, wg_ref, wu_ref, wo_ref, g_ref, b_ref, y_ref, xbf_ref):
    k = pl.program_id(1)

    def chunk(xb):
        sub = FF_CHUNK // 2
        acts = []
        for c in range(2):
            gate = jnp.dot(xb, wg_ref[:, c * sub:(c + 1) * sub], preferred_element_type=F32)
            up = jnp.dot(xb, wu_ref[:, c * sub:(c + 1) * sub], preferred_element_type=F32)
            acts.append((gate * jax.nn.sigmoid(gate) * up).astype(BF16))
        return (jnp.dot(acts[0], wo_ref[:sub, :], preferred_element_type=F32)
                + jnp.dot(acts[1], wo_ref[sub:, :], preferred_element_type=F32))

    @pl.when(k == 0)
    def _():
        x = x_ref[...]
        xb = x.astype(BF16)
        xbf_ref[...] = xb
        y_ref[...] = (2.0 * ALPHA) * x + chunk(xb)

    @pl.when(k > 0)
    def _():
        y_ref[...] += chunk(xbf_ref[...])

    @pl.when(k == n_chunks - 1)
    def _():
        y_ref[...] = _layer_norm(y_ref[...], g_ref[...], b_ref[...], scale=0.5)


def _ffn_ln(x, wg, wu, wo, g, b, *, tm=1024):
    rows = x.shape[0]
    n_chunks = D_FF_PAD // FF_CHUNK
    return pl.pallas_call(
        functools.partial(_ffn_ln_kernel, n_chunks),
        grid=(rows // tm, n_chunks),
        in_specs=[
            pl.BlockSpec((tm, D_MODEL), lambda i, k: (i, 0)),
            pl.BlockSpec((D_MODEL, FF_CHUNK), lambda i, k: (0, k)),
            pl.BlockSpec((D_MODEL, FF_CHUNK), lambda i, k: (0, k)),
            pl.BlockSpec((FF_CHUNK, D_MODEL), lambda i, k: (k, 0)),
            pl.BlockSpec((1, D_MODEL), lambda i, k: (0, 0)),
            pl.BlockSpec((1, D_MODEL), lambda i, k: (0, 0)),
        ],
        out_specs=pl.BlockSpec((tm, D_MODEL), lambda i, k: (i, 0)),
        out_shape=jax.ShapeDtypeStruct((rows, D_MODEL), F32),
        scratch_shapes=[pltpu.VMEM((tm, D_MODEL), BF16)],
        compiler_params=_params(("parallel", "arbitrary"), FFN_VMEM_LIMIT),
        name="ffn_ln",
    )(x, wg, wu, wo, g, b)


def _proj_kernel(mode, x_ref, w_ref, *rest):
    tn = w_ref.shape[1]
    starts = range(0, tn, PROJ_SUB)

    def products(x=None):
        x = x_ref[...] if x is None else x
        return [(c, jnp.dot(x, w_ref[:, c:min(c + PROJ_SUB, tn)], preferred_element_type=F32)) for c in starts]

    if mode == "dilated":
        cos_ref, sin_ref, *o_refs, slab_ref = rest
        n_rope_steps = 2 * DIL_WIDTH // tn
        tm = x_ref.shape[0]

        def emit(rope):
            if rope:
                scale = jnp.where(pl.program_id(1) == 0, DIL_Q_SCALE, 1.0)
                cos = cos_ref[...] * scale
                sin = sin_ref[...] * scale
            for c, z in products():
                group = c // PROJ_SUB
                d = DIL_PATTERNS[group][1]
                o_ref = o_refs[group]
                for h in range(HEADS_PER_GROUP):
                    zh = z[:, h * LANES:(h + 1) * LANES]
                    if rope:
                        zh = zh * cos + pltpu.roll(zh, LANES // 2, 1) * sin
                    if d == 1:
                        o_ref[h, 0] = zh.astype(BF16)
                        continue
                    slab = group * HEADS_PER_GROUP + h
                    slab_ref[slab] = zh
                    for r in range(d):
                        o_ref[h, r] = slab_ref[slab, pl.ds(r, tm // d, stride=d), :].astype(BF16)

        pl.when(pl.program_id(1) < n_rope_steps)(functools.partial(emit, True))
        pl.when(pl.program_id(1) >= n_rope_steps)(functools.partial(emit, False))
    elif mode == "gate":
        b_ref, o_ref = rest
        for c, z in products():
            sl = slice(c, c + z.shape[1])
            o_ref[:, sl] = jax.nn.sigmoid(z + b_ref[:, sl]).astype(o_ref.dtype)
    else:
        o_ref, xbf_ref = rest
        x = x_ref[...].astype(BF16)
        xbf_ref[...] = x
        for c, z in products(x):
            o_ref[:, c:c + z.shape[1]] = z.astype(o_ref.dtype)


def _proj(x, w, out_dtype, *, mode="cast", extras=(), seq=None, tm=1024, tn=512):
    rows = x.shape[0]
    n = w.shape[1]
    tn = min(tn, n)
    assert rows % tm == 0 and n % tn == 0
    in_specs = [
        pl.BlockSpec((tm, D_MODEL), lambda i, j: (i, 0)),
        pl.BlockSpec((D_MODEL, tn), lambda i, j: (0, j)),
    ]
    out_specs = pl.BlockSpec((tm, tn), lambda i, j: (i, j))
    out_shape = jax.ShapeDtypeStruct((rows, n), out_dtype)
    scratch_shapes = []
    if mode == "dilated":
        assert tn == DIL_WIDTH and PROJ_SUB == GROUP_WIDTH
        tiles_per_seq = seq // tm
        in_specs += [pl.BlockSpec((tm, LANES), lambda i, j: (i % tiles_per_seq, 0))] * 2
        out_specs = [
            pl.BlockSpec((HEADS_PER_GROUP, None, d, tm // d, LANES),
                         lambda i, j: (j, i // tiles_per_seq, 0, i % tiles_per_seq, 0))
            for _, d in DIL_PATTERNS]
        out_shape = [jax.ShapeDtypeStruct((DIL_KINDS * HEADS_PER_GROUP, rows // seq, d, seq // d, LANES), out_dtype)
                     for _, d in DIL_PATTERNS]
        scratch_shapes = [pltpu.VMEM((len(DIL_PATTERNS) * HEADS_PER_GROUP, tm, LANES), F32)]
    elif mode == "gate":
        in_specs += [pl.BlockSpec((1, tn), lambda i, j: (0, j))]
    else:
        assert tn == n
        out_specs = [out_specs, pl.BlockSpec((tm, D_MODEL), lambda i, j: (i, 0))]
        out_shape = [out_shape, jax.ShapeDtypeStruct((rows, D_MODEL), BF16)]
    return pl.pallas_call(
        functools.partial(_proj_kernel, mode),
        grid=(rows // tm, n // tn),
        in_specs=in_specs,
        out_specs=out_specs,
        out_shape=out_shape,
        scratch_shapes=scratch_shapes,
        compiler_params=_params(("parallel", "arbitrary")),
        name="proj_" + mode,
    )(x, w, *extras)


def _latent_kernel(lat_ref, qg_ref, kvg_ref, wqn_ref, wqp_ref, wkn_ref, wvt_ref,
                   cos_ref, sin_lo_ref, sin_hi_ref, q_ref, k_ref, vt_ref):
    lat = lat_ref[...]
    c_q = lat[:, :Q_LORA]
    c_kv = lat[:, Q_LORA:Q_LORA + KV_LORA]
    k_rope = lat[:, Q_LORA + KV_LORA:]

    def rms(x, g):
        return (x * lax.rsqrt(jnp.mean(x * x, axis=-1, keepdims=True) + RMS_EPS) * g).astype(BF16)

    cos = cos_ref[...]
    sin_lo = sin_lo_ref[...]
    sin_hi = sin_hi_ref[...]

    def rope64(x):
        quarter = QK_ROPE // 2
        return (x * cos + pltpu.roll(x, LANES - quarter, 1) * sin_lo
                + pltpu.roll(x, quarter, 1) * sin_hi)

    nq = rms(c_q, qg_ref[...])
    nkv = rms(c_kv, kvg_ref[...])
    q_nope = jnp.dot(nq, wqn_ref[...], preferred_element_type=F32)
    q_pe = jnp.dot(nq, wqp_ref[...], preferred_element_type=F32)
    k_nope = jnp.dot(nkv, wkn_ref[...], preferred_element_type=F32)
    vt = lax.dot_general(wvt_ref[...], nkv, _NT, preferred_element_type=F32)
    k_pe = rope64(k_rope).astype(BF16)
    for h in range(MLA_HEADS):
        src = slice(h * LANES, (h + 1) * LANES)
        lo = slice(h * MLA_SLAB, h * MLA_SLAB + LANES)
        hi = slice(h * MLA_SLAB + LANES, (h + 1) * MLA_SLAB)
        q_ref[:, lo] = (q_nope[:, src] * MLA_Q_SCALE).astype(BF16)
        q_ref[:, hi] = (rope64(q_pe[:, src]) * MLA_Q_SCALE).astype(BF16)
        k_ref[:, lo] = k_nope[:, src].astype(BF16)
        k_ref[:, hi] = k_pe
    vt_ref[...] = vt.astype(BF16)


def _latent(lat, qg, kvg, wqn, wqp, wkn, wvt, tables, *, seq, tm=1024):
    rows = lat.shape[0]
    tiles_per_seq = seq // tm
    width = MLA_HEADS * LANES
    row_spec = lambda w: pl.BlockSpec((tm, w), lambda i: (i, 0))
    full = lambda a: pl.BlockSpec(a.shape, lambda i: (0, 0), pipeline_mode=pl.Buffered(1))
    table_spec = pl.BlockSpec((tm, LANES), lambda i: (i % tiles_per_seq, 0))
    return pl.pallas_call(
        _latent_kernel,
        grid=(rows // tm,),
        in_specs=[row_spec(LAT_WIDTH), full(qg), full(kvg), full(wqn), full(wqp), full(wkn), full(wvt),
                  table_spec, table_spec, table_spec],
        out_specs=[row_spec(MLA_HEADS * MLA_SLAB), row_spec(MLA_HEADS * MLA_SLAB),
                   pl.BlockSpec((width, tm), lambda i: (0, i))],
        out_shape=[jax.ShapeDtypeStruct((rows, MLA_HEADS * MLA_SLAB), BF16),
                   jax.ShapeDtypeStruct((rows, MLA_HEADS * MLA_SLAB), BF16),
                   jax.ShapeDtypeStruct((width, rows), BF16)],
        compiler_params=_params(("parallel",)),
        name="latent",
    )(lat, qg, kvg, wqn, wqp, wkn, wvt, *tables)


def _mla_kernel(n_kv, tk, q_ref, k_ref, vt_ref, o_ref, acc_ref, s_ref, mblk_ref):
    n_sub, _, ts = acc_ref.shape
    acc_ref[...] = jnp.zeros_like(acc_ref)

    def scores_into(slot, j):
        k = k_ref[pl.ds(pl.multiple_of(j * tk, tk), tk), :]
        for t in range(n_sub):
            s = lax.dot_general(k, q_ref[pl.ds(t * ts, ts), :], _NT, preferred_element_type=F32)
            s_ref[slot, t] = s
            mblk_ref[slot, t] = jnp.max(s, axis=0, keepdims=True)

    def consume(slot, j, carry):
        vt = vt_ref[:, pl.ds(pl.multiple_of(j * tk, tk), tk)]
        out = []
        for t in range(n_sub):
            m_prev, l_prev = carry[t]
            m_new = jnp.maximum(m_prev, mblk_ref[slot, t])
            alpha = jnp.exp2(m_prev - m_new)
            p = jnp.exp2(s_ref[slot, t] - m_new)
            l_new = alpha * l_prev + jnp.sum(p, axis=0, keepdims=True)
            acc_ref[t] = alpha * acc_ref[t] + jnp.dot(vt, p.astype(BF16), preferred_element_type=F32)
            out.append((m_new, l_new))
        return tuple(out)

    def body(jj, carry):
        j = 2 * jj
        scores_into(1, j + 1)
        carry = consume(0, j, carry)
        scores_into(0, j + 2)
        return consume(1, j + 1, carry)

    scores_into(0, 0)
    init = tuple((jnp.full((1, ts), -jnp.inf, F32), jnp.zeros((1, ts), F32)) for _ in range(n_sub))
    carry = lax.fori_loop(0, n_kv // 2 - 1, body, init)
    scores_into(1, n_kv - 1)
    carry = consume(0, n_kv - 2, carry)
    carry = consume(1, n_kv - 1, carry)
    for t in range(n_sub):
        o_ref[pl.ds(t * ts, ts), :] = (acc_ref[t] / carry[t][1]).T.astype(o_ref.dtype)


def _mla(q, k, vt, *, batch, seq, tq=2048, ts=512, tk=1024):
    assert seq % tq == 0 and tq % ts == 0 and seq % (2 * tk) == 0
    n_q = seq // tq
    return pl.pallas_call(
        functools.partial(_mla_kernel, seq // tk, tk),
        grid=(batch, MLA_HEADS, n_q),
        in_specs=[
            pl.BlockSpec((tq, MLA_SLAB), lambda b, h, i: (b * n_q + i, h)),
            pl.BlockSpec((seq, MLA_SLAB), lambda b, h, i: (b, h)),
            pl.BlockSpec((V_DIM, seq), lambda b, h, i: (h, b)),
        ],
        out_specs=pl.BlockSpec((tq, V_DIM), lambda b, h, i: (b * n_q + i, h)),
        out_shape=jax.ShapeDtypeStruct((batch * seq, MLA_HEADS * V_DIM), BF16),
        scratch_shapes=[pltpu.VMEM((tq // ts, V_DIM, ts), F32), pltpu.VMEM((2, tq // ts, tk, ts), F32),
                        pltpu.VMEM((2, tq // ts, 1, ts), F32)],
        compiler_params=_params(("parallel", "parallel", "arbitrary")),
        name="mla",
    )(q, k, vt)


def _dilated_kernel(d, half, stream_len, q_ref, kp_ref, kc_ref, kn_ref, vp_ref, vc_ref, vn_ref,
                    o_ref, lse_ref, ks_ref, vs_ref):
    n_heads = q_ref.shape[0]
    tl = DIL_ROWS // d
    i = pl.program_id(2)
    for dst, before, cur, after in ((ks_ref, kp_ref, kc_ref, kn_ref), (vs_ref, vp_ref, vc_ref, vn_ref)):
        dst[:, :, 0:half] = before[...]
        dst[:, :, half:half + tl] = cur[...]
        dst[:, :, half + tl:] = after[...]

    chunk = 2 * half
    row = lax.broadcasted_iota(jnp.int32, (chunk, 2 * chunk), 0)
    col = lax.broadcasted_iota(jnp.int32, (chunk, 2 * chunk), 1)
    band = (col >= row) & (col <= row + 2 * half)
    shift = d.bit_length() - 1
    lane_head = lax.broadcasted_iota(jnp.int32, (chunk, LANES), 1) // LSE_LANES

    def unit_group(ug, carry):
        jobs = []
        for uu in range(DIL_UNITS_PER_TRIP):
            u = ug * DIL_UNITS_PER_TRIP + uu
            r = u & (d - 1)
            c = u >> shift
            row0 = pl.multiple_of(c * chunk, chunk)
            k_idx = i * tl + c * chunk - half + col
            mask = band & (k_idx >= 0) & (k_idx < stream_len)
            for h in range(n_heads):
                q = q_ref[h, r, pl.ds(row0, chunk), :]
                k = ks_ref[h, r, pl.ds(row0, 2 * chunk), :]
                s = lax.dot_general(q, k, _NT, preferred_element_type=F32)
                jobs.append((h, r, row0, mask, s))
        packed = jnp.zeros((chunk, LANES), F32)
        for h, r, row0, mask, s in jobs:
            s = jnp.where(mask, s, NEG_INF)
            m = jnp.max(s, axis=1, keepdims=True)
            p = jnp.exp2(s - m)
            denom = jnp.sum(p, axis=1, keepdims=True)
            o = jnp.dot(p.astype(BF16), vs_ref[h, r, pl.ds(row0, 2 * chunk), :], preferred_element_type=F32) / denom
            start = r + d * row0
            o_ref[h, pl.ds(start, chunk, stride=d), :] = o
            lse = m * LN_2 + jnp.log(denom)
            packed = jnp.where(lane_head == h, lse, packed)
            if h == n_heads - 1:
                lse_ref[pl.ds(start, chunk, stride=d), :] = packed
        return carry

    lax.fori_loop(0, DIL_ROWS // chunk // DIL_UNITS_PER_TRIP, unit_group, 0)


def _dilated_group(qkv, *, group, window, dilation, batch, seq):
    d = dilation
    half = window // (2 * d)
    assert seq % DIL_ROWS == 0 and d & (d - 1) == 0 and DIL_ROWS % (2 * half * d) == 0
    assert (DIL_ROWS // (2 * half)) % DIL_UNITS_PER_TRIP == 0
    n_r = seq // DIL_ROWS
    tl = DIL_ROWS // d
    halo_per_step = tl // half
    n_halo = seq // d // half
    steps_per_group = HEADS_PER_GROUP // DIL_HEADS_PER_STEP
    blk = (DIL_HEADS_PER_STEP, None, d, tl, LANES)
    hblk = (DIL_HEADS_PER_STEP, None, d, half, LANES)

    def specs(kind):
        heads = lambda g: kind * steps_per_group + g
        cur = pl.BlockSpec(blk, lambda g, b, i: (heads(g), b, 0, i, 0))
        before = pl.BlockSpec(hblk, lambda g, b, i: (heads(g), b, 0, jnp.maximum(i * halo_per_step - 1, 0), 0))
        after = pl.BlockSpec(hblk, lambda g, b, i: (heads(g), b, 0,
                                                    jnp.minimum((i + 1) * halo_per_step, n_halo - 1), 0))
        return before, cur, after

    assert DIL_HEADS_PER_STEP == HEADS_PER_GROUP
    return pl.pallas_call(
        functools.partial(_dilated_kernel, d, half, seq // d),
        grid=(steps_per_group, batch, n_r),
        in_specs=[specs(0)[1], *specs(1), *specs(2)],
        out_specs=[pl.BlockSpec((HEADS_PER_GROUP, DIL_ROWS, LANES), lambda g, b, i: (0, b * n_r + i, 0)),
                   pl.BlockSpec((DIL_ROWS, LANES), lambda g, b, i: (b * n_r + i, 0))],
        out_shape=[jax.ShapeDtypeStruct((HEADS_PER_GROUP, batch * seq, LANES), F32),
                   jax.ShapeDtypeStruct((batch * seq, LANES), F32)],
        scratch_shapes=[pltpu.VMEM((DIL_HEADS_PER_STEP, d, tl + 2 * half, LANES), BF16),
                        pltpu.VMEM((DIL_HEADS_PER_STEP, d, tl + 2 * half, LANES), BF16)],
        compiler_params=_params(("parallel", "parallel", "parallel")),
        name="dilated_g%d" % group,
    )(*([qkv] * 7))


def _merge_kernel(oa_ref, o0_ref, o1_ref, o2_ref, l0_ref, l1_ref, l2_ref, ga_ref, gb_ref, y1_ref,
                  wa_ref, wb_ref, wo_ref, g_ref, b_ref, y_ref):
    l0, l1, l2 = l0_ref[...], l1_ref[...], l2_ref[...]
    m = jnp.maximum(jnp.maximum(l0, l1), l2)
    e0, e1, e2 = jnp.exp(l0 - m), jnp.exp(l1 - m), jnp.exp(l2 - m)
    denom = e0 + e1 + e2
    w0, w1, w2 = e0 / denom, e1 / denom, e2 / denom
    heads = []
    for h in range(HEADS_PER_GROUP):
        one = slice(h * LSE_LANES, h * LSE_LANES + 1)
        heads.append((w0[:, one] * o0_ref[h] + w1[:, one] * o1_ref[h] + w2[:, one] * o2_ref[h]).astype(BF16))
    out_a = oa_ref[...]
    out_b = jnp.concatenate(heads, axis=1)
    cols = [slice(c, c + PROJ_SUB) for c in range(0, D_MODEL, PROJ_SUB)]
    branches = [(jnp.dot(out_a, wa_ref[:, sl], preferred_element_type=F32),
                 jnp.dot(out_b, wb_ref[:, sl], preferred_element_type=F32)) for sl in cols]
    mix = None
    for sl, (branch_a, branch_b) in zip(cols, branches):
        merged = (ga_ref[:, sl] * branch_a + gb_ref[:, sl] * branch_b).astype(BF16)
        part = jnp.dot(merged, wo_ref[sl, :], preferred_element_type=F32)
        mix = part if mix is None else mix + part
    y_ref[...] = _layer_norm(ALPHA * y1_ref[...] + mix, g_ref[...], b_ref[...])


def _merge(out_a, outs, lses, gates, y1, wa, wb, wo, g, b, *, tm=256):
    rows = y1.shape[0]
    row_spec = lambda w, c=0: pl.BlockSpec((tm, w), lambda i: (i, c))
    full = lambda a: pl.BlockSpec(a.shape, lambda i: (0, 0), pipeline_mode=pl.Buffered(1))
    return pl.pallas_call(
        _merge_kernel,
        grid=(rows // tm,),
        in_specs=[row_spec(MLA_HEADS * V_DIM)]
                 + [pl.BlockSpec((HEADS_PER_GROUP, tm, LANES), lambda i: (0, i, 0))] * 3 + [row_spec(LANES)] * 3
                 + [row_spec(D_MODEL, 0), row_spec(D_MODEL, 1), row_spec(D_MODEL)]
                 + [full(wa), full(wb), full(wo), full(g), full(b)],
        out_specs=row_spec(D_MODEL),
        out_shape=jax.ShapeDtypeStruct((rows, D_MODEL), F32),
        compiler_params=_params(("parallel",)),
        name="merge",
    )(out_a, *outs, *lses, gates, gates, y1, wa, wb, wo, g, b)


def _ffn_in_kernel(w_ref, wg_ref, wu_ref):
    w = w_ref[...]
    pad = jnp.zeros((w.shape[0], D_FF_PAD - D_FF), BF16)
    for dst, part in ((wg_ref, w[:, :D_FF]), (wu_ref, w[:, D_FF:])):
        dst[:, :D_FF] = part.astype(BF16)
        dst[:, D_FF:] = pad


def _ffn_out_kernel(n_valid, w_ref, o_ref):
    @pl.when(pl.program_id(0) < n_valid)
    def _():
        o_ref[...] = w_ref[...].astype(BF16)

    @pl.when(pl.program_id(0) >= n_valid)
    def _():
        o_ref[...] = jnp.zeros_like(o_ref)


def _ffn_weights(w_in, w_out, *, tr=64, tc=LANES):
    half = jax.ShapeDtypeStruct((D_MODEL, D_FF_PAD), BF16)
    wg, wu = pl.pallas_call(
        _ffn_in_kernel,
        grid=(D_MODEL // tr,),
        in_specs=[pl.BlockSpec((tr, 2 * D_FF), lambda i: (i, 0))],
        out_specs=[pl.BlockSpec((tr, D_FF_PAD), lambda i: (i, 0))] * 2,
        out_shape=[half, half],
        compiler_params=_params(("parallel",)),
        name="ffn_w_in",
    )(w_in)
    n_valid = D_FF // tc
    wo = pl.pallas_call(
        functools.partial(_ffn_out_kernel, n_valid),
        grid=(D_FF_PAD // tc,),
        in_specs=[pl.BlockSpec((tc, D_MODEL), lambda i: (jnp.minimum(i, n_valid - 1), 0))],
        out_specs=pl.BlockSpec((tc, D_MODEL), lambda i: (i, 0)),
        out_shape=jax.ShapeDtypeStruct((D_FF_PAD, D_MODEL), BF16),
        compiler_params=_params(("parallel",)),
        name="ffn_w_out",
    )(w_out)
    return wg, wu, wo


def _mix_in_kernel(w_ref, lat_ref, dil_ref, gate_ref):
    w = w_ref[...]
    o_lat = Q_LORA + KV_LORA + QK_ROPE
    o_v = o_lat + 3 * DIL_WIDTH
    lat_ref[:, :o_lat] = w[:, :o_lat].astype(BF16)
    lat_ref[:, o_lat:] = jnp.zeros((w.shape[0], LAT_WIDTH - o_lat), BF16)
    dil_ref[...] = w[:, o_lat:o_v].astype(BF16)
    gate_ref[...] = w[:, o_v:].astype(BF16)


def _mix_in_weights(w_in_mix, *, tr=64):
    n = w_in_mix.shape[1]
    widths = (LAT_WIDTH, 3 * DIL_WIDTH, 2 * D_MODEL)
    return pl.pallas_call(
        _mix_in_kernel,
        grid=(D_MODEL // tr,),
        in_specs=[pl.BlockSpec((tr, n), lambda i: (i, 0))],
        out_specs=[pl.BlockSpec((tr, w), lambda i: (i, 0)) for w in widths],
        out_shape=[jax.ShapeDtypeStruct((D_MODEL, w), BF16) for w in widths],
        compiler_params=_params(("parallel",)),
        name="mix_w_in",
    )(w_in_mix)


def _rope_tables(seq):
    pos = jnp.arange(seq, dtype=F32)[:, None]

    def angles(d):
        inv_freq = ROPE_THETA ** (-jnp.arange(0, d, 2, dtype=F32) / d)
        ang = pos * inv_freq[None, :]
        return jnp.cos(ang), jnp.sin(ang)

    cos, sin = angles(DIL_HEAD_DIM)
    dil = (jnp.concatenate([cos, cos], axis=1), jnp.concatenate([-sin, sin], axis=1))
    cos, sin = angles(QK_ROPE)
    zeros = jnp.zeros_like(cos)
    pad = jnp.zeros((seq, LANES - QK_ROPE), F32)
    mla = (jnp.concatenate([cos, cos, pad], axis=1),
           jnp.concatenate([-sin, zeros, pad], axis=1),
           jnp.concatenate([zeros, sin, pad], axis=1))
    return dil, mla


def _prepare(ffn1_w_in, ffn1_w_out, ln1_g, ln1_b, w_in_mix, b_gate, q_norm_g, w_uq, kv_norm_g, w_ukv,
             w_branch_a, w_branch_b, w_out_mix, ln2_g, ln2_b, ffn2_w_in, ffn2_w_out, ln3_g, ln3_b):
    ffn = _ffn_weights
    row = lambda a: a.reshape(1, -1)
    w_lat, w_dil, w_gate = _mix_in_weights(w_in_mix)
    uq = w_uq.reshape(Q_LORA, MLA_HEADS, QK_NOPE + QK_ROPE)
    ukv = w_ukv.reshape(KV_LORA, MLA_HEADS, QK_NOPE + V_DIM)
    flat = lambda a: a.reshape(a.shape[0], -1).astype(BF16)
    return dict(
        ffn1=ffn(ffn1_w_in, ffn1_w_out), ln1=(row(ln1_g), row(ln1_b)),
        ffn2=ffn(ffn2_w_in, ffn2_w_out), ln3=(row(ln3_g), row(ln3_b)),
        w_lat=w_lat, w_dil=w_dil, w_gate=w_gate,
        b_gate=row(b_gate),
        q_norm_g=row(q_norm_g), kv_norm_g=row(kv_norm_g),
        wqn=flat(uq[:, :, :QK_NOPE]),
        wqp=flat(jnp.pad(uq[:, :, QK_NOPE:], ((0, 0), (0, 0), (0, LANES - QK_ROPE)))),
        wkn=flat(ukv[:, :, :QK_NOPE]),
        wvt=flat(ukv[:, :, QK_NOPE:]).T,
        wa=w_branch_a.astype(BF16), wb=w_branch_b.astype(BF16), wo=w_out_mix.astype(BF16),
        ln2=(row(ln2_g), row(ln2_b)),
    )


def _encoder_layer(x, p, tables):
    batch, seq, _ = x.shape
    x2 = x.reshape(batch * seq, D_MODEL)
    dil_tables, mla_tables = tables

    y1 = _ffn_ln(x2, *p["ffn1"], *p["ln1"])

    lat, y1_bf = _proj(y1, p["w_lat"], F32, tn=LAT_WIDTH)
    qkv_d = _proj(y1_bf, p["w_dil"], BF16, mode="dilated", extras=dil_tables, seq=seq, tn=DIL_WIDTH)
    gates = _proj(y1_bf, p["w_gate"], F32, mode="gate", extras=(p["b_gate"],), tn=D_MODEL)

    q, k, vt = _latent(lat, p["q_norm_g"], p["kv_norm_g"], p["wqn"], p["wqp"], p["wkn"], p["wvt"],
                       mla_tables, seq=seq)
    out_a = _mla(q, k, vt, batch=batch, seq=seq)

    outs, lses = [], []
    for group, (window, dilation) in enumerate(DIL_PATTERNS):
        o, lse = _dilated_group(qkv_d[group], group=group, window=window, dilation=dilation, batch=batch, seq=seq)
        outs.append(o)
        lses.append(lse)

    y2 = _merge(out_a, outs, lses, gates, y1, p["wa"], p["wb"], p["wo"], *p["ln2"])
    y3 = _ffn_ln(y2, *p["ffn2"], *p["ln3"])
    return y3.reshape(batch, seq, D_MODEL)


def kernel(x_prompt, x_sample, ffn1_w_in, ffn1_w_out, ln1_g, ln1_b, w_in_mix, b_gate, q_norm_g, w_uq, kv_norm_g, w_ukv, w_branch_a, w_branch_b, w_out_mix, ln2_g, ln2_b, ffn2_w_in, ffn2_w_out, ln3_g, ln3_b):
    weights = (ffn1_w_in, ffn1_w_out, ln1_g, ln1_b, w_in_mix, b_gate, q_norm_g, w_uq, kv_norm_g, w_ukv,
               w_branch_a, w_branch_b, w_out_mix, ln2_g, ln2_b, ffn2_w_in, ffn2_w_out, ln3_g, ln3_b)
    y_prompt, y_sample = x_prompt, x_sample
    tables = _rope_tables(max(x_prompt.shape[1], x_sample.shape[1]))
    for layer in range(DEPTH):
        p = _prepare(*(w[layer] for w in weights))
        y_prompt = _encoder_layer(y_prompt, p, tables)
        y_sample = _encoder_layer(y_sample, p, tables)
    return (y_prompt, y_sample)
```
